```python
import math
import jax
import jax.numpy as jnp
from jax import lax
import numpy as np

D_MODEL = 1024
BATCH = 16
SEQ = 2048
DEPTH = 4

F32 = jnp.float32
MEM_LEN = 256
HEAD_DIM = 64
BLOCK = 128
SWA_HEADS = 8
SWA_KV_HEADS = 2
WINDOW = 128
S5_WIDTH = 512
S5_GROUP = 16
S5_GROUPS = S5_WIDTH // S5_GROUP
S5_STATE = 64
CONV_WIDTH = 512
CONV_K = 31
DIFF_HEADS = 4
DIFF_V_DIM = 2 * HEAD_DIM
CROSS_HEADS = 4
CROSS_HEAD_DIM = 128
CROSS_WIDTH = CROSS_HEADS * CROSS_HEAD_DIM
FFN_DIM = 2816
N_BRANCHES = 4

A_Q = SWA_HEADS * HEAD_DIM
A_KV = SWA_KV_HEADS * HEAD_DIM
D_QK = DIFF_HEADS * 2 * HEAD_DIM
D_V = DIFF_HEADS * DIFF_V_DIM
CONV_IN = 2 * CONV_WIDTH
GATE_COLS = N_BRANCHES * D_MODEL
MIX_IN = A_Q + 2 * A_KV + 2 * D_QK + D_V + S5_WIDTH + CONV_IN + GATE_COLS
MIX_SPLITS = (A_Q, A_Q + A_KV, A_Q + 2 * A_KV, A_Q + 2 * A_KV + D_QK, A_Q + 2 * A_KV + 2 * D_QK, A_Q + 2 * A_KV + 2 * D_QK + D_V, A_Q + 2 * A_KV + 2 * D_QK + D_V + S5_WIDTH, A_Q + 2 * A_KV + 2 * D_QK + D_V + S5_WIDTH + CONV_IN)

DEEPNORM_ALPHA = (2.0 * DEPTH) ** 0.25
DEEPNORM_BETA = (8.0 * DEPTH) ** -0.25
LN_EPS = 1e-5
NEG_INF = -1e30

kernel_name = 'hybrid_gated_swa_s5_conv_diffattn_deepnorm'


def layer_norm(x, g, b):
    xf = x.astype(F32)
    mu = jnp.mean(xf, axis=-1, keepdims=True)
    var = jnp.mean(jnp.square(xf - mu), axis=-1, keepdims=True)
    return ((xf - mu) * lax.rsqrt(var + LN_EPS) * g.astype(F32) + b.astype(F32)).astype(x.dtype)


def alibi_slopes(n):
    return jnp.asarray([2.0 ** (-8.0 * (h + 1) / n) for h in range(n)], F32)


def swiglu_ffn(x, w_in, w_out):
    gate, up = jnp.split(x @ w_in, 2, axis=-1)
    return (jax.nn.silu(gate) * up) @ w_out


def swa_sink_attention(q, k, v, sinks):
    B, L = q.shape[0], q.shape[1]
    nb = L // BLOCK
    rep = SWA_HEADS // SWA_KV_HEADS
    qb = q.reshape(B, nb, BLOCK, SWA_KV_HEADS, rep, HEAD_DIM)

    def band(t):
        tb = t.reshape(B, nb, BLOCK, SWA_KV_HEADS, HEAD_DIM)
        tp = jnp.concatenate([jnp.zeros_like(tb[:, :1]), tb], axis=1)
        return jnp.concatenate([tp[:, :-1], tp[:, 1:]], axis=2)

    kb, vb = band(k), band(v)
    s = jnp.einsum('bnqgrd,bnkgd->bngrqk', qb.astype(F32), kb.astype(F32)) / math.sqrt(HEAD_DIM)
    blk = jnp.arange(nb)[:, None, None] * BLOCK
    qpos = blk + jnp.arange(BLOCK)[None, :, None]
    kpos = blk - BLOCK + jnp.arange(2 * BLOCK)[None, None, :]
    dist = qpos - kpos
    valid = (dist >= 0) & (dist < WINDOW) & (kpos >= 0)
    slopes = alibi_slopes(SWA_HEADS).reshape(SWA_KV_HEADS, rep)[None, None, :, :, None, None]
    s = s - slopes * dist.astype(F32)[None, :, None, None]
    s = jnp.where(valid[None, :, None, None], s, NEG_INF)
    sink = sinks.astype(F32).reshape(SWA_KV_HEADS, rep)[None, None, :, :, None, None]
    m = jnp.maximum(jnp.max(s, axis=-1, keepdims=True), sink)
    p = jnp.exp(s - m)
    p = p / (jnp.sum(p, axis=-1, keepdims=True) + jnp.exp(sink - m))
    o = jnp.einsum('bngrqk,bnkgd->bnqgrd', p.astype(v.dtype), vb)
    return o.reshape(B, L, A_Q)


def s5_ssm(u, a_re, a_im, log_step, b_re, b_im, c_re, c_im, d_skip, glu_w, glu_b):
    B, L = u.shape[0], u.shape[1]
    uf = u.astype(F32).reshape(B, L, S5_GROUPS, S5_GROUP)
    step = jnp.exp(log_step.astype(F32))[:, None]
    ar, ai = a_re.astype(F32), a_im.astype(F32)
    mag = jnp.exp(ar * step)
    abar_r, abar_i = mag * jnp.cos(ai * step), mag * jnp.sin(ai * step)
    den = ar * ar + ai * ai
    nr, ni = abar_r - 1.0, abar_i
    coef_r = (nr * ar + ni * ai) / den
    coef_i = (ni * ar - nr * ai) / den
    br, bi = b_re.astype(F32), b_im.astype(F32)
    bbar_r = coef_r[..., None] * br - coef_i[..., None] * bi
    bbar_i = coef_r[..., None] * bi + coef_i[..., None] * br
    bu_r = jnp.einsum('blgc,gpc->blgp', uf, bbar_r)
    bu_i = jnp.einsum('blgc,gpc->blgp', uf, bbar_i)
    a_r = jnp.broadcast_to(abar_r[None, None], (1, L, S5_GROUPS, S5_STATE))
    a_i = jnp.broadcast_to(abar_i[None, None], (1, L, S5_GROUPS, S5_STATE))

    def combine(e1, e2):
        a1r, a1i, b1r, b1i = e1
        a2r, a2i, b2r, b2i = e2
        return (a1r * a2r - a1i * a2i, a1r * a2i + a1i * a2r,
                a2r * b1r - a2i * b1i + b2r, a2r * b1i + a2i * b1r + b2i)

    _, _, xr, xi = lax.associative_scan(combine, (a_r, a_i, bu_r, bu_i), axis=1)
    y = jnp.einsum('gcp,blgp->blgc', c_re.astype(F32), xr) - jnp.einsum('gcp,blgp->blgc', c_im.astype(F32), xi)
    y = y.reshape(B, L, S5_WIDTH) + d_skip.astype(F32) * uf.reshape(B, L, S5_WIDTH)
    y = jax.nn.gelu(y).astype(u.dtype)
    return y * jax.nn.sigmoid(y @ glu_w + glu_b)


def conformer_conv(h, conv_w, conv_b, ln_g, ln_b):
    val, gate = jnp.split(h, 2, axis=-1)
    g = val * jax.nn.sigmoid(gate)
    y = lax.conv_general_dilated(g, conv_w[:, None, :], window_strides=(1,), padding=[(CONV_K - 1, 0)],
                                 dimension_numbers=('NWC', 'WIO', 'NWC'), feature_group_count=CONV_WIDTH)
    y = layer_norm(y + conv_b, ln_g, ln_b)
    return jax.nn.silu(y)


def diff_attention(q, k, v, lam, norm_g, lambda_init):
    B, L = q.shape[0], q.shape[1]
    nb = L // BLOCK
    kf = k.astype(F32).reshape(B, L, DIFF_HEADS, 2, HEAD_DIM)
    vh = v.reshape(B, L, DIFF_HEADS, DIFF_V_DIM)
    qblocks = jnp.moveaxis(q.reshape(B, nb, BLOCK, DIFF_HEADS, 2, HEAD_DIM), 1, 0)
    slopes = alibi_slopes(DIFF_HEADS)[None, :, None, None, None]
    kpos = jnp.arange(L)

    def one_block(args):
        qb, n = args
        s = jnp.einsum('bqhcd,bkhcd->bhcqk', qb.astype(F32), kf) / math.sqrt(HEAD_DIM)
        qpos = n * BLOCK + jnp.arange(BLOCK)
        dist = qpos[:, None] - kpos[None, :]
        s = jnp.where(dist >= 0, s - slopes * dist.astype(F32), NEG_INF)
        p = jax.nn.softmax(s, axis=-1)
        w = p[:, :, 0] - lam * p[:, :, 1]
        return jnp.einsum('bhqk,bkhe->bqhe', w.astype(v.dtype), vh)

    o = lax.map(one_block, (qblocks, jnp.arange(nb)))
    o = jnp.moveaxis(o, 0, 1).reshape(B, L, DIFF_HEADS, DIFF_V_DIM).astype(F32)
    o = o * lax.rsqrt(jnp.mean(o * o, axis=-1, keepdims=True) + LN_EPS) * norm_g.astype(F32) * (1.0 - lambda_init)
    return o.astype(v.dtype).reshape(B, L, D_V)


def hybrid_mixer(x, w_in, swa_sinks, swa_proj, s5_a_re, s5_a_im, s5_log_step, s5_b_re, s5_b_im, s5_c_re, s5_c_im,
                 s5_d, s5_glu_w, s5_glu_b, s5_proj, conv_w, conv_b, conv_ln_g, conv_ln_b, conv_proj,
                 diff_lq1, diff_lk1, diff_lq2, diff_lk2, diff_norm_g, diff_proj, w_out, lambda_init):
    B, L = x.shape[0], x.shape[1]
    h = x @ w_in
    aq, ak, av, dq, dk, dv, su, cu, gl = jnp.split(h, MIX_SPLITS, axis=-1)
    gates = jax.nn.sigmoid(gl).reshape(B, L, N_BRANCHES, D_MODEL)
    y_a = swa_sink_attention(aq, ak, av, swa_sinks) @ swa_proj
    y_b = s5_ssm(su, s5_a_re, s5_a_im, s5_log_step, s5_b_re, s5_b_im, s5_c_re, s5_c_im, s5_d, s5_glu_w, s5_glu_b) @ s5_proj
    y_c = conformer_conv(cu, conv_w, conv_b, conv_ln_g, conv_ln_b) @ conv_proj
    lam = (jnp.exp(jnp.sum(diff_lq1.astype(F32) * diff_lk1.astype(F32)))
           - jnp.exp(jnp.sum(diff_lq2.astype(F32) * diff_lk2.astype(F32))) + lambda_init)
    y_d = diff_attention(dq, dk, dv, lam, diff_norm_g, lambda_init) @ diff_proj
    merged = gates[:, :, 0] * y_a + gates[:, :, 1] * y_b + gates[:, :, 2] * y_c + gates[:, :, 3] * y_d
    return merged @ w_out


def cross_attention(x, mem_n, wq, wkv, wo):
    B, L = x.shape[0], x.shape[1]
    M = mem_n.shape[1]
    q = (x @ wq).reshape(B, L, CROSS_HEADS, CROSS_HEAD_DIM)
    k, v = jnp.split(mem_n @ wkv, 2, axis=-1)
    k = k.reshape(B, M, CROSS_HEADS, CROSS_HEAD_DIM)
    v = v.reshape(B, M, CROSS_HEADS, CROSS_HEAD_DIM)
    s = jnp.einsum('bqhe,bmhe->bhqm', q.astype(F32), k.astype(F32)) / math.sqrt(CROSS_HEAD_DIM)
    p = jax.nn.softmax(s, axis=-1)
    o = jnp.einsum('bhqm,bmhe->bqhe', p.astype(v.dtype), v).reshape(B, L, CROSS_WIDTH)
    return o @ wo


def setup_inputs(seed: int = 0) -> dict:
    key = jax.random.key(seed)
    keys = jax.random.split(key, 64)
    counter = [0]

    def nk():
        counter[0] += 1
        return keys[counter[0] - 1]

    def nrm(shape, scale):
        return scale * jax.random.normal(nk(), shape, F32)

    def gain(shape):
        return 1.0 + nrm(shape, 0.02)

    Ld, D, F = DEPTH, D_MODEL, FFN_DIM
    G, P, C = S5_GROUPS, S5_STATE, S5_GROUP
    inputs = {}
    inputs['x'] = nrm((BATCH, SEQ, D), 1.0)
    inputs['mem'] = nrm((BATCH, MEM_LEN, D), 1.0)
    inputs['ffn1_w_in'] = nrm((Ld, D, 2 * F), D ** -0.5)
    inputs['ffn1_w_out'] = nrm((Ld, F, D), DEEPNORM_BETA * F ** -0.5)
    inputs['ffn1_ln_g'] = gain((Ld, D))
    inputs['ffn1_ln_b'] = nrm((Ld, D), 0.02)
    inputs['mix_w_in'] = nrm((Ld, D, MIX_IN), D ** -0.5)
    inputs['swa_sinks'] = nrm((Ld, SWA_HEADS), 1.0)
    inputs['swa_proj'] = nrm((Ld, A_Q, D), A_Q ** -0.5)
    inputs['s5_a_re'] = -0.5 + nrm((Ld, G, P), 0.01)
    inputs['s5_a_im'] = math.pi * jnp.arange(P, dtype=F32) + nrm((Ld, G, P), 0.01)
    inputs['s5_log_step'] = jax.random.uniform(nk(), (Ld, G), F32, math.log(1e-3), math.log(1e-1))
    inputs['s5_b_re'] = nrm((Ld, G, P, C), (2.0 * C) ** -0.5)
    inputs['s5_b_im'] = nrm((Ld, G, P, C), (2.0 * C) ** -0.5)
    inputs['s5_c_re'] = nrm((Ld, G, C, P), (2.0 * P) ** -0.5)
    inputs['s5_c_im'] = nrm((Ld, G, C, P), (2.0 * P) ** -0.5)
    inputs['s5_d'] = nrm((Ld, S5_WIDTH), 1.0)
    inputs['s5_glu_w'] = nrm((Ld, S5_WIDTH, S5_WIDTH), S5_WIDTH ** -0.5)
    inputs['s5_glu_b'] = nrm((Ld, S5_WIDTH), 0.02)
    inputs['s5_proj'] = nrm((Ld, S5_WIDTH, D), S5_WIDTH ** -0.5)
    inputs['conv_w'] = nrm((Ld, CONV_K, CONV_WIDTH), CONV_K ** -0.5)
    inputs['conv_b'] = nrm((Ld, CONV_WIDTH), 0.02)
    inputs['conv_ln_g'] = gain((Ld, CONV_WIDTH))
    inputs['conv_ln_b'] = nrm((Ld, CONV_WIDTH), 0.02)
    inputs['conv_proj'] = nrm((Ld, CONV_WIDTH, D), CONV_WIDTH ** -0.5)
    inputs['diff_lq1'] = nrm((Ld, HEAD_DIM), 0.1)
    inputs['diff_lk1'] = nrm((Ld, HEAD_DIM), 0.1)
    inputs['diff_lq2'] = nrm((Ld, HEAD_DIM), 0.1)
    inputs['diff_lk2'] = nrm((Ld, HEAD_DIM), 0.1)
    inputs['diff_norm_g'] = gain((Ld, DIFF_V_DIM))
    inputs['diff_proj'] = nrm((Ld, D_V, D), D_V ** -0.5)
    inputs['mix_w_out'] = nrm((Ld, D, D), DEEPNORM_BETA * D ** -0.5)
    inputs['mix_ln_g'] = gain((Ld, D))
    inputs['mix_ln_b'] = nrm((Ld, D), 0.02)
    inputs['mem_ln_g'] = gain((D,))
    inputs['mem_ln_b'] = nrm((D,), 0.02)
    inputs['cross_wq'] = nrm((Ld, D, CROSS_WIDTH), D ** -0.5)
    inputs['cross_wkv'] = nrm((Ld, D, 2 * CROSS_WIDTH), D ** -0.5)
    inputs['cross_wo'] = nrm((Ld, CROSS_WIDTH, D), DEEPNORM_BETA * CROSS_WIDTH ** -0.5)
    inputs['cross_ln_g'] = gain((Ld, D))
    inputs['cross_ln_b'] = nrm((Ld, D), 0.02)
    inputs['ffn2_w_in'] = nrm((Ld, D, 2 * F), D ** -0.5)
    inputs['ffn2_w_out'] = nrm((Ld, F, D), DEEPNORM_BETA * F ** -0.5)
    inputs['ffn2_ln_g'] = gain((Ld, D))
    inputs['ffn2_ln_b'] = nrm((Ld, D), 0.02)
    return inputs


def reference(x, mem, ffn1_w_in, ffn1_w_out, ffn1_ln_g, ffn1_ln_b, mix_w_in, swa_sinks, swa_proj,
              s5_a_re, s5_a_im, s5_log_step, s5_b_re, s5_b_im, s5_c_re, s5_c_im, s5_d, s5_glu_w, s5_glu_b, s5_proj,
              conv_w, conv_b, conv_ln_g, conv_ln_b, conv_proj,
              diff_lq1, diff_lk1, diff_lq2, diff_lk2, diff_norm_g, diff_proj,
              mix_w_out, mix_ln_g, mix_ln_b, mem_ln_g, mem_ln_b,
              cross_wq, cross_wkv, cross_wo, cross_ln_g, cross_ln_b,
              ffn2_w_in, ffn2_w_out, ffn2_ln_g, ffn2_ln_b):
    mem_n = layer_norm(mem, mem_ln_g, mem_ln_b)
    for l in range(DEPTH):
        f = swiglu_ffn(x, ffn1_w_in[l], ffn1_w_out[l])
        x = layer_norm(DEEPNORM_ALPHA * x + 0.5 * f, ffn1_ln_g[l], ffn1_ln_b[l])
        lambda_init = 0.8 - 0.6 * math.exp(-0.3 * l)
        m = hybrid_mixer(x, mix_w_in[l], swa_sinks[l], swa_proj[l], s5_a_re[l], s5_a_im[l], s5_log_step[l],
                         s5_b_re[l], s5_b_im[l], s5_c_re[l], s5_c_im[l], s5_d[l], s5_glu_w[l], s5_glu_b[l], s5_proj[l],
                         conv_w[l], conv_b[l], conv_ln_g[l], conv_ln_b[l], conv_proj[l],
                         diff_lq1[l], diff_lk1[l], diff_lq2[l], diff_lk2[l], diff_norm_g[l], diff_proj[l],
                         mix_w_out[l], lambda_init)
        x = layer_norm(DEEPNORM_ALPHA * x + m, mix_ln_g[l], mix_ln_b[l])
        c = cross_attention(x, mem_n, cross_wq[l], cross_wkv[l], cross_wo[l])
        x = layer_norm(DEEPNORM_ALPHA * x + c, cross_ln_g[l], cross_ln_b[l])
        f = swiglu_ffn(x, ffn2_w_in[l], ffn2_w_out[l])
        x = layer_norm(DEEPNORM_ALPHA * x + 0.5 * f, ffn2_ln_g[l], ffn2_ln_b[l])
    return x
```

```python
import functools
import math

import jax
import jax.numpy as jnp
from jax import lax
from jax.experimental import pallas as pl
from jax.experimental.pallas import tpu as pltpu

F32 = jnp.float32
BF16 = jnp.bfloat16

D_MODEL = 1024
DEPTH = 4
MEM_LEN = 256
HEAD_DIM = 64
BLOCK = 128
SWA_HEADS = 8
SWA_KV_HEADS = 2
SWA_REP = SWA_HEADS // SWA_KV_HEADS
S5_WIDTH = 512
S5_GROUP = 16
S5_GROUPS = S5_WIDTH // S5_GROUP
S5_STATE = 64
S5_NSTATE = S5_GROUPS * S5_STATE
CONV_WIDTH = 512
CONV_K = 31
DIFF_HEADS = 4
DIFF_V_DIM = 2 * HEAD_DIM
CROSS_HEADS = 4
CROSS_HEAD_DIM = 128
CROSS_WIDTH = CROSS_HEADS * CROSS_HEAD_DIM
FFN_DIM = 2816
N_BRANCHES = 4

A_Q = SWA_HEADS * HEAD_DIM
A_KV = SWA_KV_HEADS * HEAD_DIM
D_QK = DIFF_HEADS * 2 * HEAD_DIM
D_V = DIFF_HEADS * DIFF_V_DIM
CONV_IN = 2 * CONV_WIDTH
GATE_COLS = N_BRANCHES * D_MODEL
OFF_AQ = 0
OFF_AK = OFF_AQ + A_Q
OFF_AV = OFF_AK + A_KV
OFF_DQ = OFF_AV + A_KV
OFF_DK = OFF_DQ + D_QK
OFF_DV = OFF_DK + D_QK
OFF_SU = OFF_DV + D_V
OFF_CU = OFF_SU + S5_WIDTH
OFF_GL = OFF_CU + CONV_IN
ATT_COLS = 3 * D_QK + A_Q + 2 * A_KV
SC_COLS = CONV_IN + S5_WIDTH

DEEPNORM_ALPHA = (2.0 * DEPTH) ** 0.25
LN_EPS = 1e-5
NEG_INF = -1e30

LANES = 128
MXU_WIDTH = 256
VMEM_LIMIT = 56 * 1024 * 1024

ROW_TILE = 512
FFN_CHUNK = MXU_WIDTH
DIFF_TILE = 256
CONV_TILE = 256
CONV_HALO = 32
S5_TILE = 128
CROSS_TILE = 512


def _alibi_slope(h, n):
    return 2.0 ** (-8.0 * (h + 1) / n)


def _layer_norm(z, g, b):
    mu = jnp.mean(z, axis=-1, keepdims=True)
    zc = z - mu
    var = jnp.mean(zc * zc, axis=-1, keepdims=True)
    return zc * lax.rsqrt(var + LN_EPS) * g + b


def _dot(a, b):
    return jnp.dot(a, b, preferred_element_type=F32)


def _dot_nt(a, b):
    return lax.dot_general(a, b, (((1,), (1,)), ((), ())), preferred_element_type=F32)


def _params(*sem):
    return pltpu.CompilerParams(dimension_semantics=sem, vmem_limit_bytes=VMEM_LIMIT)


def _resident(shape):
    nd = len(shape)
    return pl.BlockSpec(shape, lambda *_: (0,) * nd, pipeline_mode=pl.Buffered(1))


def _ffn_ln_kernel(x_ref, wg_ref, wu_ref, wo_ref, g_ref, b_ref, o_ref, a_ref):
    x = x_ref[...]
    xb = x.astype(BF16)
    for c in range(FFN_DIM // FFN_CHUNK):
        sl = slice(c * FFN_CHUNK, (c + 1) * FFN_CHUNK)
        gate = _dot(xb, wg_ref[:, sl])
        up = _dot(xb, wu_ref[:, sl])
        a_ref[:, sl] = (gate * jax.nn.sigmoid(gate) * up).astype(BF16)
    f = _dot(a_ref[...], wo_ref[...])
    o_ref[...] = _layer_norm(DEEPNORM_ALPHA * x + 0.5 * f, g_ref[...], b_ref[...])


def _ffn_ln(x, w_in, w_out, g, b):
    n = x.shape[0]
    return pl.pallas_call(
        _ffn_ln_kernel,
        grid=(n // ROW_TILE,),
        in_specs=[
            pl.BlockSpec((ROW_TILE, D_MODEL), lambda i: (i, 0)),
            pl.BlockSpec((D_MODEL, FFN_DIM), lambda i: (0, 0), pipeline_mode=pl.Buffered(1)),
            pl.BlockSpec((D_MODEL, FFN_DIM), lambda i: (0, 1), pipeline_mode=pl.Buffered(1)),
            _resident((FFN_DIM, D_MODEL)),
            _resident((1, D_MODEL)),
            _resident((1, D_MODEL)),
        ],
        out_specs=pl.BlockSpec((ROW_TILE, D_MODEL), lambda i: (i, 0)),
        out_shape=jax.ShapeDtypeStruct((n, D_MODEL), F32),
        scratch_shapes=[pltpu.VMEM((ROW_TILE, FFN_DIM), BF16)],
        compiler_params=_params("parallel"),
        name="ffn_ln",
    )(x, w_in, w_in, w_out, g, b)


def _mix_proj_kernel(x_ref, wa_ref, ws_ref, ha_ref, hs_ref):
    xb = x_ref[...].astype(BF16)
    ha_ref[...] = _dot(xb, wa_ref[...]).astype(BF16)
    hs_ref[...] = _dot(xb, ws_ref[...])


def _mix_proj(x, w_att, w_sc):
    n = x.shape[0]
    return pl.pallas_call(
        _mix_proj_kernel,
        grid=(n // ROW_TILE,),
        in_specs=[
            pl.BlockSpec((ROW_TILE, D_MODEL), lambda i: (i, 0)),
            _resident((D_MODEL, ATT_COLS)),
            _resident((D_MODEL, SC_COLS)),
        ],
        out_specs=[
            pl.BlockSpec((ROW_TILE, ATT_COLS), lambda i: (i, 0)),
            pl.BlockSpec((ROW_TILE, SC_COLS), lambda i: (i, 0)),
        ],
        out_shape=[
            jax.ShapeDtypeStruct((n, ATT_COLS), BF16),
            jax.ShapeDtypeStruct((n, SC_COLS), F32),
        ],
        compiler_params=_params("parallel"),
        name="mix_proj",
    )(x, w_att, w_sc)


def _swa_kernel(sink_ref, q_ref, kc_ref, vc_ref, kp_ref, vp_ref, o_ref):
    n = pl.program_id(1)
    qi = lax.broadcasted_iota(jnp.int32, (BLOCK, 2 * BLOCK), 0)
    kj = lax.broadcasted_iota(jnp.int32, (BLOCK, 2 * BLOCK), 1)
    dist = BLOCK + qi - kj
    first_key = jnp.where(n > 0, 0, BLOCK)
    valid = (dist >= 0) & (dist < BLOCK) & (kj >= first_key)
    distf = dist.astype(F32)
    q = q_ref[...]
    k = jnp.concatenate([kp_ref[...], kc_ref[...]], axis=0)
    v = jnp.concatenate([vp_ref[...], vc_ref[...]], axis=0)
    scale = 1.0 / math.sqrt(HEAD_DIM)
    for h in range(SWA_HEADS):
        g = h // SWA_REP
        qh = q[:, h * HEAD_DIM:(h + 1) * HEAD_DIM]
        kg = k[:, g * HEAD_DIM:(g + 1) * HEAD_DIM]
        vg = v[:, g * HEAD_DIM:(g + 1) * HEAD_DIM]
        s = _dot_nt(qh, kg) * scale - _alibi_slope(h, SWA_HEADS) * distf
        s = jnp.where(valid, s, NEG_INF)
        sink = sink_ref[h]
        m = jnp.maximum(jnp.max(s, axis=-1, keepdims=True), sink)
        p = jnp.exp(s - m)
        p = p / (jnp.sum(p, axis=-1, keepdims=True) + jnp.exp(sink - m))
        o_ref[:, h * HEAD_DIM:(h + 1) * HEAD_DIM] = _dot(p.astype(BF16), vg).astype(BF16)


def _swa(h_att, sinks, batch, seq):
    nb = seq // BLOCK
    h3 = h_att.reshape(batch, seq, ATT_COLS)
    cq = (3 * D_QK) // A_Q
    ck = (3 * D_QK + A_Q) // A_KV
    prev = lambda n: jnp.maximum(n - 1, 0)
    return pl.pallas_call(
        _swa_kernel,
        grid=(batch, nb),
        in_specs=[
            pl.BlockSpec(memory_space=pltpu.SMEM),
            pl.BlockSpec((None, BLOCK, A_Q), lambda b, n: (b, n, cq)),
            pl.BlockSpec((None, BLOCK, A_KV), lambda b, n: (b, n, ck)),
            pl.BlockSpec((None, BLOCK, A_KV), lambda b, n: (b, n, ck + 1)),
            pl.BlockSpec((None, BLOCK, A_KV), lambda b, n: (b, prev(n), ck)),
            pl.BlockSpec((None, BLOCK, A_KV), lambda b, n: (b, prev(n), ck + 1)),
        ],
        out_specs=pl.BlockSpec((None, BLOCK, A_Q), lambda b, n: (b, n, 0)),
        out_shape=jax.ShapeDtypeStruct((batch, seq, A_Q), BF16),
        compiler_params=_params("parallel", "arbitrary"),
        name="swa",
    )(sinks, h3, h3, h3, h3, h3).reshape(batch * seq, A_Q)


def _diff_kernel(lam_ref, q_ref, k_ref, v_ref, g_ref, o_ref, *, out_scale):
    h = pl.program_id(1)
    qt = pl.program_id(2)
    t = DIFF_TILE
    slope = jnp.float32(_alibi_slope(DIFF_HEADS - 1, DIFF_HEADS))
    for hh in range(DIFF_HEADS - 1):
        slope = jnp.where(h == hh, jnp.float32(_alibi_slope(hh, DIFF_HEADS)), slope)
    lane = lax.broadcasted_iota(jnp.int32, (t, 2 * HEAD_DIM), 1)
    q = q_ref[...].astype(F32) * (1.0 / math.sqrt(HEAD_DIM))
    q1 = jnp.where(lane < HEAD_DIM, q, 0.0).astype(BF16)
    q2 = jnp.where(lane >= HEAD_DIM, q, 0.0).astype(BF16)
    kcol = lax.broadcasted_iota(jnp.int32, (1, t), 1).astype(F32)

    def scores(kt):
        kb = k_ref[pl.ds(kt * t, t), :]
        bias = slope * (kcol + (kt * t).astype(F32))
        return _dot_nt(q1, kb) + bias, _dot_nt(q2, kb) + bias

    def update(carry, s, vb):
        m, l, acc = carry
        m_new = jnp.maximum(m, jnp.max(s, axis=-1, keepdims=True))
        a = jnp.exp(m - m_new)
        p = jnp.exp(s - m_new)
        l = a * l + jnp.sum(p, axis=-1, keepdims=True)
        acc = a * acc + _dot(p.astype(BF16), vb)
        return m_new, l, acc

    def body(kt, carry):
        c1, c2 = carry
        s1, s2 = scores(kt)
        vb = v_ref[pl.ds(kt * t, t), :]
        return update(c1, s1, vb), update(c2, s2, vb)

    init = (jnp.full((t, 1), NEG_INF, F32), jnp.zeros((t, 1), F32), jnp.zeros((t, DIFF_V_DIM), F32))
    c1, c2 = lax.fori_loop(0, qt, body, (init, init))
    s1, s2 = scores(qt)
    causal = (lax.broadcasted_iota(jnp.int32, (t, t), 0) >= lax.broadcasted_iota(jnp.int32, (t, t), 1))
    vb = v_ref[pl.ds(qt * t, t), :]
    _, l1, a1 = update(c1, jnp.where(causal, s1, NEG_INF), vb)
    _, l2, a2 = update(c2, jnp.where(causal, s2, NEG_INF), vb)
    o = a1 / l1 - lam_ref[0] * (a2 / l2)
    o = o * lax.rsqrt(jnp.mean(o * o, axis=-1, keepdims=True) + LN_EPS) * g_ref[...] * out_scale
    o_ref[...] = o.astype(BF16)


def _diff_attn(h_att, lam, norm_g, lambda_init, batch, seq):
    h3 = h_att.reshape(batch, seq, ATT_COLS)
    w = 2 * HEAD_DIM
    return pl.pallas_call(
        functools.partial(_diff_kernel, out_scale=1.0 - lambda_init),
        grid=(batch, DIFF_HEADS, seq // DIFF_TILE),
        in_specs=[
            pl.BlockSpec(memory_space=pltpu.SMEM),
            pl.BlockSpec((None, DIFF_TILE, w), lambda b, h, i: (b, i, h)),
            pl.BlockSpec((None, seq, w), lambda b, h, i: (b, 0, DIFF_HEADS + h)),
            pl.BlockSpec((None, seq, w), lambda b, h, i: (b, 0, 2 * DIFF_HEADS + h)),
            _resident((1, DIFF_V_DIM)),
        ],
        out_specs=pl.BlockSpec((None, DIFF_TILE, w), lambda b, h, i: (b, i, h)),
        out_shape=jax.ShapeDtypeStruct((batch, seq, D_V), BF16),
        compiler_params=_params("parallel", "parallel", "arbitrary"),
        name="diff_attn",
    )(lam, h3, h3, h3, norm_g).reshape(batch * seq, D_V)


def _conv_kernel(cur_ref, halo_ref, w_ref, cb_ref, g_ref, b_ref, o_ref, buf_ref):
    i = pl.program_id(1)

    def glu(z):
        return z[:, :CONV_WIDTH] * jax.nn.sigmoid(z[:, CONV_WIDTH:])

    halo = glu(halo_ref[...])
    buf_ref[0:CONV_HALO, :] = halo * (i > 0).astype(F32)
    buf_ref[CONV_HALO:, :] = glu(cur_ref[...])
    first = CONV_HALO - (CONV_K - 1)
    acc = jnp.zeros((CONV_TILE, CONV_WIDTH), F32)
    for k in range(CONV_K):
        acc = acc + w_ref[k:k + 1, :] * buf_ref[first + k:first + k + CONV_TILE, :]
    y = _layer_norm(acc + cb_ref[...], g_ref[...], b_ref[...])
    o_ref[...] = (y * jax.nn.sigmoid(y)).astype(BF16)


def _conv_module(h_sc, conv_w, conv_b, ln_g, ln_b, batch, seq):
    h3 = h_sc.reshape(batch, seq, SC_COLS)
    r = CONV_TILE // CONV_HALO
    return pl.pallas_call(
        _conv_kernel,
        grid=(batch, seq // CONV_TILE),
        in_specs=[
            pl.BlockSpec((None, CONV_TILE, CONV_IN), lambda b, i: (b, i, 0)),
            pl.BlockSpec((None, CONV_HALO, CONV_IN), lambda b, i: (b, jnp.maximum(i * r - 1, 0), 0)),
            _resident((CONV_K, CONV_WIDTH)),
            _resident((1, CONV_WIDTH)),
            _resident((1, CONV_WIDTH)),
            _resident((1, CONV_WIDTH)),
        ],
        out_specs=pl.BlockSpec((None, CONV_TILE, CONV_WIDTH), lambda b, i: (b, i, 0)),
        out_shape=jax.ShapeDtypeStruct((batch, seq, CONV_WIDTH), BF16),
        scratch_shapes=[pltpu.VMEM((CONV_TILE + CONV_HALO, CONV_WIDTH), F32)],
        compiler_params=_params("parallel", "arbitrary"),
        name="conv_module",
    )(h3, h3, conv_w, conv_b, ln_g, ln_b).reshape(batch * seq, CONV_WIDTH)


def _s5_kernel(u_ref, bw_ref, cw_ref, air_ref, aii_ref, apr_ref, api_ref, tri_ref,
               d_ref, gw_ref, gb_ref, o_ref, hr_ref, hi_ref, xr_ref, xi_ref):
    i = pl.program_id(1)

    @pl.when(i == 0)
    def _():
        hr_ref[...] = jnp.zeros_like(hr_ref)
        hi_ref[...] = jnp.zeros_like(hi_ref)

    u = u_ref[...]
    ub = u.astype(BF16)
    tri = tri_ref[...]
    nq = S5_WIDTH // LANES
    sw = S5_NSTATE // nq
    for j in range(nq):
        sl = slice(j * sw, (j + 1) * sw)
        bu = _dot(ub[:, j * LANES:(j + 1) * LANES], bw_ref[j])
        bur, bui = bu[:, :sw], bu[:, sw:]
        air, aii = air_ref[:, sl], aii_ref[:, sl]
        zr = (air * bur - aii * bui).astype(BF16)
        zi = (air * bui + aii * bur).astype(BF16)
        cr = _dot(tri, zr) + hr_ref[:, sl]
        ci = _dot(tri, zi) + hi_ref[:, sl]
        apr, api = apr_ref[:, sl], api_ref[:, sl]
        xr_ref[:, sl] = apr * cr - api * ci
        xi_ref[:, sl] = apr * ci + api * cr
    hr_ref[...] = xr_ref[S5_TILE - 1:S5_TILE, :]
    hi_ref[...] = xi_ref[S5_TILE - 1:S5_TILE, :]
    ys = []
    for j in range(nq):
        sl = slice(j * sw, (j + 1) * sw)
        xc = jnp.concatenate([xr_ref[:, sl], xi_ref[:, sl]], axis=1).astype(BF16)
        ys.append(_dot(xc, cw_ref[j]))
    y = jnp.concatenate(ys, axis=1) + d_ref[...] * u
    y = jax.nn.gelu(y, approximate=True)
    gate = _dot(y.astype(BF16), gw_ref[...]) + gb_ref[...]
    o_ref[...] = (y * jax.nn.sigmoid(gate)).astype(BF16)


def _s5_tables(a_re, a_im, log_step, b_re, b_im, c_re, c_im):
    g, p, c = S5_GROUPS, S5_STATE, S5_GROUP
    step = jnp.exp(log_step)[:, None]
    mag = jnp.exp(a_re * step)
    abar_r, abar_i = mag * jnp.cos(a_im * step), mag * jnp.sin(a_im * step)
    den = a_re * a_re + a_im * a_im
    nr, ni = abar_r - 1.0, abar_i
    coef_r = (nr * a_re + ni * a_im) / den
    coef_i = (ni * a_re - nr * a_im) / den
    bbar_r = coef_r[..., None] * b_re - coef_i[..., None] * b_im
    bbar_i = coef_r[..., None] * b_im + coef_i[..., None] * b_re
    nq = S5_WIDTH // LANES
    gq = g // nq
    eye = jnp.eye(gq, dtype=F32)

    def in_slab(bb):
        bb = bb.reshape(nq, gq, p, c)
        return jnp.einsum('qgpc,gh->qgchp', bb, eye).reshape(nq, gq * c, gq * p)

    bw = jnp.concatenate([in_slab(bbar_r), in_slab(bbar_i)], axis=2).astype(BF16)

    def out_slab(cc):
        cc = cc.reshape(nq, gq, c, p)
        return jnp.einsum('qgcp,gh->qgphc', cc, eye).reshape(nq, gq * p, gq * c)

    cw = jnp.concatenate([out_slab(c_re), -out_slab(c_im)], axis=1).astype(BF16)
    tt = jnp.arange(1, S5_TILE + 1, dtype=F32)[:, None]
    la = (a_re * step).reshape(1, g * p)
    th = (a_im * step).reshape(1, g * p)
    pm, ang = jnp.exp(tt * la), tt * th
    im_ = jnp.exp(-tt * la)
    apr, api = pm * jnp.cos(ang), pm * jnp.sin(ang)
    air, aii = im_ * jnp.cos(ang), -im_ * jnp.sin(ang)
    return bw, cw, air, aii, apr, api


def _s5(h_sc, tables, d_skip, glu_w, glu_b, batch, seq):
    bw, cw, air, aii, apr, api = tables
    h3 = h_sc.reshape(batch, seq, SC_COLS)
    tri = jnp.tril(jnp.ones((S5_TILE, S5_TILE), F32)).astype(BF16)
    cu = CONV_IN // S5_WIDTH
    tab = (S5_TILE, S5_NSTATE)
    return pl.pallas_call(
        _s5_kernel,
        grid=(batch, seq // S5_TILE),
        in_specs=[
            pl.BlockSpec((None, S5_TILE, S5_WIDTH), lambda b, i: (b, i, cu)),
            _resident(bw.shape), _resident(cw.shape),
            _resident(tab), _resident(tab), _resident(tab), _resident(tab),
            _resident((S5_TILE, S5_TILE)),
            _resident((1, S5_WIDTH)),
            _resident((S5_WIDTH, S5_WIDTH)),
            _resident((1, S5_WIDTH)),
        ],
        out_specs=pl.BlockSpec((None, S5_TILE, S5_WIDTH), lambda b, i: (b, i, 0)),
        out_shape=jax.ShapeDtypeStruct((batch, seq, S5_WIDTH), BF16),
        scratch_shapes=[pltpu.VMEM((1, S5_NSTATE), F32), pltpu.VMEM((1, S5_NSTATE), F32),
                        pltpu.VMEM(tab, F32), pltpu.VMEM(tab, F32)],
        compiler_params=_params("parallel", "arbitrary"),
        name="s5_ssm",
    )(h3, bw, cw, air, aii, apr, api, tri, d_skip, glu_w, glu_b).reshape(batch * seq, S5_WIDTH)


def _merge_kernel(x_ref, oa_ref, ob_ref, oc_ref, od_ref, wg_ref, pa_ref, pb_ref, pc_ref, pd_ref,
                  wo_ref, g_ref, b_ref, o_ref):
    x = x_ref[...]
    xb = x.astype(BF16)
    merged = jnp.zeros((ROW_TILE, D_MODEL), F32)
    for i, (br_ref, pr_ref) in enumerate(((oa_ref, pa_ref), (ob_ref, pb_ref), (oc_ref, pc_ref), (od_ref, pd_ref))):
        gl = _dot(xb, wg_ref[:, i * D_MODEL:(i + 1) * D_MODEL])
        merged = merged + jax.nn.sigmoid(gl) * _dot(br_ref[...], pr_ref[...])
    m = _dot(merged.astype(BF16), wo_ref[...])
    o_ref[...] = _layer_norm(DEEPNORM_ALPHA * x + m, g_ref[...], b_ref[...])


def _merge(x, oa, ob, oc, od, w_gate, pa, pb, pc, pd, w_out, g, b):
    n = x.shape[0]
    row = lambda w: pl.BlockSpec((ROW_TILE, w), lambda i: (i, 0))
    return pl.pallas_call(
        _merge_kernel,
        grid=(n // ROW_TILE,),
        in_specs=[
            row(D_MODEL), row(A_Q), row(S5_WIDTH), row(CONV_WIDTH), row(D_V),
            _resident((D_MODEL, GATE_COLS)),
            _resident((A_Q, D_MODEL)), _resident((S5_WIDTH, D_MODEL)),
            _resident((CONV_WIDTH, D_MODEL)), _resident((D_V, D_MODEL)),
            _resident((D_MODEL, D_MODEL)),
            _resident((1, D_MODEL)), _resident((1, D_MODEL)),
        ],
        out_specs=row(D_MODEL),
        out_shape=jax.ShapeDtypeStruct((n, D_MODEL), F32),
        compiler_params=_params("parallel"),
        name="gated_merge",
    )(x, oa, ob, oc, od, w_gate, pa, pb, pc, pd, w_out, g, b)


def _mem_kv_kernel(m_ref, g_ref, b_ref, w_ref, o_ref):
    mn = _layer_norm(m_ref[...], g_ref[...], b_ref[...])
    o_ref[...] = _dot(mn.astype(BF16), w_ref[...]).astype(BF16)


def _mem_kv(mem, g, b, wkv):
    n = mem.shape[0]
    return pl.pallas_call(
        _mem_kv_kernel,
        grid=(DEPTH, n // ROW_TILE),
        in_specs=[
            pl.BlockSpec((ROW_TILE, D_MODEL), lambda l, i: (i, 0)),
            _resident((1, D_MODEL)), _resident((1, D_MODEL)),
            pl.BlockSpec((None, D_MODEL, 2 * CROSS_WIDTH), lambda l, i: (l, 0, 0)),
        ],
        out_specs=pl.BlockSpec((None, ROW_TILE, 2 * CROSS_WIDTH), lambda l, i: (l, i, 0)),
        out_shape=jax.ShapeDtypeStruct((DEPTH, n, 2 * CROSS_WIDTH), BF16),
        compiler_params=_params("parallel", "parallel"),
        name="mem_kv",
    )(mem, g, b, wkv)


def _cross_kernel(x_ref, kv_ref, wq_ref, wo_ref, g_ref, b_ref, o_ref):
    x = x_ref[...]
    q = _dot(x.astype(BF16), wq_ref[...]).astype(BF16)
    kv = kv_ref[...]
    scale = 1.0 / math.sqrt(CROSS_HEAD_DIM)
    outs = []
    for h in range(CROSS_HEADS):
        sl = slice(h * CROSS_HEAD_DIM, (h + 1) * CROSS_HEAD_DIM)
        s = _dot_nt(q[:, sl], kv[:, sl]) * scale
        p = jnp.exp(s - jnp.max(s, axis=-1, keepdims=True))
        p = p / jnp.sum(p, axis=-1, keepdims=True)
        outs.append(_dot(p.astype(BF16), kv[:, CROSS_WIDTH + h * CROSS_HEAD_DIM:CROSS_WIDTH + (h + 1) * CROSS_HEAD_DIM]))
    o = jnp.concatenate(outs, axis=1).astype(BF16)
    c = _dot(o, wo_ref[...])
    o_ref[...] = _layer_norm(DEEPNORM_ALPHA * x + c, g_ref[...], b_ref[...])


def _cross_attn(x, kv, wq, wo, g, b, batch, seq):
    x3 = x.reshape(batch, seq, D_MODEL)
    kv3 = kv.reshape(batch, MEM_LEN, 2 * CROSS_WIDTH)
    return pl.pallas_call(
        _cross_kernel,
        grid=(batch, seq // CROSS_TILE),
        in_specs=[
            pl.BlockSpec((None, CROSS_TILE, D_MODEL), lambda b, i: (b, i, 0)),
            pl.BlockSpec((None, MEM_LEN, 2 * CROSS_WIDTH), lambda b, i: (b, 0, 0)),
            _resident((D_MODEL, CROSS_WIDTH)),
            _resident((CROSS_WIDTH, D_MODEL)),
            _resident((1, D_MODEL)), _resident((1, D_MODEL)),
        ],
        out_specs=pl.BlockSpec((None, CROSS_TILE, D_MODEL), lambda b, i: (b, i, 0)),
        out_shape=jax.ShapeDtypeStruct((batch, seq, D_MODEL), F32),
        compiler_params=_params("parallel", "parallel"),
        name="cross_attn",
    )(x3, kv3, wq, wo, g, b).reshape(batch * seq, D_MODEL)


def _row(v):
    return v.reshape(1, -1).astype(F32)


def kernel(x, mem, ffn1_w_in, ffn1_w_out, ffn1_ln_g, ffn1_ln_b, mix_w_in, swa_sinks, swa_proj, s5_a_re, s5_a_im, s5_log_step, s5_b_re, s5_b_im, s5_c_re, s5_c_im, s5_d, s5_glu_w, s5_glu_b, s5_proj, conv_w, conv_b, conv_ln_g, conv_ln_b, conv_proj, diff_lq1, diff_lk1, diff_lq2, diff_lk2, diff_norm_g, diff_proj, mix_w_out, mix_ln_g, mix_ln_b, mem_ln_g, mem_ln_b, cross_wq, cross_wkv, cross_wo, cross_ln_g, cross_ln_b, ffn2_w_in, ffn2_w_out, ffn2_ln_g, ffn2_ln_b):
    batch, seq, _ = x.shape
    n = batch * seq
    assert seq % DIFF_TILE == 0 and seq % CONV_TILE == 0 and n % ROW_TILE == 0 and seq % CROSS_TILE == 0
    h = x.reshape(n, D_MODEL)
    kv_all = _mem_kv(mem.reshape(batch * MEM_LEN, D_MODEL), _row(mem_ln_g), _row(mem_ln_b),
                     cross_wkv.astype(BF16))
    for l in range(DEPTH):
        h = _ffn_ln(h, ffn1_w_in[l].astype(BF16), ffn1_w_out[l].astype(BF16),
                    _row(ffn1_ln_g[l]), _row(ffn1_ln_b[l]))
        lambda_init = 0.8 - 0.6 * math.exp(-0.3 * l)
        w = mix_w_in[l]
        cols = lambda off, width: w[:, off:off + width]
        w_att = jnp.concatenate([cols(OFF_DQ, D_QK), cols(OFF_DK, D_QK), cols(OFF_DV, D_V),
                                 cols(OFF_AQ, A_Q), cols(OFF_AK, A_KV), cols(OFF_AV, A_KV)], axis=1).astype(BF16)
        w_sc = jnp.concatenate([cols(OFF_CU, CONV_IN), cols(OFF_SU, S5_WIDTH)], axis=1).astype(BF16)
        w_gate = cols(OFF_GL, GATE_COLS).astype(BF16)
        h_att, h_sc = _mix_proj(h, w_att, w_sc)
        o_a = _swa(h_att, swa_sinks[l].astype(F32), batch, seq)
        tables = _s5_tables(s5_a_re[l], s5_a_im[l], s5_log_step[l], s5_b_re[l], s5_b_im[l], s5_c_re[l], s5_c_im[l])
        o_b = _s5(h_sc, tables, _row(s5_d[l]), s5_glu_w[l].astype(BF16), _row(s5_glu_b[l]), batch, seq)
        o_c = _conv_module(h_sc, conv_w[l], _row(conv_b[l]), _row(conv_ln_g[l]), _row(conv_ln_b[l]), batch, seq)
        lam = (jnp.exp(jnp.sum(diff_lq1[l] * diff_lk1[l])) - jnp.exp(jnp.sum(diff_lq2[l] * diff_lk2[l]))
               + lambda_init).reshape(1).astype(F32)
        o_d = _diff_attn(h_att, lam, _row(diff_norm_g[l]), lambda_init, batch, seq)
        h = _merge(h, o_a, o_b, o_c, o_d, w_gate, swa_proj[l].astype(BF16), s5_proj[l].astype(BF16),
                   conv_proj[l].astype(BF16), diff_proj[l].astype(BF16), mix_w_out[l].astype(BF16),
                   _row(mix_ln_g[l]), _row(mix_ln_b[l]))
        h = _cross_attn(h, kv_all[l], cross_wq[l].astype(BF16), cross_wo[l].astype(BF16),
                        _row(cross_ln_g[l]), _row(cross_ln_b[l]), batch, seq)
        h = _ffn_ln(h, ffn2_w_in[l].astype(BF16), ffn2_w_out[l].astype(BF16),
                    _row(ffn2_ln_g[l]), _row(ffn2_ln_b[l]))
    return h.reshape(batch, seq, D_MODEL)
```

```python
import functools
import math

import jax
import jax.numpy as jnp
from jax import lax
from jax.experimental import pallas as pl
from jax.experimental.pallas import tpu as pltpu

F32 = jnp.float32
BF16 = jnp.bfloat16

D_MODEL = 1024
DEPTH = 4
MEM_LEN = 256
HEAD_DIM = 64
BLOCK = 128
SWA_HEADS = 8
SWA_KV_HEADS = 2
SWA_REP = SWA_HEADS // SWA_KV_HEADS
S5_WIDTH = 512
S5_GROUP = 16
S5_GROUPS = S5_WIDTH // S5_GROUP
S5_STATE = 64
S5_NSTATE = S5_GROUPS * S5_STATE
CONV_WIDTH = 512
CONV_K = 31
DIFF_HEADS = 4
DIFF_V_DIM = 2 * HEAD_DIM
CROSS_HEADS = 4
CROSS_HEAD_DIM = 128
CROSS_WIDTH = CROSS_HEADS * CROSS_HEAD_DIM
FFN_DIM = 2816
N_BRANCHES = 4

A_Q = SWA_HEADS * HEAD_DIM
A_KV = SWA_KV_HEADS * HEAD_DIM
D_QK = DIFF_HEADS * 2 * HEAD_DIM
D_V = DIFF_HEADS * DIFF_V_DIM
CONV_IN = 2 * CONV_WIDTH
GATE_COLS = N_BRANCHES * D_MODEL
OFF_AQ = 0
OFF_AK = OFF_AQ + A_Q
OFF_AV = OFF_AK + A_KV
OFF_DQ = OFF_AV + A_KV
OFF_DK = OFF_DQ + D_QK
OFF_DV = OFF_DK + D_QK
OFF_SU = OFF_DV + D_V
OFF_CU = OFF_SU + S5_WIDTH
OFF_GL = OFF_CU + CONV_IN
ATT_COLS = 2 * D_QK + A_Q + 2 * A_KV
SC_COLS = CONV_IN + S5_WIDTH

DEEPNORM_ALPHA = (2.0 * DEPTH) ** 0.25
LN_EPS = 1e-5
NEG_INF = -1e30

LANES = 128
SUBLANES = 8
MXU_WIDTH = 256
VMEM_LIMIT = 56 * 1024 * 1024

ROW_TILE = 512
FFN_CHUNK = MXU_WIDTH
DIFF_TILE = 512
DIFF_VROWS = DIFF_V_DIM + 16
LOG2E = math.log2(math.e)
CONV_TILE = 256
CONV_HALO = 32
CONV_SHIFT_ROWS = CONV_TILE + CONV_HALO - SUBLANES
S5_TILE = 128
CROSS_TILE = 512


def _alibi_slope(h, n):
    return 2.0 ** (-8.0 * (h + 1) / n)


def _layer_norm(z, g, b):
    mu = jnp.mean(z, axis=-1, keepdims=True)
    zc = z - mu
    var = jnp.mean(zc * zc, axis=-1, keepdims=True)
    return zc * lax.rsqrt(var + LN_EPS) * g + b


def _dot(a, b):
    return jnp.dot(a, b, preferred_element_type=F32)


def _dot_nt(a, b):
    return lax.dot_general(a, b, (((1,), (1,)), ((), ())), preferred_element_type=F32)


def _params(*sem):
    return pltpu.CompilerParams(dimension_semantics=sem, vmem_limit_bytes=VMEM_LIMIT)


def _resident(shape):
    nd = len(shape)
    return pl.BlockSpec(shape, lambda *_: (0,) * nd, pipeline_mode=pl.Buffered(1))


def _ffn_ln_kernel(x_ref, wg_ref, wu_ref, wo_ref, g_ref, b_ref, o_ref, a_ref):
    x = x_ref[...]
    xb = x.astype(BF16)
    for c in range(FFN_DIM // FFN_CHUNK):
        sl = slice(c * FFN_CHUNK, (c + 1) * FFN_CHUNK)
        gate = _dot(xb, wg_ref[:, sl])
        up = _dot(xb, wu_ref[:, sl])
        a_ref[:, sl] = (gate * jax.nn.sigmoid(gate) * up).astype(BF16)
    f = _dot(a_ref[...], wo_ref[...])
    o_ref[...] = _layer_norm(DEEPNORM_ALPHA * x + 0.5 * f, g_ref[...], b_ref[...])


def _ffn_ln(x, w_in, w_out, g, b):
    n = x.shape[0]
    return pl.pallas_call(
        _ffn_ln_kernel,
        grid=(n // ROW_TILE,),
        in_specs=[
            pl.BlockSpec((ROW_TILE, D_MODEL), lambda i: (i, 0)),
            pl.BlockSpec((D_MODEL, FFN_DIM), lambda i: (0, 0), pipeline_mode=pl.Buffered(1)),
            pl.BlockSpec((D_MODEL, FFN_DIM), lambda i: (0, 1), pipeline_mode=pl.Buffered(1)),
            _resident((FFN_DIM, D_MODEL)),
            _resident((1, D_MODEL)),
            _resident((1, D_MODEL)),
        ],
        out_specs=pl.BlockSpec((ROW_TILE, D_MODEL), lambda i: (i, 0)),
        out_shape=jax.ShapeDtypeStruct((n, D_MODEL), F32),
        scratch_shapes=[pltpu.VMEM((ROW_TILE, FFN_DIM), BF16)],
        compiler_params=_params("parallel"),
        name="ffn_ln",
    )(x, w_in, w_in, w_out, g, b)


def _mix_proj_kernel(x_ref, wa_ref, ws_ref, wvt_ref, ha_ref, hs_ref, vt_ref):
    xb = x_ref[...].astype(BF16)
    ha_ref[...] = _dot(xb, wa_ref[...]).astype(BF16)
    hs_ref[...] = _dot(xb, ws_ref[...])
    vt_ref[...] = _dot_nt(wvt_ref[...], xb).astype(BF16)


def _mix_proj(x, w_att, w_sc, w_dv_t, batch, seq):
    n = x.shape[0]
    per_seq = seq // ROW_TILE
    return pl.pallas_call(
        _mix_proj_kernel,
        grid=(n // ROW_TILE,),
        in_specs=[
            pl.BlockSpec((ROW_TILE, D_MODEL), lambda i: (i, 0)),
            _resident((D_MODEL, ATT_COLS)),
            _resident((D_MODEL, SC_COLS)),
            _resident((D_V, D_MODEL)),
        ],
        out_specs=[
            pl.BlockSpec((ROW_TILE, ATT_COLS), lambda i: (i, 0)),
            pl.BlockSpec((ROW_TILE, SC_COLS), lambda i: (i, 0)),
            pl.BlockSpec((None, D_V, ROW_TILE), lambda i: (i // per_seq, 0, i % per_seq)),
        ],
        out_shape=[
            jax.ShapeDtypeStruct((n, ATT_COLS), BF16),
            jax.ShapeDtypeStruct((n, SC_COLS), F32),
            jax.ShapeDtypeStruct((batch, D_V, seq), BF16),
        ],
        compiler_params=_params("parallel"),
        name="mix_proj",
    )(x, w_att, w_sc, w_dv_t)


def _swa_kernel(sink_ref, q_ref, kc_ref, vc_ref, kp_ref, vp_ref, o_ref):
    n = pl.program_id(1)
    qi = lax.broadcasted_iota(jnp.int32, (BLOCK, 2 * BLOCK), 0)
    kj = lax.broadcasted_iota(jnp.int32, (BLOCK, 2 * BLOCK), 1)
    dist = BLOCK + qi - kj
    first_key = jnp.where(n > 0, 0, BLOCK)
    valid = (dist >= 0) & (dist < BLOCK) & (kj >= first_key)
    distf = dist.astype(F32)
    q = q_ref[...]
    k = jnp.concatenate([kp_ref[...], kc_ref[...]], axis=0)
    v = jnp.concatenate([vp_ref[...], vc_ref[...]], axis=0)
    scale = 1.0 / math.sqrt(HEAD_DIM)
    for h in range(SWA_HEADS):
        g = h // SWA_REP
        qh = q[:, h * HEAD_DIM:(h + 1) * HEAD_DIM]
        kg = k[:, g * HEAD_DIM:(g + 1) * HEAD_DIM]
        vg = v[:, g * HEAD_DIM:(g + 1) * HEAD_DIM]
        s = _dot_nt(qh, kg) * scale - _alibi_slope(h, SWA_HEADS) * distf
        s = jnp.where(valid, s, NEG_INF)
        sink = sink_ref[h]
        m = jnp.maximum(jnp.max(s, axis=-1, keepdims=True), sink)
        p = jnp.exp(s - m)
        p = p / (jnp.sum(p, axis=-1, keepdims=True) + jnp.exp(sink - m))
        o_ref[:, h * HEAD_DIM:(h + 1) * HEAD_DIM] = _dot(p.astype(BF16), vg).astype(BF16)


def _swa(h_att, sinks, batch, seq):
    nb = seq // BLOCK
    h3 = h_att.reshape(batch, seq, ATT_COLS)
    cq = (2 * D_QK) // A_Q
    ck = (2 * D_QK + A_Q) // A_KV
    prev = lambda n: jnp.maximum(n - 1, 0)
    return pl.pallas_call(
        _swa_kernel,
        grid=(batch, nb),
        in_specs=[
            pl.BlockSpec(memory_space=pltpu.SMEM),
            pl.BlockSpec((None, BLOCK, A_Q), lambda b, n: (b, n, cq)),
            pl.BlockSpec((None, BLOCK, A_KV), lambda b, n: (b, n, ck)),
            pl.BlockSpec((None, BLOCK, A_KV), lambda b, n: (b, n, ck + 1)),
            pl.BlockSpec((None, BLOCK, A_KV), lambda b, n: (b, prev(n), ck)),
            pl.BlockSpec((None, BLOCK, A_KV), lambda b, n: (b, prev(n), ck + 1)),
        ],
        out_specs=pl.BlockSpec((None, BLOCK, A_Q), lambda b, n: (b, n, 0)),
        out_shape=jax.ShapeDtypeStruct((batch, seq, A_Q), BF16),
        compiler_params=_params("parallel", "arbitrary"),
        name="swa",
    )(sinks, h3, h3, h3, h3, h3).reshape(batch * seq, A_Q)


def _diff_kernel(sc_ref, q_ref, k_ref, vt_ref, g_ref, o_ref,
                 qs_ref, vx_ref, b_ref, pv_ref, mx_ref, m_ref, acc_ref, *, n_tiles):
    h = pl.program_id(1)
    t = DIFF_TILE
    v_dim = DIFF_V_DIM
    slope = jnp.float32(_alibi_slope(DIFF_HEADS - 1, DIFF_HEADS) * LOG2E)
    for hh in range(DIFF_HEADS - 1):
        slope = jnp.where(h == hh, jnp.float32(_alibi_slope(hh, DIFF_HEADS) * LOG2E), slope)

    for j in range(n_tiles):
        vx_ref[j, :v_dim, :] = vt_ref[:, j * t:(j + 1) * t]
        vx_ref[j, v_dim:, :] = jnp.ones((DIFF_VROWS - v_dim, t), BF16)
    krow = lax.broadcasted_iota(jnp.int32, (t, t), 0)
    qcol = lax.broadcasted_iota(jnp.int32, (t, t), 1)
    b_ref[0] = slope * krow.astype(F32)
    b_ref[1] = jnp.where(qcol >= krow, slope * krow.astype(F32), NEG_INF)

    def local(j, qt):
        k0 = pl.multiple_of(j * t, t)
        kb = k_ref[pl.ds(k0, t), :]
        bias = b_ref[jnp.where(j == qt, 1, 0)]
        vx = vx_ref[j]
        slot = j & 1
        for c in range(2):
            s = _dot_nt(kb, qs_ref[c]) + bias
            mx = jnp.max(s, axis=0, keepdims=True)
            e = jnp.exp2(s - mx).astype(BF16)
            pv_ref[slot, c] = _dot(vx, e)
            mx_ref[slot, c] = mx

    def merge(j):
        off = slope * (j * t).astype(F32)
        slot = j & 1
        for c in range(2):
            m_old = m_ref[c]
            mxo = mx_ref[slot, c] + off
            m_new = jnp.maximum(m_old, mxo)
            m_ref[c] = m_new
            acc_ref[c] = jnp.exp2(m_old - m_new) * acc_ref[c] + jnp.exp2(mxo - m_new) * pv_ref[slot, c]

    def q_tile(qt, carry):
        q0 = pl.multiple_of(qt * t, t)
        lane = lax.broadcasted_iota(jnp.int32, (t, 2 * HEAD_DIM), 1)
        q = q_ref[pl.ds(q0, t), :].astype(F32) * (LOG2E / math.sqrt(HEAD_DIM))
        qs_ref[0] = jnp.where(lane < HEAD_DIM, q, 0.0).astype(BF16)
        qs_ref[1] = jnp.where(lane >= HEAD_DIM, q, 0.0).astype(BF16)
        m_ref[...] = jnp.full(m_ref.shape, NEG_INF, F32)
        acc_ref[...] = jnp.zeros(acc_ref.shape, F32)
        local(0, qt)

        def body(kt, c):
            merge(kt)
            local(kt + 1, qt)
            return c

        lax.fori_loop(0, qt, body, 0)
        merge(qt)

        def normalised(c):
            a = acc_ref[c]
            return a[:v_dim] * (1.0 / a[v_dim:v_dim + 1])

        o = (normalised(0) - sc_ref[0] * normalised(1)).T
        o = o * lax.rsqrt(jnp.mean(o * o, axis=-1, keepdims=True) + LN_EPS) * g_ref[...] * sc_ref[1]
        o_ref[pl.ds(q0, t), :] = o.astype(BF16)
        return carry

    lax.fori_loop(0, n_tiles, q_tile, 0)


def _diff_attn(h_att, v_t, scalars, norm_g, batch, seq):
    h3 = h_att.reshape(batch, seq, ATT_COLS)
    w = 2 * HEAD_DIM
    t = DIFF_TILE
    n_tiles = seq // t
    return pl.pallas_call(
        functools.partial(_diff_kernel, n_tiles=n_tiles),
        grid=(batch, DIFF_HEADS),
        in_specs=[
            pl.BlockSpec(memory_space=pltpu.SMEM),
            pl.BlockSpec((None, seq, w), lambda b, h: (b, 0, h)),
            pl.BlockSpec((None, seq, w), lambda b, h: (b, 0, DIFF_HEADS + h)),
            pl.BlockSpec((None, DIFF_V_DIM, seq), lambda b, h: (b, h, 0)),
            _resident((1, DIFF_V_DIM)),
        ],
        out_specs=pl.BlockSpec((None, seq, w), lambda b, h: (b, 0, h)),
        out_shape=jax.ShapeDtypeStruct((batch, seq, D_V), BF16),
        scratch_shapes=[
            pltpu.VMEM((2, t, w), BF16),
            pltpu.VMEM((n_tiles, DIFF_VROWS, t), BF16),
            pltpu.VMEM((2, t, t), F32),
            pltpu.VMEM((2, 2, DIFF_VROWS, t), F32),
            pltpu.VMEM((2, 2, 1, t), F32),
            pltpu.VMEM((2, 1, t), F32),
            pltpu.VMEM((2, DIFF_VROWS, t), F32),
        ],
        compiler_params=_params("parallel", "parallel"),
        name="diff_attn",
    )(scalars, h3, h3, v_t, norm_g).reshape(batch * seq, D_V)


def _conv_kernel(cur_ref, halo_ref, w_ref, cb_ref, g_ref, b_ref, o_ref, buf_ref, sh_ref):
    i = pl.program_id(1)

    def glu(z):
        return z[:, :CONV_WIDTH] * jax.nn.sigmoid(z[:, CONV_WIDTH:])

    halo = glu(halo_ref[...])
    buf_ref[0:CONV_HALO, :] = halo * (i > 0).astype(F32)
    buf_ref[CONV_HALO:, :] = glu(cur_ref[...])
    for b in range(1, SUBLANES):
        sh_ref[b - 1] = buf_ref[b:b + CONV_SHIFT_ROWS, :]
    first = CONV_HALO - (CONV_K - 1)
    acc = jnp.zeros((CONV_TILE, CONV_WIDTH), F32)
    for k in range(CONV_K):
        a, b = divmod(first + k, SUBLANES)
        lo = a * SUBLANES
        src = buf_ref[lo:lo + CONV_TILE, :] if b == 0 else sh_ref[b - 1, lo:lo + CONV_TILE, :]
        acc = acc + w_ref[k:k + 1, :] * src
    y = _layer_norm(acc + cb_ref[...], g_ref[...], b_ref[...])
    o_ref[...] = (y * jax.nn.sigmoid(y)).astype(BF16)


def _conv_module(h_sc, conv_w, conv_b, ln_g, ln_b, batch, seq):
    h3 = h_sc.reshape(batch, seq, SC_COLS)
    r = CONV_TILE // CONV_HALO
    return pl.pallas_call(
        _conv_kernel,
        grid=(batch, seq // CONV_TILE),
        in_specs=[
            pl.BlockSpec((None, CONV_TILE, CONV_IN), lambda b, i: (b, i, 0)),
            pl.BlockSpec((None, CONV_HALO, CONV_IN), lambda b, i: (b, jnp.maximum(i * r - 1, 0), 0)),
            _resident((CONV_K, CONV_WIDTH)),
            _resident((1, CONV_WIDTH)),
            _resident((1, CONV_WIDTH)),
            _resident((1, CONV_WIDTH)),
        ],
        out_specs=pl.BlockSpec((None, CONV_TILE, CONV_WIDTH), lambda b, i: (b, i, 0)),
        out_shape=jax.ShapeDtypeStruct((batch, seq, CONV_WIDTH), BF16),
        scratch_shapes=[pltpu.VMEM((CONV_TILE + CONV_HALO, CONV_WIDTH), F32),
                        pltpu.VMEM((SUBLANES - 1, CONV_SHIFT_ROWS, CONV_WIDTH), F32)],
        compiler_params=_params("parallel", "arbitrary"),
        name="conv_module",
    )(h3, h3, conv_w, conv_b, ln_g, ln_b).reshape(batch * seq, CONV_WIDTH)


def _s5_kernel(u_ref, bw_ref, cw_ref, air_ref, aii_ref, apr_ref, api_ref, tri_ref,
               d_ref, gw_ref, gb_ref, o_ref, hr_ref, hi_ref, xr_ref, xi_ref):
    i = pl.program_id(1)

    @pl.when(i == 0)
    def _():
        hr_ref[...] = jnp.zeros_like(hr_ref)
        hi_ref[...] = jnp.zeros_like(hi_ref)

    u = u_ref[...]
    ub = u.astype(BF16)
    tri = tri_ref[...]
    nq = S5_WIDTH // LANES
    sw = S5_NSTATE // nq
    for j in range(nq):
        sl = slice(j * sw, (j + 1) * sw)
        bu = _dot(ub[:, j * LANES:(j + 1) * LANES], bw_ref[j])
        bur, bui = bu[:, :sw], bu[:, sw:]
        air, aii = air_ref[:, sl], aii_ref[:, sl]
        zr = (air * bur - aii * bui).astype(BF16)
        zi = (air * bui + aii * bur).astype(BF16)
        cr = _dot(tri, zr) + hr_ref[:, sl]
        ci = _dot(tri, zi) + hi_ref[:, sl]
        apr, api = apr_ref[:, sl], api_ref[:, sl]
        xr_ref[:, sl] = apr * cr - api * ci
        xi_ref[:, sl] = apr * ci + api * cr
    hr_ref[...] = xr_ref[S5_TILE - 1:S5_TILE, :]
    hi_ref[...] = xi_ref[S5_TILE - 1:S5_TILE, :]
    ys = []
    for j in range(nq):
        sl = slice(j * sw, (j + 1) * sw)
        xc = jnp.concatenate([xr_ref[:, sl], xi_ref[:, sl]], axis=1).astype(BF16)
        ys.append(_dot(xc, cw_ref[j]))
    y = jnp.concatenate(ys, axis=1) + d_ref[...] * u
    y = jax.nn.gelu(y, approximate=True)
    gate = _dot(y.astype(BF16), gw_ref[...]) + gb_ref[...]
    o_ref[...] = (y * jax.nn.sigmoid(gate)).astype(BF16)


def _s5_tables(a_re, a_im, log_step, b_re, b_im, c_re, c_im):
    g, p, c = S5_GROUPS, S5_STATE, S5_GROUP
    step = jnp.exp(log_step)[:, None]
    mag = jnp.exp(a_re * step)
    abar_r, abar_i = mag * jnp.cos(a_im * step), mag * jnp.sin(a_im * step)
    den = a_re * a_re + a_im * a_im
    nr, ni = abar_r - 1.0, abar_i
    coef_r = (nr * a_re + ni * a_im) / den
    coef_i = (ni * a_re - nr * a_im) / den
    bbar_r = coef_r[..., None] * b_re - coef_i[..., None] * b_im
    bbar_i = coef_r[..., None] * b_im + coef_i[..., None] * b_re
    nq = S5_WIDTH // LANES
    gq = g // nq
    eye = jnp.eye(gq, dtype=F32)

    def in_slab(bb):
        bb = bb.reshape(nq, gq, p, c)
        return jnp.einsum('qgpc,gh->qgchp', bb, eye).reshape(nq, gq * c, gq * p)

    bw = jnp.concatenate([in_slab(bbar_r), in_slab(bbar_i)], axis=2).astype(BF16)

    def out_slab(cc):
        cc = cc.reshape(nq, gq, c, p)
        return jnp.einsum('qgcp,gh->qgphc', cc, eye).reshape(nq, gq * p, gq * c)

    cw = jnp.concatenate([out_slab(c_re), -out_slab(c_im)], axis=1).astype(BF16)
    tt = jnp.arange(1, S5_TILE + 1, dtype=F32)[:, None]
    la = (a_re * step).reshape(1, g * p)
    th = (a_im * step).reshape(1, g * p)
    pm, ang = jnp.exp(tt * la), tt * th
    im_ = jnp.exp(-tt * la)
    apr, api = pm * jnp.cos(ang), pm * jnp.sin(ang)
    air, aii = im_ * jnp.cos(ang), -im_ * jnp.sin(ang)
    return bw, cw, air, aii, apr, api


def _s5(h_sc, tables, d_skip, glu_w, glu_b, batch, seq):
    bw, cw, air, aii, apr, api = tables
    h3 = h_sc.reshape(batch, seq, SC_COLS)
    tri = jnp.tril(jnp.ones((S5_TILE, S5_TILE), F32)).astype(BF16)
    cu = CONV_IN // S5_WIDTH
    tab = (S5_TILE, S5_NSTATE)
    return pl.pallas_call(
        _s5_kernel,
        grid=(batch, seq // S5_TILE),
        in_specs=[
            pl.BlockSpec((None, S5_TILE, S5_WIDTH), lambda b, i: (b, i, cu)),
            _resident(bw.shape), _resident(cw.shape),
            _resident(tab), _resident(tab), _resident(tab), _resident(tab),
            _resident((S5_TILE, S5_TILE)),
            _resident((1, S5_WIDTH)),
            _resident((S5_WIDTH, S5_WIDTH)),
            _resident((1, S5_WIDTH)),
        ],
        out_specs=pl.BlockSpec((None, S5_TILE, S5_WIDTH), lambda b, i: (b, i, 0)),
        out_shape=jax.ShapeDtypeStruct((batch, seq, S5_WIDTH), BF16),
        scratch_shapes=[pltpu.VMEM((1, S5_NSTATE), F32), pltpu.VMEM((1, S5_NSTATE), F32),
                        pltpu.VMEM(tab, F32), pltpu.VMEM(tab, F32)],
        compiler_params=_params("parallel", "arbitrary"),
        name="s5_ssm",
    )(h3, bw, cw, air, aii, apr, api, tri, d_skip, glu_w, glu_b).reshape(batch * seq, S5_WIDTH)


def _merge_kernel(x_ref, oa_ref, ob_ref, oc_ref, od_ref, wg_ref, pa_ref, pb_ref, pc_ref, pd_ref,
                  wo_ref, g_ref, b_ref, o_ref):
    x = x_ref[...]
    xb = x.astype(BF16)
    merged = jnp.zeros((ROW_TILE, D_MODEL), F32)
    for i, (br_ref, pr_ref) in enumerate(((oa_ref, pa_ref), (ob_ref, pb_ref), (oc_ref, pc_ref), (od_ref, pd_ref))):
        gl = _dot(xb, wg_ref[:, i * D_MODEL:(i + 1) * D_MODEL])
        merged = merged + jax.nn.sigmoid(gl) * _dot(br_ref[...], pr_ref[...])
    m = _dot(merged.astype(BF16), wo_ref[...])
    o_ref[...] = _layer_norm(DEEPNORM_ALPHA * x + m, g_ref[...], b_ref[...])


def _merge(x, oa, ob, oc, od, w_gate, pa, pb, pc, pd, w_out, g, b):
    n = x.shape[0]
    row = lambda w: pl.BlockSpec((ROW_TILE, w), lambda i: (i, 0))
    return pl.pallas_call(
        _merge_kernel,
        grid=(n // ROW_TILE,),
        in_specs=[
            row(D_MODEL), row(A_Q), row(S5_WIDTH), row(CONV_WIDTH), row(D_V),
            _resident((D_MODEL, GATE_COLS)),
            _resident((A_Q, D_MODEL)), _resident((S5_WIDTH, D_MODEL)),
            _resident((CONV_WIDTH, D_MODEL)), _resident((D_V, D_MODEL)),
            _resident((D_MODEL, D_MODEL)),
            _resident((1, D_MODEL)), _resident((1, D_MODEL)),
        ],
        out_specs=row(D_MODEL),
        out_shape=jax.ShapeDtypeStruct((n, D_MODEL), F32),
        compiler_params=_params("parallel"),
        name="gated_merge",
    )(x, oa, ob, oc, od, w_gate, pa, pb, pc, pd, w_out, g, b)


def _mem_kv_kernel(m_ref, g_ref, b_ref, w_ref, o_ref):
    mn = _layer_norm(m_ref[...], g_ref[...], b_ref[...])
    o_ref[...] = _dot(mn.astype(BF16), w_ref[...]).astype(BF16)


def _mem_kv(mem, g, b, wkv):
    n = mem.shape[0]
    return pl.pallas_call(
        _mem_kv_kernel,
        grid=(DEPTH, n // ROW_TILE),
        in_specs=[
            pl.BlockSpec((ROW_TILE, D_MODEL), lambda l, i: (i, 0)),
            _resident((1, D_MODEL)), _resident((1, D_MODEL)),
            pl.BlockSpec((None, D_MODEL, 2 * CROSS_WIDTH), lambda l, i: (l, 0, 0)),
        ],
        out_specs=pl.BlockSpec((None, ROW_TILE, 2 * CROSS_WIDTH), lambda l, i: (l, i, 0)),
        out_shape=jax.ShapeDtypeStruct((DEPTH, n, 2 * CROSS_WIDTH), BF16),
        compiler_params=_params("parallel", "parallel"),
        name="mem_kv",
    )(mem, g, b, wkv)


def _cross_kernel(x_ref, kv_ref, wq_ref, wo_ref, g_ref, b_ref, o_ref):
    x = x_ref[...]
    q = _dot(x.astype(BF16), wq_ref[...]).astype(BF16)
    kv = kv_ref[...]
    scale = 1.0 / math.sqrt(CROSS_HEAD_DIM)
    outs = []
    for h in range(CROSS_HEADS):
        sl = slice(h * CROSS_HEAD_DIM, (h + 1) * CROSS_HEAD_DIM)
        s = _dot_nt(q[:, sl], kv[:, sl]) * scale
        p = jnp.exp(s - jnp.max(s, axis=-1, keepdims=True))
        p = p / jnp.sum(p, axis=-1, keepdims=True)
        outs.append(_dot(p.astype(BF16), kv[:, CROSS_WIDTH + h * CROSS_HEAD_DIM:CROSS_WIDTH + (h + 1) * CROSS_HEAD_DIM]))
    o = jnp.concatenate(outs, axis=1).astype(BF16)
    c = _dot(o, wo_ref[...])
    o_ref[...] = _layer_norm(DEEPNORM_ALPHA * x + c, g_ref[...], b_ref[...])


def _cross_attn(x, kv, wq, wo, g, b, batch, seq):
    x3 = x.reshape(batch, seq, D_MODEL)
    kv3 = kv.reshape(batch, MEM_LEN, 2 * CROSS_WIDTH)
    return pl.pallas_call(
        _cross_kernel,
        grid=(batch, seq // CROSS_TILE),
        in_specs=[
            pl.BlockSpec((None, CROSS_TILE, D_MODEL), lambda b, i: (b, i, 0)),
            pl.BlockSpec((None, MEM_LEN, 2 * CROSS_WIDTH), lambda b, i: (b, 0, 0)),
            _resident((D_MODEL, CROSS_WIDTH)),
            _resident((CROSS_WIDTH, D_MODEL)),
            _resident((1, D_MODEL)), _resident((1, D_MODEL)),
        ],
        out_specs=pl.BlockSpec((None, CROSS_TILE, D_MODEL), lambda b, i: (b, i, 0)),
        out_shape=jax.ShapeDtypeStruct((batch, seq, D_MODEL), F32),
        compiler_params=_params("parallel", "parallel"),
        name="cross_attn",
    )(x3, kv3, wq, wo, g, b).reshape(batch * seq, D_MODEL)


def _row(v):
    return v.reshape(1, -1).astype(F32)


def kernel(x, mem, ffn1_w_in, ffn1_w_out, ffn1_ln_g, ffn1_ln_b, mix_w_in, swa_sinks, swa_proj, s5_a_re, s5_a_im, s5_log_step, s5_b_re, s5_b_im, s5_c_re, s5_c_im, s5_d, s5_glu_w, s5_glu_b, s5_proj, conv_w, conv_b, conv_ln_g, conv_ln_b, conv_proj, diff_lq1, diff_lk1, diff_lq2, diff_lk2, diff_norm_g, diff_proj, mix_w_out, mix_ln_g, mix_ln_b, mem_ln_g, mem_ln_b, cross_wq, cross_wkv, cross_wo, cross_ln_g, cross_ln_b, ffn2_w_in, ffn2_w_out, ffn2_ln_g, ffn2_ln_b):
    batch, seq, _ = x.shape
    n = batch * seq
    assert seq % DIFF_TILE == 0 and seq % CONV_TILE == 0 and seq % ROW_TILE == 0 and seq % CROSS_TILE == 0
    assert seq % S5_TILE == 0 and seq % BLOCK == 0 and (batch * MEM_LEN) % ROW_TILE == 0
    h = x.reshape(n, D_MODEL)
    kv_all = _mem_kv(mem.reshape(batch * MEM_LEN, D_MODEL), _row(mem_ln_g), _row(mem_ln_b),
                     cross_wkv.astype(BF16))
    for l in range(DEPTH):
        h = _ffn_ln(h, ffn1_w_in[l].astype(BF16), ffn1_w_out[l].astype(BF16),
                    _row(ffn1_ln_g[l]), _row(ffn1_ln_b[l]))
        lambda_init = 0.8 - 0.6 * math.exp(-0.3 * l)
        w = mix_w_in[l]
        cols = lambda off, width: w[:, off:off + width]
        w_att = jnp.concatenate([cols(OFF_DQ, D_QK), cols(OFF_DK, D_QK),
                                 cols(OFF_AQ, A_Q), cols(OFF_AK, A_KV), cols(OFF_AV, A_KV)], axis=1).astype(BF16)
        w_dv_t = cols(OFF_DV, D_V).T.astype(BF16)
        w_sc = jnp.concatenate([cols(OFF_CU, CONV_IN), cols(OFF_SU, S5_WIDTH)], axis=1).astype(BF16)
        w_gate = cols(OFF_GL, GATE_COLS).astype(BF16)
        h_att, h_sc, v_t = _mix_proj(h, w_att, w_sc, w_dv_t, batch, seq)
        o_a = _swa(h_att, swa_sinks[l].astype(F32), batch, seq)
        tables = _s5_tables(s5_a_re[l], s5_a_im[l], s5_log_step[l], s5_b_re[l], s5_b_im[l], s5_c_re[l], s5_c_im[l])
        o_b = _s5(h_sc, tables, _row(s5_d[l]), s5_glu_w[l].astype(BF16), _row(s5_glu_b[l]), batch, seq)
        o_c = _conv_module(h_sc, conv_w[l], _row(conv_b[l]), _row(conv_ln_g[l]), _row(conv_ln_b[l]), batch, seq)
        lam = (jnp.exp(jnp.sum(diff_lq1[l] * diff_lk1[l])) - jnp.exp(jnp.sum(diff_lq2[l] * diff_lk2[l]))
               + lambda_init).astype(F32)
        o_d = _diff_attn(h_att, v_t, jnp.stack([lam, jnp.float32(1.0 - lambda_init)]), _row(diff_norm_g[l]), batch, seq)
        h = _merge(h, o_a, o_b, o_c, o_d, w_gate, swa_proj[l].astype(BF16), s5_proj[l].astype(BF16),
                   conv_proj[l].astype(BF16), diff_proj[l].astype(BF16), mix_w_out[l].astype(BF16),
                   _row(mix_ln_g[l]), _row(mix_ln_b[l]))
        h = _cross_attn(h, kv_all[l], cross_wq[l].astype(BF16), cross_wo[l].astype(BF16),
                        _row(cross_ln_g[l]), _row(cross_ln_b[l]), batch, seq)
        h = _ffn_ln(h, ffn2_w_in[l].astype(BF16), ffn2_w_out[l].astype(BF16),
                    _row(ffn2_ln_g[l]), _row(ffn2_ln_b[l]))
    return h.reshape(batch, seq, D_MODEL)
```

```python
import functools
import math

import jax
import jax.numpy as jnp
from jax import lax
from jax.experimental import pallas as pl
from jax.experimental.pallas import tpu as pltpu

F32 = jnp.float32
BF16 = jnp.bfloat16

D_MODEL = 1024
DEPTH = 4
MEM_LEN = 256
HEAD_DIM = 64
BLOCK = 128
SWA_HEADS = 8
SWA_KV_HEADS = 2
SWA_REP = SWA_HEADS // SWA_KV_HEADS
S5_WIDTH = 512
S5_GROUP = 16
S5_GROUPS = S5_WIDTH // S5_GROUP
S5_STATE = 64
S5_NSTATE = S5_GROUPS * S5_STATE
CONV_WIDTH = 512
CONV_K = 31
DIFF_HEADS = 4
DIFF_V_DIM = 2 * HEAD_DIM
CROSS_HEADS = 4
CROSS_HEAD_DIM = 128
CROSS_WIDTH = CROSS_HEADS * CROSS_HEAD_DIM
FFN_DIM = 2816
N_BRANCHES = 4

A_Q = SWA_HEADS * HEAD_DIM
A_KV = SWA_KV_HEADS * HEAD_DIM
D_QK = DIFF_HEADS * 2 * HEAD_DIM
D_V = DIFF_HEADS * DIFF_V_DIM
CONV_IN = 2 * CONV_WIDTH
GATE_COLS = N_BRANCHES * D_MODEL
OFF_AQ = 0
OFF_AK = OFF_AQ + A_Q
OFF_AV = OFF_AK + A_KV
OFF_DQ = OFF_AV + A_KV
OFF_DK = OFF_DQ + D_QK
OFF_DV = OFF_DK + D_QK
OFF_SU = OFF_DV + D_V
OFF_CU = OFF_SU + S5_WIDTH
OFF_GL = OFF_CU + CONV_IN
ATT_COLS = 2 * D_QK + A_KV
VT_ROWS = D_V + A_KV
SC_COLS = CONV_IN + S5_WIDTH

DEEPNORM_ALPHA = (2.0 * DEPTH) ** 0.25
LN_EPS = 1e-5
NEG_INF = -1e30

LANES = 128
SUBLANES = 8
MXU_WIDTH = 256
VMEM_LIMIT = 56 * 1024 * 1024

ROW_TILE = 512
FFN_CHUNK = MXU_WIDTH
DIFF_TILE = 512
ONES_ROWS = 16
DIFF_VROWS = DIFF_V_DIM + ONES_ROWS
SWA_TILE = 512
LOG2E = math.log2(math.e)
CONV_TILE = 512
CONV_HALO = 32
CONV_SHIFT_ROWS = CONV_TILE + CONV_HALO - SUBLANES
S5_TILE = 256
CROSS_TILE = 512


def _alibi_slope(h, n):
    return 2.0 ** (-8.0 * (h + 1) / n)


def _layer_norm(z, g, b):
    mu = jnp.mean(z, axis=-1, keepdims=True)
    zc = z - mu
    var = jnp.mean(zc * zc, axis=-1, keepdims=True)
    return zc * lax.rsqrt(var + LN_EPS) * g + b


def _dot(a, b):
    return jnp.dot(a, b, preferred_element_type=F32)


def _dot_nt(a, b):
    return lax.dot_general(a, b, (((1,), (1,)), ((), ())), preferred_element_type=F32)


def _params(*sem):
    return pltpu.CompilerParams(dimension_semantics=sem, vmem_limit_bytes=VMEM_LIMIT)


def _resident(shape):
    nd = len(shape)
    return pl.BlockSpec(shape, lambda *_: (0,) * nd, pipeline_mode=pl.Buffered(1))


def _ffn_ln_kernel(x_ref, wg_ref, wu_ref, wo_ref, g_ref, b_ref, o_ref, a_ref):
    x = x_ref[...]
    xb = x.astype(BF16)
    for c in range(FFN_DIM // FFN_CHUNK):
        sl = slice(c * FFN_CHUNK, (c + 1) * FFN_CHUNK)
        gate = _dot(xb, wg_ref[:, sl])
        up = _dot(xb, wu_ref[:, sl])
        a_ref[:, sl] = (gate * jax.nn.sigmoid(gate) * up).astype(BF16)
    f = _dot(a_ref[...], wo_ref[...])
    o_ref[...] = _layer_norm(DEEPNORM_ALPHA * x + 0.5 * f, g_ref[...], b_ref[...])


def _ffn_ln(x, w_in, w_out, g, b):
    n = x.shape[0]
    return pl.pallas_call(
        _ffn_ln_kernel,
        grid=(n // ROW_TILE,),
        in_specs=[
            pl.BlockSpec((ROW_TILE, D_MODEL), lambda i: (i, 0)),
            pl.BlockSpec((D_MODEL, FFN_DIM), lambda i: (0, 0), pipeline_mode=pl.Buffered(1)),
            pl.BlockSpec((D_MODEL, FFN_DIM), lambda i: (0, 1), pipeline_mode=pl.Buffered(1)),
            _resident((FFN_DIM, D_MODEL)),
            _resident((1, D_MODEL)),
            _resident((1, D_MODEL)),
        ],
        out_specs=pl.BlockSpec((ROW_TILE, D_MODEL), lambda i: (i, 0)),
        out_shape=jax.ShapeDtypeStruct((n, D_MODEL), F32),
        scratch_shapes=[pltpu.VMEM((ROW_TILE, FFN_DIM), BF16)],
        compiler_params=_params("parallel"),
        name="ffn_ln",
    )(x, w_in, w_in, w_out, g, b)


def _mix_proj_kernel(x_ref, wa_ref, wq_ref, ws_ref, wvt_ref, ha_ref, q_ref, hs_ref, vt_ref):
    xb = x_ref[...].astype(BF16)
    ha_ref[...] = _dot(xb, wa_ref[...]).astype(BF16)
    q = _dot(xb, wq_ref[...])
    for h in range(SWA_HEADS):
        q_ref[h] = q[:, h * LANES:(h + 1) * LANES].astype(BF16)
    hs_ref[...] = _dot(xb, ws_ref[...])
    vt_ref[...] = _dot_nt(wvt_ref[...], xb).astype(BF16)


def _mix_proj(x, w_att, w_q, w_sc, w_v_t, batch, seq):
    n = x.shape[0]
    per_seq = seq // ROW_TILE
    return pl.pallas_call(
        _mix_proj_kernel,
        grid=(n // ROW_TILE,),
        in_specs=[
            pl.BlockSpec((ROW_TILE, D_MODEL), lambda i: (i, 0)),
            _resident((D_MODEL, ATT_COLS)),
            _resident((D_MODEL, SWA_HEADS * LANES)),
            _resident((D_MODEL, SC_COLS)),
            _resident((VT_ROWS, D_MODEL)),
        ],
        out_specs=[
            pl.BlockSpec((ROW_TILE, ATT_COLS), lambda i: (i, 0)),
            pl.BlockSpec((SWA_HEADS, ROW_TILE, LANES), lambda i: (0, i, 0)),
            pl.BlockSpec((ROW_TILE, SC_COLS), lambda i: (i, 0)),
            pl.BlockSpec((None, VT_ROWS, ROW_TILE), lambda i: (i // per_seq, 0, i % per_seq)),
        ],
        out_shape=[
            jax.ShapeDtypeStruct((n, ATT_COLS), BF16),
            jax.ShapeDtypeStruct((SWA_HEADS, n, LANES), BF16),
            jax.ShapeDtypeStruct((n, SC_COLS), F32),
            jax.ShapeDtypeStruct((batch, VT_ROWS, seq), BF16),
        ],
        compiler_params=_params("parallel"),
        name="mix_proj",
    )(x, w_att, w_q, w_sc, w_v_t)


def _swa_q_weight(w_aq):
    d = w_aq.shape[0]
    w = (w_aq * (LOG2E / math.sqrt(HEAD_DIM))).reshape(d, SWA_KV_HEADS, SWA_REP, 1, HEAD_DIM)
    sel = jnp.eye(SWA_KV_HEADS, dtype=w.dtype).reshape(1, SWA_KV_HEADS, 1, SWA_KV_HEADS, 1)
    return (w * sel).reshape(d, SWA_HEADS * LANES)


def _swa_kernel(sink_ref, bias_ref, q_ref, kc_ref, kp_ref, vc_ref, vp_ref, o_ref):
    i = pl.program_id(1)
    gw = SWA_REP * BLOCK
    k_all = jnp.concatenate([kp_ref[...], kc_ref[...]], axis=0)
    v_all = jnp.concatenate([vp_ref[...], vc_ref[...]], axis=1)
    vx_all = jnp.concatenate([v_all, jnp.ones((ONES_ROWS, v_all.shape[1]), BF16)], axis=0)
    no_prev = jnp.where(i == 0, NEG_INF, 0.0)

    def scores(blk, g):
        lo = blk * BLOCK
        qg = q_ref[g * SWA_REP:(g + 1) * SWA_REP, lo:lo + BLOCK, :].reshape(gw, LANES)
        s = _dot_nt(k_all[lo:lo + 2 * BLOCK], qg) + bias_ref[g]
        if blk == 0:
            s = jnp.concatenate([s[:BLOCK] + no_prev, s[BLOCK:]], axis=0)
        return s

    units = [(blk, g) for blk in range(SWA_TILE // BLOCK) for g in range(SWA_KV_HEADS)]
    s_next = scores(*units[0])
    outs = []
    for u, (blk, g) in enumerate(units):
        s = s_next
        if u + 1 < len(units):
            s_next = scores(*units[u + 1])
        lo = blk * BLOCK
        sink = sink_ref[g]
        m = jnp.maximum(jnp.max(s, axis=0, keepdims=True), sink)
        e = jnp.exp2(s - m).astype(BF16)
        pv = _dot(vx_all[:, lo:lo + 2 * BLOCK], e)
        den = pv[LANES:LANES + 1] + jnp.exp2(sink - m)
        og = pv[g * HEAD_DIM:(g + 1) * HEAD_DIM] * (1.0 / den)
        outs += [og[:, r * BLOCK:(r + 1) * BLOCK] for r in range(SWA_REP)]
        if g == SWA_KV_HEADS - 1:
            o_ref[lo:lo + BLOCK, :] = jnp.concatenate(outs, axis=0).T.astype(BF16)
            outs = []


def _swa_bias():
    kj = jnp.arange(2 * BLOCK)[:, None]
    qi = jnp.arange(BLOCK)[None, :]
    dist = BLOCK + qi - kj
    valid = (dist >= 0) & (dist < BLOCK)
    slopes = jnp.asarray([_alibi_slope(h, SWA_HEADS) * LOG2E for h in range(SWA_HEADS)], F32)
    b = jnp.where(valid[None], -slopes[:, None, None] * dist[None].astype(F32), NEG_INF)
    b = b.reshape(SWA_KV_HEADS, SWA_REP, 2 * BLOCK, BLOCK)
    return jnp.transpose(b, (0, 2, 1, 3)).reshape(SWA_KV_HEADS, 2 * BLOCK, SWA_REP * BLOCK)


def _swa(h_att, q_swa, v_t, sinks, batch, seq):
    h3 = h_att.reshape(batch, seq, ATT_COLS)
    ck = (2 * D_QK) // A_KV
    rv = D_V // A_KV
    per_seq = seq // SWA_TILE
    r = SWA_TILE // BLOCK
    gw = SWA_REP * BLOCK
    sink_rows = jnp.repeat(sinks.astype(F32).reshape(SWA_KV_HEADS, SWA_REP) * LOG2E, BLOCK, axis=1)
    prev = lambda i: jnp.maximum(i * r - 1, 0)
    return pl.pallas_call(
        _swa_kernel,
        grid=(batch, per_seq),
        in_specs=[
            _resident((SWA_KV_HEADS, 1, gw)),
            _resident((SWA_KV_HEADS, 2 * BLOCK, gw)),
            pl.BlockSpec((SWA_HEADS, SWA_TILE, LANES), lambda b, i: (0, b * per_seq + i, 0)),
            pl.BlockSpec((None, SWA_TILE, A_KV), lambda b, i: (b, i, ck)),
            pl.BlockSpec((None, BLOCK, A_KV), lambda b, i: (b, prev(i), ck)),
            pl.BlockSpec((None, A_KV, SWA_TILE), lambda b, i: (b, rv, i)),
            pl.BlockSpec((None, A_KV, BLOCK), lambda b, i: (b, rv, prev(i))),
        ],
        out_specs=pl.BlockSpec((None, SWA_TILE, A_Q), lambda b, i: (b, i, 0)),
        out_shape=jax.ShapeDtypeStruct((batch, seq, A_Q), BF16),
        compiler_params=_params("parallel", "parallel"),
        name="swa",
    )(sink_rows.reshape(SWA_KV_HEADS, 1, gw), _swa_bias(), q_swa, h3, h3, v_t, v_t).reshape(batch * seq, A_Q)


def _diff_kernel(sc_ref, q_ref, k_ref, vt_ref, g_ref, o_ref,
                 qs_ref, vx_ref, b_ref, s_ref, mx_ref, m_ref, acc_ref, *, n_tiles):
    h = pl.program_id(1)
    t = DIFF_TILE
    v_dim = DIFF_V_DIM
    slope = jnp.float32(_alibi_slope(DIFF_HEADS - 1, DIFF_HEADS) * LOG2E)
    for hh in range(DIFF_HEADS - 1):
        slope = jnp.where(h == hh, jnp.float32(_alibi_slope(hh, DIFF_HEADS) * LOG2E), slope)

    for j in range(n_tiles):
        vx_ref[j, :v_dim, :] = vt_ref[:, j * t:(j + 1) * t]
        vx_ref[j, v_dim:, :] = jnp.ones((DIFF_VROWS - v_dim, t), BF16)
    krow = lax.broadcasted_iota(jnp.int32, (t, t), 0)
    qcol = lax.broadcasted_iota(jnp.int32, (t, t), 1)
    b_ref[0] = slope * krow.astype(F32)
    b_ref[1] = jnp.where(qcol >= krow, slope * krow.astype(F32), NEG_INF)

    def scores(j, c, qt):
        k0 = pl.multiple_of(j * t, t)
        s = _dot_nt(k_ref[pl.ds(k0, t), :], qs_ref[c]) + b_ref[jnp.where(j == qt, 1, 0)]
        s_ref[c] = s
        mx_ref[c] = jnp.max(s, axis=0, keepdims=True)

    def accumulate(j, c):
        mx = mx_ref[c]
        e = jnp.exp2(s_ref[c] - mx).astype(BF16)
        pv = _dot(vx_ref[j], e)
        mxo = mx + slope * (j * t).astype(F32)
        m_old = m_ref[c]
        m_new = jnp.maximum(m_old, mxo)
        m_ref[c] = m_new
        acc_ref[c] = jnp.exp2(m_old - m_new) * acc_ref[c] + jnp.exp2(mxo - m_new) * pv

    def q_tile(qt, carry):
        q0 = pl.multiple_of(qt * t, t)
        lane = lax.broadcasted_iota(jnp.int32, (t, 2 * HEAD_DIM), 1)
        q = q_ref[pl.ds(q0, t), :].astype(F32) * (LOG2E / math.sqrt(HEAD_DIM))
        qs_ref[0] = jnp.where(lane < HEAD_DIM, q, 0.0).astype(BF16)
        qs_ref[1] = jnp.where(lane >= HEAD_DIM, q, 0.0).astype(BF16)
        m_ref[...] = jnp.full(m_ref.shape, NEG_INF, F32)
        acc_ref[...] = jnp.zeros(acc_ref.shape, F32)
        scores(jnp.int32(0), 0, qt)

        def body(kt, carry):
            scores(kt, 1, qt)
            accumulate(kt, 0)
            scores(kt + 1, 0, qt)
            accumulate(kt, 1)
            return carry

        lax.fori_loop(0, qt, body, 0)
        scores(qt, 1, qt)
        accumulate(qt, 0)
        accumulate(qt, 1)

        def normalised(c):
            a = acc_ref[c]
            return a[:v_dim] * (1.0 / a[v_dim:v_dim + 1])

        o = (normalised(0) - sc_ref[0] * normalised(1)).T
        o = o * lax.rsqrt(jnp.mean(o * o, axis=-1, keepdims=True) + LN_EPS) * g_ref[...] * sc_ref[1]
        o_ref[pl.ds(q0, t), :] = o.astype(BF16)
        return carry

    lax.fori_loop(0, n_tiles, q_tile, 0)


def _diff_attn(h_att, v_t, scalars, norm_g, batch, seq):
    h3 = h_att.reshape(batch, seq, ATT_COLS)
    w = 2 * HEAD_DIM
    t = DIFF_TILE
    n_tiles = seq // t
    return pl.pallas_call(
        functools.partial(_diff_kernel, n_tiles=n_tiles),
        grid=(batch, DIFF_HEADS),
        in_specs=[
            pl.BlockSpec(memory_space=pltpu.SMEM),
            pl.BlockSpec((None, seq, w), lambda b, h: (b, 0, h)),
            pl.BlockSpec((None, seq, w), lambda b, h: (b, 0, DIFF_HEADS + h)),
            pl.BlockSpec((None, DIFF_V_DIM, seq), lambda b, h: (b, h, 0)),
            _resident((1, DIFF_V_DIM)),
        ],
        out_specs=pl.BlockSpec((None, seq, w), lambda b, h: (b, 0, h)),
        out_shape=jax.ShapeDtypeStruct((batch, seq, D_V), BF16),
        scratch_shapes=[
            pltpu.VMEM((2, t, w), BF16),
            pltpu.VMEM((n_tiles, DIFF_VROWS, t), BF16),
            pltpu.VMEM((2, t, t), F32),
            pltpu.VMEM((2, t, t), F32),
            pltpu.VMEM((2, 1, t), F32),
            pltpu.VMEM((2, 1, t), F32),
            pltpu.VMEM((2, DIFF_VROWS, t), F32),
        ],
        compiler_params=_params("parallel", "parallel"),
        name="diff_attn",
    )(scalars, h3, h3, v_t, norm_g).reshape(batch * seq, D_V)


def _conv_kernel(cur_ref, halo_ref, w_ref, cb_ref, g_ref, b_ref, o_ref, buf_ref, sh_ref):
    i = pl.program_id(1)

    def glu(z):
        return z[:, :CONV_WIDTH] * jax.nn.sigmoid(z[:, CONV_WIDTH:])

    halo = glu(halo_ref[...])
    buf_ref[0:CONV_HALO, :] = halo * (i > 0).astype(F32)
    buf_ref[CONV_HALO:, :] = glu(cur_ref[...])
    for b in range(1, SUBLANES):
        sh_ref[b - 1] = buf_ref[b:b + CONV_SHIFT_ROWS, :]
    first = CONV_HALO - (CONV_K - 1)
    acc = jnp.zeros((CONV_TILE, CONV_WIDTH), F32)
    for k in range(CONV_K):
        a, b = divmod(first + k, SUBLANES)
        lo = a * SUBLANES
        src = buf_ref[lo:lo + CONV_TILE, :] if b == 0 else sh_ref[b - 1, lo:lo + CONV_TILE, :]
        acc = acc + w_ref[k:k + 1, :] * src
    y = _layer_norm(acc + cb_ref[...], g_ref[...], b_ref[...])
    o_ref[...] = (y * jax.nn.sigmoid(y)).astype(BF16)


def _conv_module(h_sc, conv_w, conv_b, ln_g, ln_b, batch, seq):
    h3 = h_sc.reshape(batch, seq, SC_COLS)
    r = CONV_TILE // CONV_HALO
    return pl.pallas_call(
        _conv_kernel,
        grid=(batch, seq // CONV_TILE),
        in_specs=[
            pl.BlockSpec((None, CONV_TILE, CONV_IN), lambda b, i: (b, i, 0)),
            pl.BlockSpec((None, CONV_HALO, CONV_IN), lambda b, i: (b, jnp.maximum(i * r - 1, 0), 0)),
            _resident((CONV_K, CONV_WIDTH)),
            _resident((1, CONV_WIDTH)),
            _resident((1, CONV_WIDTH)),
            _resident((1, CONV_WIDTH)),
        ],
        out_specs=pl.BlockSpec((None, CONV_TILE, CONV_WIDTH), lambda b, i: (b, i, 0)),
        out_shape=jax.ShapeDtypeStruct((batch, seq, CONV_WIDTH), BF16),
        scratch_shapes=[pltpu.VMEM((CONV_TILE + CONV_HALO, CONV_WIDTH), F32),
                        pltpu.VMEM((SUBLANES - 1, CONV_SHIFT_ROWS, CONV_WIDTH), F32)],
        compiler_params=_params("parallel", "arbitrary"),
        name="conv_module",
    )(h3, h3, conv_w, conv_b, ln_g, ln_b).reshape(batch * seq, CONV_WIDTH)


def _s5_kernel(u_ref, bw_ref, cw_ref, air_ref, aii_ref, apr_ref, api_ref, ah_ref, tri_ref,
               d_ref, gw_ref, gb_ref, o_ref, hr_ref, hi_ref):
    i = pl.program_id(1)

    @pl.when(i == 0)
    def _():
        hr_ref[...] = jnp.zeros_like(hr_ref)
        hi_ref[...] = jnp.zeros_like(hi_ref)

    u = u_ref[...]
    ub = u.astype(BF16)
    tri = tri_ref[...]
    nq = S5_WIDTH // LANES
    sw = S5_NSTATE // nq

    def in_proj(j):
        return _dot(ub[:, j * LANES:(j + 1) * LANES], bw_ref[j])

    bu_next = in_proj(0)
    xc_prev = None
    ys = []
    for j in range(nq):
        sl = slice(j * sw, (j + 1) * sw)
        bu = bu_next
        if j + 1 < nq:
            bu_next = in_proj(j + 1)
        bur, bui = bu[:, :sw], bu[:, sw:]
        air, aii = air_ref[:, sl], aii_ref[:, sl]
        zr = (air * bur - aii * bui).astype(BF16)
        zi = (air * bui + aii * bur).astype(BF16)
        cr = _dot(tri, zr) + hr_ref[:, sl]
        ci = _dot(tri, zi) + hi_ref[:, sl]
        if xc_prev is not None:
            ys.append(_dot(xc_prev, cw_ref[j - 1]))
        apr, api = apr_ref[:, sl], api_ref[:, sl]
        xr = apr * cr - api * ci
        xi = apr * ci + api * cr
        lr, li = xr[S5_TILE - 1:, :], xi[S5_TILE - 1:, :]
        ahr, ahi = ah_ref[0:1, sl], ah_ref[1:2, sl]
        hr_ref[:, sl] = ahr * lr - ahi * li
        hi_ref[:, sl] = ahr * li + ahi * lr
        xc_prev = jnp.concatenate([xr, xi], axis=1).astype(BF16)
    ys.append(_dot(xc_prev, cw_ref[nq - 1]))
    y = jnp.concatenate(ys, axis=1) + d_ref[...] * u
    y = jax.nn.gelu(y, approximate=True)
    gate = _dot(y.astype(BF16), gw_ref[...]) + gb_ref[...]
    o_ref[...] = (y * jax.nn.sigmoid(gate)).astype(BF16)


def _s5_tables(a_re, a_im, log_step, b_re, b_im, c_re, c_im):
    g, p, c = S5_GROUPS, S5_STATE, S5_GROUP
    step = jnp.exp(log_step)[:, None]
    mag = jnp.exp(a_re * step)
    abar_r, abar_i = mag * jnp.cos(a_im * step), mag * jnp.sin(a_im * step)
    den = a_re * a_re + a_im * a_im
    nr, ni = abar_r - 1.0, abar_i
    coef_r = (nr * a_re + ni * a_im) / den
    coef_i = (ni * a_re - nr * a_im) / den
    bbar_r = coef_r[..., None] * b_re - coef_i[..., None] * b_im
    bbar_i = coef_r[..., None] * b_im + coef_i[..., None] * b_re
    nq = S5_WIDTH // LANES
    gq = g // nq
    eye = jnp.eye(gq, dtype=F32)

    def in_slab(bb):
        bb = bb.reshape(nq, gq, p, c)
        return jnp.einsum('qgpc,gh->qgchp', bb, eye).reshape(nq, gq * c, gq * p)

    bw = jnp.concatenate([in_slab(bbar_r), in_slab(bbar_i)], axis=2).astype(BF16)

    def out_slab(cc):
        cc = cc.reshape(nq, gq, c, p)
        return jnp.einsum('qgcp,gh->qgphc', cc, eye).reshape(nq, gq * p, gq * c)

    cw = jnp.concatenate([out_slab(c_re), -out_slab(c_im)], axis=1).astype(BF16)
    half = S5_TILE // 2
    tt = jnp.arange(1 - half, S5_TILE + 1 - half, dtype=F32)[:, None]
    la = (a_re * step).reshape(1, g * p)
    th = (a_im * step).reshape(1, g * p)
    pm, ang = jnp.exp(tt * la), tt * th
    im_ = jnp.exp(-tt * la)
    apr, api = pm * jnp.cos(ang), pm * jnp.sin(ang)
    air, aii = im_ * jnp.cos(ang), -im_ * jnp.sin(ang)
    hm = jnp.exp(half * la)
    ah = jnp.concatenate([hm * jnp.cos(half * th), hm * jnp.sin(half * th)], axis=0)
    return bw, cw, air, aii, apr, api, ah


def _s5(h_sc, tables, d_skip, glu_w, glu_b, batch, seq):
    bw, cw, air, aii, apr, api, ah = tables
    h3 = h_sc.reshape(batch, seq, SC_COLS)
    tri = jnp.tril(jnp.ones((S5_TILE, S5_TILE), F32)).astype(BF16)
    cu = CONV_IN // S5_WIDTH
    tab = (S5_TILE, S5_NSTATE)
    return pl.pallas_call(
        _s5_kernel,
        grid=(batch, seq // S5_TILE),
        in_specs=[
            pl.BlockSpec((None, S5_TILE, S5_WIDTH), lambda b, i: (b, i, cu)),
            _resident(bw.shape), _resident(cw.shape),
            _resident(tab), _resident(tab), _resident(tab), _resident(tab),
            _resident((2, S5_NSTATE)),
            _resident((S5_TILE, S5_TILE)),
            _resident((1, S5_WIDTH)),
            _resident((S5_WIDTH, S5_WIDTH)),
            _resident((1, S5_WIDTH)),
        ],
        out_specs=pl.BlockSpec((None, S5_TILE, S5_WIDTH), lambda b, i: (b, i, 0)),
        out_shape=jax.ShapeDtypeStruct((batch, seq, S5_WIDTH), BF16),
        scratch_shapes=[pltpu.VMEM((1, S5_NSTATE), F32), pltpu.VMEM((1, S5_NSTATE), F32)],
        compiler_params=_params("parallel", "arbitrary"),
        name="s5_ssm",
    )(h3, bw, cw, air, aii, apr, api, ah, tri, d_skip, glu_w, glu_b).reshape(batch * seq, S5_WIDTH)


def _merge_kernel(x_ref, oa_ref, ob_ref, oc_ref, od_ref, wg_ref, pa_ref, pb_ref, pc_ref, pd_ref,
                  wo_ref, g_ref, b_ref, o_ref):
    x = x_ref[...]
    xb = x.astype(BF16)
    merged = jnp.zeros((ROW_TILE, D_MODEL), F32)
    for i, (br_ref, pr_ref) in enumerate(((oa_ref, pa_ref), (ob_ref, pb_ref), (oc_ref, pc_ref), (od_ref, pd_ref))):
        gl = _dot(xb, wg_ref[:, i * D_MODEL:(i + 1) * D_MODEL])
        merged = merged + jax.nn.sigmoid(gl) * _dot(br_ref[...], pr_ref[...])
    m = _dot(merged.astype(BF16), wo_ref[...])
    o_ref[...] = _layer_norm(DEEPNORM_ALPHA * x + m, g_ref[...], b_ref[...])


def _merge(x, oa, ob, oc, od, w_gate, pa, pb, pc, pd, w_out, g, b):
    n = x.shape[0]
    row = lambda w: pl.BlockSpec((ROW_TILE, w), lambda i: (i, 0))
    return pl.pallas_call(
        _merge_kernel,
        grid=(n // ROW_TILE,),
        in_specs=[
            row(D_MODEL), row(A_Q), row(S5_WIDTH), row(CONV_WIDTH), row(D_V),
            _resident((D_MODEL, GATE_COLS)),
            _resident((A_Q, D_MODEL)), _resident((S5_WIDTH, D_MODEL)),
            _resident((CONV_WIDTH, D_MODEL)), _resident((D_V, D_MODEL)),
            _resident((D_MODEL, D_MODEL)),
            _resident((1, D_MODEL)), _resident((1, D_MODEL)),
        ],
        out_specs=row(D_MODEL),
        out_shape=jax.ShapeDtypeStruct((n, D_MODEL), F32),
        compiler_params=_params("parallel"),
        name="gated_merge",
    )(x, oa, ob, oc, od, w_gate, pa, pb, pc, pd, w_out, g, b)


def _mem_kv_kernel(m_ref, g_ref, b_ref, w_ref, o_ref):
    mn = _layer_norm(m_ref[...], g_ref[...], b_ref[...])
    o_ref[...] = _dot(mn.astype(BF16), w_ref[...]).astype(BF16)


def _mem_kv(mem, g, b, wkv):
    n = mem.shape[0]
    return pl.pallas_call(
        _mem_kv_kernel,
        grid=(DEPTH, n // ROW_TILE),
        in_specs=[
            pl.BlockSpec((ROW_TILE, D_MODEL), lambda l, i: (i, 0)),
            _resident((1, D_MODEL)), _resident((1, D_MODEL)),
            pl.BlockSpec((None, D_MODEL, 2 * CROSS_WIDTH), lambda l, i: (l, 0, 0)),
        ],
        out_specs=pl.BlockSpec((None, ROW_TILE, 2 * CROSS_WIDTH), lambda l, i: (l, i, 0)),
        out_shape=jax.ShapeDtypeStruct((DEPTH, n, 2 * CROSS_WIDTH), BF16),
        compiler_params=_params("parallel", "parallel"),
        name="mem_kv",
    )(mem, g, b, wkv)


def _cross_kernel(x_ref, kv_ref, wq_ref, wo_ref, g_ref, b_ref, o_ref):
    x = x_ref[...]
    q = _dot(x.astype(BF16), wq_ref[...]).astype(BF16)
    kv = kv_ref[...]
    scale = 1.0 / math.sqrt(CROSS_HEAD_DIM)
    outs = []

    def scores(h):
        sl = slice(h * CROSS_HEAD_DIM, (h + 1) * CROSS_HEAD_DIM)
        return _dot_nt(q[:, sl], kv[:, sl]) * scale

    s_next = scores(0)
    for h in range(CROSS_HEADS):
        s = s_next
        if h + 1 < CROSS_HEADS:
            s_next = scores(h + 1)
        p = jnp.exp(s - jnp.max(s, axis=-1, keepdims=True))
        p = p / jnp.sum(p, axis=-1, keepdims=True)
        outs.append(_dot(p.astype(BF16), kv[:, CROSS_WIDTH + h * CROSS_HEAD_DIM:CROSS_WIDTH + (h + 1) * CROSS_HEAD_DIM]))
    o = jnp.concatenate(outs, axis=1).astype(BF16)
    c = _dot(o, wo_ref[...])
    o_ref[...] = _layer_norm(DEEPNORM_ALPHA * x + c, g_ref[...], b_ref[...])


def _cross_attn(x, kv, wq, wo, g, b, batch, seq):
    x3 = x.reshape(batch, seq, D_MODEL)
    kv3 = kv.reshape(batch, MEM_LEN, 2 * CROSS_WIDTH)
    return pl.pallas_call(
        _cross_kernel,
        grid=(batch, seq // CROSS_TILE),
        in_specs=[
            pl.BlockSpec((None, CROSS_TILE, D_MODEL), lambda b, i: (b, i, 0)),
            pl.BlockSpec((None, MEM_LEN, 2 * CROSS_WIDTH), lambda b, i: (b, 0, 0)),
            _resident((D_MODEL, CROSS_WIDTH)),
            _resident((CROSS_WIDTH, D_MODEL)),
            _resident((1, D_MODEL)), _resident((1, D_MODEL)),
        ],
        out_specs=pl.BlockSpec((None, CROSS_TILE, D_MODEL), lambda b, i: (b, i, 0)),
        out_shape=jax.ShapeDtypeStruct((batch, seq, D_MODEL), F32),
        compiler_params=_params("parallel", "parallel"),
        name="cross_attn",
    )(x3, kv3, wq, wo, g, b).reshape(batch * seq, D_MODEL)


def _row(v):
    return v.reshape(1, -1).astype(F32)


def kernel(x, mem, ffn1_w_in, ffn1_w_out, ffn1_ln_g, ffn1_ln_b, mix_w_in, swa_sinks, swa_proj, s5_a_re, s5_a_im, s5_log_step, s5_b_re, s5_b_im, s5_c_re, s5_c_im, s5_d, s5_glu_w, s5_glu_b, s5_proj, conv_w, conv_b, conv_ln_g, conv_ln_b, conv_proj, diff_lq1, diff_lk1, diff_lq2, diff_lk2, diff_norm_g, diff_proj, mix_w_out, mix_ln_g, mix_ln_b, mem_ln_g, mem_ln_b, cross_wq, cross_wkv, cross_wo, cross_ln_g, cross_ln_b, ffn2_w_in, ffn2_w_out, ffn2_ln_g, ffn2_ln_b):
    batch, seq, _ = x.shape
    n = batch * seq
    assert seq % DIFF_TILE == 0 and seq % CONV_TILE == 0 and seq % ROW_TILE == 0 and seq % CROSS_TILE == 0
    assert seq % S5_TILE == 0 and seq % SWA_TILE == 0 and (batch * MEM_LEN) % ROW_TILE == 0
    h = x.reshape(n, D_MODEL)
    kv_all = _mem_kv(mem.reshape(batch * MEM_LEN, D_MODEL), _row(mem_ln_g), _row(mem_ln_b),
                     cross_wkv.astype(BF16))
    for l in range(DEPTH):
        h = _ffn_ln(h, ffn1_w_in[l].astype(BF16), ffn1_w_out[l].astype(BF16),
                    _row(ffn1_ln_g[l]), _row(ffn1_ln_b[l]))
        lambda_init = 0.8 - 0.6 * math.exp(-0.3 * l)
        w = mix_w_in[l]
        cols = lambda off, width: w[:, off:off + width]
        w_att = jnp.concatenate([cols(OFF_DQ, D_QK), cols(OFF_DK, D_QK), cols(OFF_AK, A_KV)], axis=1).astype(BF16)
        w_q = _swa_q_weight(cols(OFF_AQ, A_Q)).astype(BF16)
        w_v_t = jnp.concatenate([cols(OFF_DV, D_V), cols(OFF_AV, A_KV)], axis=1).T.astype(BF16)
        w_sc = jnp.concatenate([cols(OFF_CU, CONV_IN), cols(OFF_SU, S5_WIDTH)], axis=1).astype(BF16)
        w_gate = cols(OFF_GL, GATE_COLS).astype(BF16)
        h_att, q_swa, h_sc, v_t = _mix_proj(h, w_att, w_q, w_sc, w_v_t, batch, seq)
        o_a = _swa(h_att, q_swa, v_t, swa_sinks[l], batch, seq)
        tables = _s5_tables(s5_a_re[l], s5_a_im[l], s5_log_step[l], s5_b_re[l], s5_b_im[l], s5_c_re[l], s5_c_im[l])
        o_b = _s5(h_sc, tables, _row(s5_d[l]), s5_glu_w[l].astype(BF16), _row(s5_glu_b[l]), batch, seq)
        o_c = _conv_module(h_sc, conv_w[l], _row(conv_b[l]), _row(conv_ln_g[l]), _row(conv_ln_b[l]), batch, seq)
        lam = (jnp.exp(jnp.sum(diff_lq1[l] * diff_lk1[l])) - jnp.exp(jnp.sum(diff_lq2[l] * diff_lk2[l]))
               + lambda_init).astype(F32)
        o_d = _diff_attn(h_att, v_t, jnp.stack([lam, jnp.float32(1.0 - lambda_init)]), _row(diff_norm_g[l]), batch, seq)
        h = _merge(h, o_a, o_b, o_c, o_d, w_gate, swa_proj[l].astype(BF16), s5_proj[l].astype(BF16),
                   conv_proj[l].astype(BF16), diff_proj[l].astype(BF16), mix_w_out[l].astype(BF16),
                   _row(mix_ln_g[l]), _row(mix_ln_b[l]))
        h = _cross_attn(h, kv_all[l], cross_wq[l].astype(BF16), cross_wo[l].astype(BF16),
                        _row(cross_ln_g[l]), _row(cross_ln_b[l]), batch, seq)
        h = _ffn_ln(h, ffn2_w_in[l].astype(BF16), ffn2_w_out[l].astype(BF16),
                    _row(ffn2_ln_g[l]), _row(ffn2_ln_b[l]))
    return h.reshape(batch, seq, D_MODEL)
```

```python
import functools
import math

import jax
import jax.numpy as jnp
from jax import lax
from jax.experimental import pallas as pl
from jax.experimental.pallas import tpu as pltpu

F32 = jnp.float32
BF16 = jnp.bfloat16

D_MODEL = 1024
DEPTH = 4
MEM_LEN = 256
HEAD_DIM = 64
BLOCK = 128
SWA_HEADS = 8
SWA_KV_HEADS = 2
SWA_REP = SWA_HEADS // SWA_KV_HEADS
S5_WIDTH = 512
S5_GROUP = 16
S5_GROUPS = S5_WIDTH // S5_GROUP
S5_STATE = 64
S5_NSTATE = S5_GROUPS * S5_STATE
CONV_WIDTH = 512
CONV_K = 31
DIFF_HEADS = 4
DIFF_V_DIM = 2 * HEAD_DIM
CROSS_HEADS = 4
CROSS_HEAD_DIM = 128
CROSS_WIDTH = CROSS_HEADS * CROSS_HEAD_DIM
FFN_DIM = 2816
N_BRANCHES = 4

A_Q = SWA_HEADS * HEAD_DIM
A_KV = SWA_KV_HEADS * HEAD_DIM
D_QK = DIFF_HEADS * 2 * HEAD_DIM
D_V = DIFF_HEADS * DIFF_V_DIM
CONV_IN = 2 * CONV_WIDTH
GATE_COLS = N_BRANCHES * D_MODEL
OFF_AQ = 0
OFF_AK = OFF_AQ + A_Q
OFF_AV = OFF_AK + A_KV
OFF_DQ = OFF_AV + A_KV
OFF_DK = OFF_DQ + D_QK
OFF_DV = OFF_DK + D_QK
OFF_SU = OFF_DV + D_V
OFF_CU = OFF_SU + S5_WIDTH
OFF_GL = OFF_CU + CONV_IN
ATT_COLS = 2 * D_QK + A_KV
VT_ROWS = D_V + A_KV

DEEPNORM_ALPHA = (2.0 * DEPTH) ** 0.25
LN_EPS = 1e-5
NEG_INF = -1e30

LANES = 128
SUBLANES = 8
MXU_WIDTH = 256
VMEM_LIMIT = 56 * 1024 * 1024

ROW_TILE = 512
FFN_CHUNK = MXU_WIDTH
DIFF_TILE = 512
ONES_ROWS = 16
DIFF_VROWS = DIFF_V_DIM + ONES_ROWS
SWA_TILE = 512
LOG2E = math.log2(math.e)
CONV_HALO = 32
CONV_SHIFT_ROWS = ROW_TILE + CONV_HALO - SUBLANES
S5_TILE = 256
CROSS_TILE = 512


def _alibi_slope(h, n):
    return 2.0 ** (-8.0 * (h + 1) / n)


def _layer_norm(z, g, b):
    mu = jnp.mean(z, axis=-1, keepdims=True)
    zc = z - mu
    var = jnp.mean(zc * zc, axis=-1, keepdims=True)
    return zc * lax.rsqrt(var + LN_EPS) * g + b


def _dot(a, b):
    return jnp.dot(a, b, preferred_element_type=F32)


def _dot_nt(a, b):
    return lax.dot_general(a, b, (((1,), (1,)), ((), ())), preferred_element_type=F32)


def _params(*sem):
    return pltpu.CompilerParams(dimension_semantics=sem, vmem_limit_bytes=VMEM_LIMIT)


def _resident(shape):
    nd = len(shape)
    return pl.BlockSpec(shape, lambda *_: (0,) * nd, pipeline_mode=pl.Buffered(1))


def _ffn_ln_kernel(x_ref, wg_ref, wu_ref, wo_ref, g_ref, b_ref, o_ref, a_ref):
    half = ROW_TILE // 2
    for r in range(2):
        rows = slice(r * half, (r + 1) * half)
        x = x_ref[rows, :]
        xb = x.astype(BF16)
        for c in range(FFN_DIM // FFN_CHUNK):
            sl = slice(c * FFN_CHUNK, (c + 1) * FFN_CHUNK)
            gate = _dot(xb, wg_ref[:, sl])
            up = _dot(xb, wu_ref[:, sl])
            a_ref[rows, sl] = (gate * jax.nn.sigmoid(gate) * up).astype(BF16)
        f = _dot(a_ref[rows, :], wo_ref[...])
        o_ref[rows, :] = _layer_norm(DEEPNORM_ALPHA * x + 0.5 * f, g_ref[...], b_ref[...])


def _ffn_ln(x, w_in, w_out, g, b):
    n = x.shape[0]
    return pl.pallas_call(
        _ffn_ln_kernel,
        grid=(n // ROW_TILE,),
        in_specs=[
            pl.BlockSpec((ROW_TILE, D_MODEL), lambda i: (i, 0)),
            pl.BlockSpec((D_MODEL, FFN_DIM), lambda i: (0, 0), pipeline_mode=pl.Buffered(1)),
            pl.BlockSpec((D_MODEL, FFN_DIM), lambda i: (0, 1), pipeline_mode=pl.Buffered(1)),
            _resident((FFN_DIM, D_MODEL)),
            _resident((1, D_MODEL)),
            _resident((1, D_MODEL)),
        ],
        out_specs=pl.BlockSpec((ROW_TILE, D_MODEL), lambda i: (i, 0)),
        out_shape=jax.ShapeDtypeStruct((n, D_MODEL), F32),
        scratch_shapes=[pltpu.VMEM((ROW_TILE, FFN_DIM), BF16)],
        compiler_params=_params("parallel"),
        name="ffn_ln",
    )(x, w_in, w_in, w_out, g, b)


def _mix_proj_kernel(x_ref, wc_ref, wa_ref, wq_ref, ws_ref, wvt_ref, cw_ref, cb_ref, cg_ref, cbeta_ref,
                     ha_ref, q_ref, su_ref, vt_ref, oc_ref, buf_ref, sh_ref, *, per_seq):
    i = pl.program_id(0)

    @pl.when(i == 0)
    def _():
        buf_ref[ROW_TILE:, :] = jnp.zeros((CONV_HALO, CONV_WIDTH), F32)

    xb = x_ref[...].astype(BF16)
    cu = _dot(xb, wc_ref[...])
    ha_ref[...] = _dot(xb, wa_ref[...]).astype(BF16)
    q = _dot(xb, wq_ref[...])
    for h in range(SWA_HEADS):
        q_ref[h] = q[:, h * LANES:(h + 1) * LANES].astype(BF16)
    su_ref[...] = _dot(xb, ws_ref[...])
    vt_ref[...] = _dot_nt(wvt_ref[...], xb).astype(BF16)
    keep = (i % per_seq != 0).astype(F32)
    buf_ref[0:CONV_HALO, :] = buf_ref[ROW_TILE:, :] * keep
    buf_ref[CONV_HALO:, :] = cu[:, :CONV_WIDTH] * jax.nn.sigmoid(cu[:, CONV_WIDTH:])
    for b in range(1, SUBLANES):
        sh_ref[b - 1] = buf_ref[b:b + CONV_SHIFT_ROWS, :]
    first = CONV_HALO - (CONV_K - 1)
    acc = jnp.zeros((ROW_TILE, CONV_WIDTH), F32)
    for k in range(CONV_K):
        a, b = divmod(first + k, SUBLANES)
        lo = a * SUBLANES
        src = buf_ref[lo:lo + ROW_TILE, :] if b == 0 else sh_ref[b - 1, lo:lo + ROW_TILE, :]
        acc = acc + cw_ref[k:k + 1, :] * src
    y = _layer_norm(acc + cb_ref[...], cg_ref[...], cbeta_ref[...])
    oc_ref[...] = (y * jax.nn.sigmoid(y)).astype(BF16)


def _mix_proj(x, w_cu, w_att, w_q, w_su, w_v_t, conv_w, conv_b, conv_g, conv_beta, batch, seq):
    n = x.shape[0]
    per_seq = seq // ROW_TILE
    row = lambda w: pl.BlockSpec((ROW_TILE, w), lambda i: (i, 0))
    return pl.pallas_call(
        functools.partial(_mix_proj_kernel, per_seq=per_seq),
        grid=(n // ROW_TILE,),
        in_specs=[
            row(D_MODEL),
            _resident((D_MODEL, CONV_IN)),
            _resident((D_MODEL, ATT_COLS)),
            _resident((D_MODEL, SWA_HEADS * LANES)),
            _resident((D_MODEL, S5_WIDTH)),
            _resident((VT_ROWS, D_MODEL)),
            _resident((CONV_K, CONV_WIDTH)),
            _resident((1, CONV_WIDTH)), _resident((1, CONV_WIDTH)), _resident((1, CONV_WIDTH)),
        ],
        out_specs=[
            row(ATT_COLS),
            pl.BlockSpec((SWA_HEADS, ROW_TILE, LANES), lambda i: (0, i, 0)),
            row(S5_WIDTH),
            pl.BlockSpec((None, VT_ROWS, ROW_TILE), lambda i: (i // per_seq, 0, i % per_seq)),
            row(CONV_WIDTH),
        ],
        out_shape=[
            jax.ShapeDtypeStruct((n, ATT_COLS), BF16),
            jax.ShapeDtypeStruct((SWA_HEADS, n, LANES), BF16),
            jax.ShapeDtypeStruct((n, S5_WIDTH), F32),
            jax.ShapeDtypeStruct((batch, VT_ROWS, seq), BF16),
            jax.ShapeDtypeStruct((n, CONV_WIDTH), BF16),
        ],
        scratch_shapes=[pltpu.VMEM((ROW_TILE + CONV_HALO, CONV_WIDTH), F32),
                        pltpu.VMEM((SUBLANES - 1, CONV_SHIFT_ROWS, CONV_WIDTH), F32)],
        compiler_params=_params("arbitrary"),
        name="mix_proj_conv",
    )(x, w_cu, w_att, w_q, w_su, w_v_t, conv_w, conv_b, conv_g, conv_beta)


def _swa_q_weight(w_aq):
    d = w_aq.shape[0]
    w = (w_aq * (LOG2E / math.sqrt(HEAD_DIM))).reshape(d, SWA_KV_HEADS, SWA_REP, 1, HEAD_DIM)
    sel = jnp.eye(SWA_KV_HEADS, dtype=w.dtype).reshape(1, SWA_KV_HEADS, 1, SWA_KV_HEADS, 1)
    return (w * sel).reshape(d, SWA_HEADS * LANES)


def _swa_kernel(sink_ref, bias_ref, q_ref, kc_ref, kp_ref, vc_ref, vp_ref, o_ref):
    i = pl.program_id(1)
    gw = SWA_REP * BLOCK
    k_all = jnp.concatenate([kp_ref[...], kc_ref[...]], axis=0)
    v_all = jnp.concatenate([vp_ref[...], vc_ref[...]], axis=1)
    vx_all = jnp.concatenate([v_all, jnp.ones((ONES_ROWS, v_all.shape[1]), BF16)], axis=0)
    no_prev = jnp.where(i == 0, NEG_INF, 0.0)

    def scores(blk, g):
        lo = blk * BLOCK
        qg = q_ref[g * SWA_REP:(g + 1) * SWA_REP, lo:lo + BLOCK, :].reshape(gw, LANES)
        s = _dot_nt(k_all[lo:lo + 2 * BLOCK], qg) + bias_ref[g]
        if blk == 0:
            s = jnp.concatenate([s[:BLOCK] + no_prev, s[BLOCK:]], axis=0)
        return s

    units = [(blk, g) for blk in range(SWA_TILE // BLOCK) for g in range(SWA_KV_HEADS)]
    s_next = scores(*units[0])
    outs = []
    for u, (blk, g) in enumerate(units):
        s = s_next
        if u + 1 < len(units):
            s_next = scores(*units[u + 1])
        lo = blk * BLOCK
        sink = sink_ref[g]
        m = jnp.maximum(jnp.max(s, axis=0, keepdims=True), sink)
        e = jnp.exp2(s - m).astype(BF16)
        pv = _dot(vx_all[:, lo:lo + 2 * BLOCK], e)
        den = pv[LANES:LANES + 1] + jnp.exp2(sink - m)
        og = pv[g * HEAD_DIM:(g + 1) * HEAD_DIM] * (1.0 / den)
        outs += [og[:, r * BLOCK:(r + 1) * BLOCK] for r in range(SWA_REP)]
        if g == SWA_KV_HEADS - 1:
            o_ref[lo:lo + BLOCK, :] = jnp.concatenate(outs, axis=0).T.astype(BF16)
            outs = []


def _swa_bias():
    kj = jnp.arange(2 * BLOCK)[:, None]
    qi = jnp.arange(BLOCK)[None, :]
    dist = BLOCK + qi - kj
    valid = (dist >= 0) & (dist < BLOCK)
    slopes = jnp.asarray([_alibi_slope(h, SWA_HEADS) * LOG2E for h in range(SWA_HEADS)], F32)
    b = jnp.where(valid[None], -slopes[:, None, None] * dist[None].astype(F32), NEG_INF)
    b = b.reshape(SWA_KV_HEADS, SWA_REP, 2 * BLOCK, BLOCK)
    return jnp.transpose(b, (0, 2, 1, 3)).reshape(SWA_KV_HEADS, 2 * BLOCK, SWA_REP * BLOCK)


def _swa(h_att, q_swa, v_t, sinks, batch, seq):
    h3 = h_att.reshape(batch, seq, ATT_COLS)
    ck = (2 * D_QK) // A_KV
    rv = D_V // A_KV
    per_seq = seq // SWA_TILE
    r = SWA_TILE // BLOCK
    gw = SWA_REP * BLOCK
    sink_rows = jnp.repeat(sinks.astype(F32).reshape(SWA_KV_HEADS, SWA_REP) * LOG2E, BLOCK, axis=1)
    prev = lambda i: jnp.maximum(i * r - 1, 0)
    return pl.pallas_call(
        _swa_kernel,
        grid=(batch, per_seq),
        in_specs=[
            _resident((SWA_KV_HEADS, 1, gw)),
            _resident((SWA_KV_HEADS, 2 * BLOCK, gw)),
            pl.BlockSpec((SWA_HEADS, SWA_TILE, LANES), lambda b, i: (0, b * per_seq + i, 0)),
            pl.BlockSpec((None, SWA_TILE, A_KV), lambda b, i: (b, i, ck)),
            pl.BlockSpec((None, BLOCK, A_KV), lambda b, i: (b, prev(i), ck)),
            pl.BlockSpec((None, A_KV, SWA_TILE), lambda b, i: (b, rv, i)),
            pl.BlockSpec((None, A_KV, BLOCK), lambda b, i: (b, rv, prev(i))),
        ],
        out_specs=pl.BlockSpec((None, SWA_TILE, A_Q), lambda b, i: (b, i, 0)),
        out_shape=jax.ShapeDtypeStruct((batch, seq, A_Q), BF16),
        compiler_params=_params("parallel", "parallel"),
        name="swa",
    )(sink_rows.reshape(SWA_KV_HEADS, 1, gw), _swa_bias(), q_swa, h3, h3, v_t, v_t).reshape(batch * seq, A_Q)


def _diff_kernel(sc_ref, q_ref, k_ref, vt_ref, g_ref, o_ref,
                 qs_ref, vx_ref, b_ref, s_ref, mx_ref, m_ref, acc_ref, *, n_tiles):
    h = pl.program_id(1)
    t = DIFF_TILE
    v_dim = DIFF_V_DIM
    slope = jnp.float32(_alibi_slope(DIFF_HEADS - 1, DIFF_HEADS) * LOG2E)
    for hh in range(DIFF_HEADS - 1):
        slope = jnp.where(h == hh, jnp.float32(_alibi_slope(hh, DIFF_HEADS) * LOG2E), slope)

    for j in range(n_tiles):
        vx_ref[j, :v_dim, :] = vt_ref[:, j * t:(j + 1) * t]
        vx_ref[j, v_dim:, :] = jnp.ones((DIFF_VROWS - v_dim, t), BF16)
    krow = lax.broadcasted_iota(jnp.int32, (t, t), 0)
    qcol = lax.broadcasted_iota(jnp.int32, (t, t), 1)
    b_ref[0] = slope * krow.astype(F32)
    b_ref[1] = jnp.where(qcol >= krow, slope * krow.astype(F32), NEG_INF)

    def scores(j, c, qt):
        k0 = pl.multiple_of(j * t, t)
        s = _dot_nt(k_ref[pl.ds(k0, t), :], qs_ref[c]) + b_ref[jnp.where(j == qt, 1, 0)]
        s_ref[c] = s
        mx_ref[c] = jnp.max(s, axis=0, keepdims=True)

    def accumulate(j, c):
        mx = mx_ref[c]
        e = jnp.exp2(s_ref[c] - mx).astype(BF16)
        pv = _dot(vx_ref[j], e)
        mxo = mx + slope * jnp.asarray(j * t, F32)
        m_old = m_ref[c]
        m_new = jnp.maximum(m_old, mxo)
        m_ref[c] = m_new
        acc_ref[c] = jnp.exp2(m_old - m_new) * acc_ref[c] + jnp.exp2(mxo - m_new) * pv

    def q_tile(qt, carry):
        qt = jnp.asarray(qt, jnp.int32)
        q0 = pl.multiple_of(qt * t, t)
        lane = lax.broadcasted_iota(jnp.int32, (t, 2 * HEAD_DIM), 1)
        q = q_ref[pl.ds(q0, t), :].astype(F32) * (LOG2E / math.sqrt(HEAD_DIM))
        qs_ref[0] = jnp.where(lane < HEAD_DIM, q, 0.0).astype(BF16)
        qs_ref[1] = jnp.where(lane >= HEAD_DIM, q, 0.0).astype(BF16)
        m_ref[...] = jnp.full(m_ref.shape, NEG_INF, F32)
        acc_ref[...] = jnp.zeros(acc_ref.shape, F32)
        scores(jnp.int32(0), 0, qt)

        def body(kt, carry):
            scores(kt, 1, qt)
            accumulate(kt, 0)
            scores(kt + 1, 0, qt)
            accumulate(kt, 1)
            return carry

        lax.fori_loop(0, qt, body, 0)
        scores(qt, 1, qt)
        accumulate(qt, 0)
        accumulate(qt, 1)

        def normalised(c):
            a = acc_ref[c]
            return a[:v_dim] * (1.0 / a[v_dim:v_dim + 1])

        o = (normalised(0) - sc_ref[0] * normalised(1)).T
        o = o * lax.rsqrt(jnp.mean(o * o, axis=-1, keepdims=True) + LN_EPS) * g_ref[...] * sc_ref[1]
        o_ref[pl.ds(q0, t), :] = o.astype(BF16)
        return carry

    lax.fori_loop(0, n_tiles, q_tile, 0)


def _diff_attn(h_att, v_t, scalars, norm_g, batch, seq):
    h3 = h_att.reshape(batch, seq, ATT_COLS)
    w = 2 * HEAD_DIM
    t = DIFF_TILE
    n_tiles = seq // t
    return pl.pallas_call(
        functools.partial(_diff_kernel, n_tiles=n_tiles),
        grid=(batch, DIFF_HEADS),
        in_specs=[
            pl.BlockSpec(memory_space=pltpu.SMEM),
            pl.BlockSpec((None, seq, w), lambda b, h: (b, 0, h)),
            pl.BlockSpec((None, seq, w), lambda b, h: (b, 0, DIFF_HEADS + h)),
            pl.BlockSpec((None, DIFF_V_DIM, seq), lambda b, h: (b, h, 0)),
            _resident((1, DIFF_V_DIM)),
        ],
        out_specs=pl.BlockSpec((None, seq, w), lambda b, h: (b, 0, h)),
        out_shape=jax.ShapeDtypeStruct((batch, seq, D_V), BF16),
        scratch_shapes=[
            pltpu.VMEM((2, t, w), BF16),
            pltpu.VMEM((n_tiles, DIFF_VROWS, t), BF16),
            pltpu.VMEM((2, t, t), F32),
            pltpu.VMEM((2, t, t), F32),
            pltpu.VMEM((2, 1, t), F32),
            pltpu.VMEM((2, 1, t), F32),
            pltpu.VMEM((2, DIFF_VROWS, t), F32),
        ],
        compiler_params=_params("parallel", "parallel"),
        name="diff_attn",
    )(scalars, h3, h3, v_t, norm_g).reshape(batch * seq, D_V)


def _s5_kernel(u_ref, bw_ref, cw_ref, air_ref, aii_ref, apr_ref, api_ref, ah_ref, tri_ref,
               d_ref, gw_ref, gb_ref, o_ref, hr_ref, hi_ref):
    i = pl.program_id(1)

    @pl.when(i == 0)
    def _():
        hr_ref[...] = jnp.zeros_like(hr_ref)
        hi_ref[...] = jnp.zeros_like(hi_ref)

    u = u_ref[...]
    ub = u.astype(BF16)
    tri = tri_ref[...]
    nq = S5_WIDTH // LANES
    sw = S5_NSTATE // nq

    def in_proj(j):
        return _dot(ub[:, j * LANES:(j + 1) * LANES], bw_ref[j])

    bu_next = in_proj(0)
    xc_prev = None
    ys = []
    for j in range(nq):
        sl = slice(j * sw, (j + 1) * sw)
        bu = bu_next
        if j + 1 < nq:
            bu_next = in_proj(j + 1)
        bur, bui = bu[:, :sw], bu[:, sw:]
        air, aii = air_ref[:, sl], aii_ref[:, sl]
        zr = (air * bur - aii * bui).astype(BF16)
        zi = (air * bui + aii * bur).astype(BF16)
        cr = _dot(tri, zr) + hr_ref[:, sl]
        ci = _dot(tri, zi) + hi_ref[:, sl]
        if xc_prev is not None:
            ys.append(_dot(xc_prev, cw_ref[j - 1]))
        apr, api = apr_ref[:, sl], api_ref[:, sl]
        xr = apr * cr - api * ci
        xi = apr * ci + api * cr
        lr, li = xr[S5_TILE - 1:, :], xi[S5_TILE - 1:, :]
        ahr, ahi = ah_ref[0:1, sl], ah_ref[1:2, sl]
        hr_ref[:, sl] = ahr * lr - ahi * li
        hi_ref[:, sl] = ahr * li + ahi * lr
        xc_prev = jnp.concatenate([xr, xi], axis=1).astype(BF16)
    ys.append(_dot(xc_prev, cw_ref[nq - 1]))
    y = jnp.concatenate(ys, axis=1) + d_ref[...] * u
    y = jax.nn.gelu(y, approximate=True)
    gate = _dot(y.astype(BF16), gw_ref[...]) + gb_ref[...]
    o_ref[...] = (y * jax.nn.sigmoid(gate)).astype(BF16)


def _s5_tables(a_re, a_im, log_step, b_re, b_im, c_re, c_im):
    g, p, c = S5_GROUPS, S5_STATE, S5_GROUP
    step = jnp.exp(log_step)[:, None]
    mag = jnp.exp(a_re * step)
    abar_r, abar_i = mag * jnp.cos(a_im * step), mag * jnp.sin(a_im * step)
    den = a_re * a_re + a_im * a_im
    nr, ni = abar_r - 1.0, abar_i
    coef_r = (nr * a_re + ni * a_im) / den
    coef_i = (ni * a_re - nr * a_im) / den
    bbar_r = coef_r[..., None] * b_re - coef_i[..., None] * b_im
    bbar_i = coef_r[..., None] * b_im + coef_i[..., None] * b_re
    nq = S5_WIDTH // LANES
    gq = g // nq
    eye = jnp.eye(gq, dtype=F32)

    def in_slab(bb):
        bb = bb.reshape(nq, gq, p, c)
        return jnp.einsum('qgpc,gh->qgchp', bb, eye).reshape(nq, gq * c, gq * p)

    bw = jnp.concatenate([in_slab(bbar_r), in_slab(bbar_i)], axis=2).astype(BF16)

    def out_slab(cc):
        cc = cc.reshape(nq, gq, c, p)
        return jnp.einsum('qgcp,gh->qgphc', cc, eye).reshape(nq, gq * p, gq * c)

    cw = jnp.concatenate([out_slab(c_re), -out_slab(c_im)], axis=1).astype(BF16)
    half = S5_TILE // 2
    tt = jnp.arange(1 - half, S5_TILE + 1 - half, dtype=F32)[:, None]
    la = (a_re * step).reshape(1, g * p)
    th = (a_im * step).reshape(1, g * p)
    pm, ang = jnp.exp(tt * la), tt * th
    im_ = jnp.exp(-tt * la)
    apr, api = pm * jnp.cos(ang), pm * jnp.sin(ang)
    air, aii = im_ * jnp.cos(ang), -im_ * jnp.sin(ang)
    hm = jnp.exp(half * la)
    ah = jnp.concatenate([hm * jnp.cos(half * th), hm * jnp.sin(half * th)], axis=0)
    return bw, cw, air, aii, apr, api, ah


def _s5(su, tables, d_skip, glu_w, glu_b, batch, seq):
    bw, cw, air, aii, apr, api, ah = tables
    h3 = su.reshape(batch, seq, S5_WIDTH)
    tri = jnp.tril(jnp.ones((S5_TILE, S5_TILE), F32)).astype(BF16)
    tab = (S5_TILE, S5_NSTATE)
    return pl.pallas_call(
        _s5_kernel,
        grid=(batch, seq // S5_TILE),
        in_specs=[
            pl.BlockSpec((None, S5_TILE, S5_WIDTH), lambda b, i: (b, i, 0)),
            _resident(bw.shape), _resident(cw.shape),
            _resident(tab), _resident(tab), _resident(tab), _resident(tab),
            _resident((2, S5_NSTATE)),
            _resident((S5_TILE, S5_TILE)),
            _resident((1, S5_WIDTH)),
            _resident((S5_WIDTH, S5_WIDTH)),
            _resident((1, S5_WIDTH)),
        ],
        out_specs=pl.BlockSpec((None, S5_TILE, S5_WIDTH), lambda b, i: (b, i, 0)),
        out_shape=jax.ShapeDtypeStruct((batch, seq, S5_WIDTH), BF16),
        scratch_shapes=[pltpu.VMEM((1, S5_NSTATE), F32), pltpu.VMEM((1, S5_NSTATE), F32)],
        compiler_params=_params("parallel", "arbitrary"),
        name="s5_ssm",
    )(h3, bw, cw, air, aii, apr, api, ah, tri, d_skip, glu_w, glu_b).reshape(batch * seq, S5_WIDTH)


def _merge_kernel(x_ref, oa_ref, ob_ref, oc_ref, od_ref, wg_ref, pa_ref, pb_ref, pc_ref, pd_ref,
                  wo_ref, g_ref, b_ref, o_ref):
    half = ROW_TILE // 2
    for r in range(2):
        rows = slice(r * half, (r + 1) * half)
        x = x_ref[rows, :]
        xb = x.astype(BF16)
        merged = jnp.zeros((half, D_MODEL), F32)
        for i, (br_ref, pr_ref) in enumerate(((oa_ref, pa_ref), (ob_ref, pb_ref), (oc_ref, pc_ref), (od_ref, pd_ref))):
            gl = _dot(xb, wg_ref[:, i * D_MODEL:(i + 1) * D_MODEL])
            merged = merged + jax.nn.sigmoid(gl) * _dot(br_ref[rows, :], pr_ref[...])
        m = _dot(merged.astype(BF16), wo_ref[...])
        o_ref[rows, :] = _layer_norm(DEEPNORM_ALPHA * x + m, g_ref[...], b_ref[...])


def _merge(x, oa, ob, oc, od, w_gate, pa, pb, pc, pd, w_out, g, b):
    n = x.shape[0]
    row = lambda w: pl.BlockSpec((ROW_TILE, w), lambda i: (i, 0))
    return pl.pallas_call(
        _merge_kernel,
        grid=(n // ROW_TILE,),
        in_specs=[
            row(D_MODEL), row(A_Q), row(S5_WIDTH), row(CONV_WIDTH), row(D_V),
            _resident((D_MODEL, GATE_COLS)),
            _resident((A_Q, D_MODEL)), _resident((S5_WIDTH, D_MODEL)),
            _resident((CONV_WIDTH, D_MODEL)), _resident((D_V, D_MODEL)),
            _resident((D_MODEL, D_MODEL)),
            _resident((1, D_MODEL)), _resident((1, D_MODEL)),
        ],
        out_specs=row(D_MODEL),
        out_shape=jax.ShapeDtypeStruct((n, D_MODEL), F32),
        compiler_params=_params("parallel"),
        name="gated_merge",
    )(x, oa, ob, oc, od, w_gate, pa, pb, pc, pd, w_out, g, b)


def _mem_kv_kernel(m_ref, g_ref, b_ref, w_ref, o_ref):
    mn = _layer_norm(m_ref[...], g_ref[...], b_ref[...])
    o_ref[...] = _dot(mn.astype(BF16), w_ref[...]).astype(BF16)


def _mem_kv(mem, g, b, wkv):
    n = mem.shape[0]
    return pl.pallas_call(
        _mem_kv_kernel,
        grid=(DEPTH, n // ROW_TILE),
        in_specs=[
            pl.BlockSpec((ROW_TILE, D_MODEL), lambda l, i: (i, 0)),
            _resident((1, D_MODEL)), _resident((1, D_MODEL)),
            pl.BlockSpec((None, D_MODEL, 2 * CROSS_WIDTH), lambda l, i: (l, 0, 0)),
        ],
        out_specs=pl.BlockSpec((None, ROW_TILE, 2 * CROSS_WIDTH), lambda l, i: (l, i, 0)),
        out_shape=jax.ShapeDtypeStruct((DEPTH, n, 2 * CROSS_WIDTH), BF16),
        compiler_params=_params("parallel", "parallel"),
        name="mem_kv",
    )(mem, g, b, wkv)


def _cross_kernel(x_ref, kv_ref, wq_ref, wo_ref, g_ref, b_ref, o_ref):
    x = x_ref[...]
    q = _dot(x.astype(BF16), wq_ref[...]).astype(BF16)
    kv = kv_ref[...]
    scale = 1.0 / math.sqrt(CROSS_HEAD_DIM)
    outs = []

    def scores(h):
        sl = slice(h * CROSS_HEAD_DIM, (h + 1) * CROSS_HEAD_DIM)
        return _dot_nt(q[:, sl], kv[:, sl]) * scale

    s_next = scores(0)
    for h in range(CROSS_HEADS):
        s = s_next
        if h + 1 < CROSS_HEADS:
            s_next = scores(h + 1)
        p = jnp.exp(s - jnp.max(s, axis=-1, keepdims=True))
        p = p / jnp.sum(p, axis=-1, keepdims=True)
        outs.append(_dot(p.astype(BF16), kv[:, CROSS_WIDTH + h * CROSS_HEAD_DIM:CROSS_WIDTH + (h + 1) * CROSS_HEAD_DIM]))
    o = jnp.concatenate(outs, axis=1).astype(BF16)
    c = _dot(o, wo_ref[...])
    o_ref[...] = _layer_norm(DEEPNORM_ALPHA * x + c, g_ref[...], b_ref[...])


def _cross_attn(x, kv, wq, wo, g, b, batch, seq):
    x3 = x.reshape(batch, seq, D_MODEL)
    kv3 = kv.reshape(batch, MEM_LEN, 2 * CROSS_WIDTH)
    return pl.pallas_call(
        _cross_kernel,
        grid=(batch, seq // CROSS_TILE),
        in_specs=[
            pl.BlockSpec((None, CROSS_TILE, D_MODEL), lambda b, i: (b, i, 0)),
            pl.BlockSpec((None, MEM_LEN, 2 * CROSS_WIDTH), lambda b, i: (b, 0, 0)),
            _resident((D_MODEL, CROSS_WIDTH)),
            _resident((CROSS_WIDTH, D_MODEL)),
            _resident((1, D_MODEL)), _resident((1, D_MODEL)),
        ],
        out_specs=pl.BlockSpec((None, CROSS_TILE, D_MODEL), lambda b, i: (b, i, 0)),
        out_shape=jax.ShapeDtypeStruct((batch, seq, D_MODEL), F32),
        compiler_params=_params("parallel", "parallel"),
        name="cross_attn",
    )(x3, kv3, wq, wo, g, b).reshape(batch * seq, D_MODEL)


def _row(v):
    return v.reshape(1, -1).astype(F32)


def kernel(x, mem, ffn1_w_in, ffn1_w_out, ffn1_ln_g, ffn1_ln_b, mix_w_in, swa_sinks, swa_proj, s5_a_re, s5_a_im, s5_log_step, s5_b_re, s5_b_im, s5_c_re, s5_c_im, s5_d, s5_glu_w, s5_glu_b, s5_proj, conv_w, conv_b, conv_ln_g, conv_ln_b, conv_proj, diff_lq1, diff_lk1, diff_lq2, diff_lk2, diff_norm_g, diff_proj, mix_w_out, mix_ln_g, mix_ln_b, mem_ln_g, mem_ln_b, cross_wq, cross_wkv, cross_wo, cross_ln_g, cross_ln_b, ffn2_w_in, ffn2_w_out, ffn2_ln_g, ffn2_ln_b):
    batch, seq, _ = x.shape
    n = batch * seq
    assert seq % DIFF_TILE == 0 and seq % ROW_TILE == 0 and seq % CROSS_TILE == 0
    assert seq % S5_TILE == 0 and seq % SWA_TILE == 0 and (batch * MEM_LEN) % ROW_TILE == 0
    h = x.reshape(n, D_MODEL)
    kv_all = _mem_kv(mem.reshape(batch * MEM_LEN, D_MODEL), _row(mem_ln_g), _row(mem_ln_b),
                     cross_wkv.astype(BF16))
    for l in range(DEPTH):
        h = _ffn_ln(h, ffn1_w_in[l].astype(BF16), ffn1_w_out[l].astype(BF16),
                    _row(ffn1_ln_g[l]), _row(ffn1_ln_b[l]))
        lambda_init = 0.8 - 0.6 * math.exp(-0.3 * l)
        w = mix_w_in[l]
        cols = lambda off, width: w[:, off:off + width]
        w_att = jnp.concatenate([cols(OFF_DQ, D_QK), cols(OFF_DK, D_QK), cols(OFF_AK, A_KV)], axis=1).astype(BF16)
        w_q = _swa_q_weight(cols(OFF_AQ, A_Q)).astype(BF16)
        w_v_t = jnp.concatenate([cols(OFF_DV, D_V), cols(OFF_AV, A_KV)], axis=1).T.astype(BF16)
        w_gate = cols(OFF_GL, GATE_COLS).astype(BF16)
        h_att, q_swa, su, v_t, o_c = _mix_proj(
            h, cols(OFF_CU, CONV_IN).astype(BF16), w_att, w_q, cols(OFF_SU, S5_WIDTH).astype(BF16), w_v_t,
            conv_w[l], _row(conv_b[l]), _row(conv_ln_g[l]), _row(conv_ln_b[l]), batch, seq)
        o_a = _swa(h_att, q_swa, v_t, swa_sinks[l], batch, seq)
        tables = _s5_tables(s5_a_re[l], s5_a_im[l], s5_log_step[l], s5_b_re[l], s5_b_im[l], s5_c_re[l], s5_c_im[l])
        o_b = _s5(su, tables, _row(s5_d[l]), s5_glu_w[l].astype(BF16), _row(s5_glu_b[l]), batch, seq)
        lam = (jnp.exp(jnp.sum(diff_lq1[l] * diff_lk1[l])) - jnp.exp(jnp.sum(diff_lq2[l] * diff_lk2[l]))
               + lambda_init).astype(F32)
        o_d = _diff_attn(h_att, v_t, jnp.stack([lam, jnp.float32(1.0 - lambda_init)]), _row(diff_norm_g[l]), batch, seq)
        h = _merge(h, o_a, o_b, o_c, o_d, w_gate, swa_proj[l].astype(BF16), s5_proj[l].astype(BF16),
                   conv_proj[l].astype(BF16), diff_proj[l].astype(BF16), mix_w_out[l].astype(BF16),
                   _row(mix_ln_g[l]), _row(mix_ln_b[l]))
        h = _cross_attn(h, kv_all[l], cross_wq[l].astype(BF16), cross_wo[l].astype(BF16),
                        _row(cross_ln_g[l]), _row(cross_ln_b[l]), batch, seq)
        h = _ffn_ln(h, ffn2_w_in[l].astype(BF16), ffn2_w_out[l].astype(BF16),
                    _row(ffn2_ln_g[l]), _row(ffn2_ln_b[l]))
    return h.reshape(batch, seq, D_MODEL)
```

```python
import functools
import math

import jax
import jax.numpy as jnp
from jax import lax
from jax.experimental import pallas as pl
from jax.experimental.pallas import tpu as pltpu

F32 = jnp.float32
BF16 = jnp.bfloat16

D_MODEL = 1024
DEPTH = 4
MEM_LEN = 256
HEAD_DIM = 64
BLOCK = 128
SWA_HEADS = 8
SWA_KV_HEADS = 2
SWA_REP = SWA_HEADS // SWA_KV_HEADS
S5_WIDTH = 512
S5_GROUP = 16
S5_GROUPS = S5_WIDTH // S5_GROUP
S5_STATE = 64
S5_NSTATE = S5_GROUPS * S5_STATE
CONV_WIDTH = 512
CONV_K = 31
DIFF_HEADS = 4
DIFF_V_DIM = 2 * HEAD_DIM
CROSS_HEADS = 4
CROSS_HEAD_DIM = 128
CROSS_WIDTH = CROSS_HEADS * CROSS_HEAD_DIM
FFN_DIM = 2816
N_BRANCHES = 4

A_Q = SWA_HEADS * HEAD_DIM
A_KV = SWA_KV_HEADS * HEAD_DIM
D_QK = DIFF_HEADS * 2 * HEAD_DIM
D_V = DIFF_HEADS * DIFF_V_DIM
CONV_IN = 2 * CONV_WIDTH
GATE_COLS = N_BRANCHES * D_MODEL
OFF_AQ = 0
OFF_AK = OFF_AQ + A_Q
OFF_AV = OFF_AK + A_KV
OFF_DQ = OFF_AV + A_KV
OFF_DK = OFF_DQ + D_QK
OFF_DV = OFF_DK + D_QK
OFF_SU = OFF_DV + D_V
OFF_CU = OFF_SU + S5_WIDTH
OFF_GL = OFF_CU + CONV_IN
ATT_COLS = 2 * D_QK + A_KV
VT_ROWS = D_V + A_KV

DEEPNORM_ALPHA = (2.0 * DEPTH) ** 0.25
LN_EPS = 1e-5
NEG_INF = -1e30

LANES = 128
SUBLANES = 8
MXU_WIDTH = 256
VMEM_LIMIT = 56 * 1024 * 1024

ROW_TILE = 512
FFN_CHUNK = MXU_WIDTH
DIFF_TILE = 512
ONES_ROWS = 16
DIFF_VROWS = DIFF_V_DIM + ONES_ROWS
SWA_TILE = 512
LOG2E = math.log2(math.e)
CONV_HALO = 32
CONV_SHIFT_ROWS = ROW_TILE + CONV_HALO - SUBLANES
S5_TILE = 256
S5_STEP = 512
CROSS_TILE = 512


def _alibi_slope(h, n):
    return 2.0 ** (-8.0 * (h + 1) / n)


def _layer_norm(z, g, b):
    mu = jnp.mean(z, axis=-1, keepdims=True)
    zc = z - mu
    var = jnp.mean(zc * zc, axis=-1, keepdims=True)
    return zc * lax.rsqrt(var + LN_EPS) * g + b


def _dot(a, b):
    return jnp.dot(a, b, preferred_element_type=F32)


def _dot_nt(a, b):
    return lax.dot_general(a, b, (((1,), (1,)), ((), ())), preferred_element_type=F32)


def _params(*sem):
    return pltpu.CompilerParams(dimension_semantics=sem, vmem_limit_bytes=VMEM_LIMIT)


def _resident(shape):
    nd = len(shape)
    return pl.BlockSpec(shape, lambda *_: (0,) * nd, pipeline_mode=pl.Buffered(1))


def _ffn_ln_kernel(x_ref, wg_ref, wu_ref, wo_ref, g_ref, b_ref, o_ref, a_ref):
    half = ROW_TILE // 2
    for r in range(2):
        rows = slice(r * half, (r + 1) * half)
        x = x_ref[rows, :]
        xb = x.astype(BF16)
        for c in range(FFN_DIM // FFN_CHUNK):
            sl = slice(c * FFN_CHUNK, (c + 1) * FFN_CHUNK)
            gate = _dot(xb, wg_ref[:, sl])
            up = _dot(xb, wu_ref[:, sl])
            a_ref[rows, sl] = (gate * jax.nn.sigmoid(gate) * up).astype(BF16)
        f = _dot(a_ref[rows, :], wo_ref[...])
        o_ref[rows, :] = _layer_norm(DEEPNORM_ALPHA * x + 0.5 * f, g_ref[...], b_ref[...])


def _ffn_ln(x, w_in, w_out, g, b):
    n = x.shape[0]
    return pl.pallas_call(
        _ffn_ln_kernel,
        grid=(n // ROW_TILE,),
        in_specs=[
            pl.BlockSpec((ROW_TILE, D_MODEL), lambda i: (i, 0)),
            pl.BlockSpec((D_MODEL, FFN_DIM), lambda i: (0, 0), pipeline_mode=pl.Buffered(1)),
            pl.BlockSpec((D_MODEL, FFN_DIM), lambda i: (0, 1), pipeline_mode=pl.Buffered(1)),
            _resident((FFN_DIM, D_MODEL)),
            _resident((1, D_MODEL)),
            _resident((1, D_MODEL)),
        ],
        out_specs=pl.BlockSpec((ROW_TILE, D_MODEL), lambda i: (i, 0)),
        out_shape=jax.ShapeDtypeStruct((n, D_MODEL), F32),
        scratch_shapes=[pltpu.VMEM((ROW_TILE, FFN_DIM), BF16)],
        compiler_params=_params("parallel"),
        name="ffn_ln",
    )(x, w_in, w_in, w_out, g, b)


def _mix_proj_kernel(x_ref, wc_ref, wa_ref, wq_ref, ws_ref, wvt_ref, cw_ref, cb_ref, cg_ref, cbeta_ref,
                     ha_ref, q_ref, su_ref, vt_ref, oc_ref, buf_ref, sh_ref, *, per_seq):
    i = pl.program_id(0)

    @pl.when(i == 0)
    def _():
        buf_ref[ROW_TILE:, :] = jnp.zeros((CONV_HALO, CONV_WIDTH), F32)

    xb = x_ref[...].astype(BF16)
    cu = _dot(xb, wc_ref[...])
    ha_ref[...] = _dot(xb, wa_ref[...]).astype(BF16)
    q = _dot(xb, wq_ref[...])
    for h in range(SWA_HEADS):
        q_ref[h] = q[:, h * LANES:(h + 1) * LANES].astype(BF16)
    su_ref[...] = _dot(xb, ws_ref[...])
    vt_ref[...] = _dot_nt(wvt_ref[...], xb).astype(BF16)
    keep = (i % per_seq != 0).astype(F32)
    buf_ref[0:CONV_HALO, :] = buf_ref[ROW_TILE:, :] * keep
    buf_ref[CONV_HALO:, :] = cu[:, :CONV_WIDTH] * jax.nn.sigmoid(cu[:, CONV_WIDTH:])
    for b in range(1, SUBLANES):
        sh_ref[b - 1] = buf_ref[b:b + CONV_SHIFT_ROWS, :]
    first = CONV_HALO - (CONV_K - 1)
    acc = jnp.zeros((ROW_TILE, CONV_WIDTH), F32)
    for k in range(CONV_K):
        a, b = divmod(first + k, SUBLANES)
        lo = a * SUBLANES
        src = buf_ref[lo:lo + ROW_TILE, :] if b == 0 else sh_ref[b - 1, lo:lo + ROW_TILE, :]
        acc = acc + cw_ref[k:k + 1, :] * src
    y = _layer_norm(acc + cb_ref[...], cg_ref[...], cbeta_ref[...])
    oc_ref[...] = (y * jax.nn.sigmoid(y)).astype(BF16)


def _mix_proj(x, w_cu, w_att, w_q, w_su, w_v_t, conv_w, conv_b, conv_g, conv_beta, batch, seq):
    n = x.shape[0]
    per_seq = seq // ROW_TILE
    row = lambda w: pl.BlockSpec((ROW_TILE, w), lambda i: (i, 0))
    return pl.pallas_call(
        functools.partial(_mix_proj_kernel, per_seq=per_seq),
        grid=(n // ROW_TILE,),
        in_specs=[
            row(D_MODEL),
            _resident((D_MODEL, CONV_IN)),
            _resident((D_MODEL, ATT_COLS)),
            _resident((D_MODEL, SWA_HEADS * LANES)),
            _resident((D_MODEL, S5_WIDTH)),
            _resident((VT_ROWS, D_MODEL)),
            _resident((CONV_K, CONV_WIDTH)),
            _resident((1, CONV_WIDTH)), _resident((1, CONV_WIDTH)), _resident((1, CONV_WIDTH)),
        ],
        out_specs=[
            row(ATT_COLS),
            pl.BlockSpec((SWA_HEADS, ROW_TILE, LANES), lambda i: (0, i, 0)),
            row(S5_WIDTH),
            pl.BlockSpec((None, VT_ROWS, ROW_TILE), lambda i: (i // per_seq, 0, i % per_seq)),
            row(CONV_WIDTH),
        ],
        out_shape=[
            jax.ShapeDtypeStruct((n, ATT_COLS), BF16),
            jax.ShapeDtypeStruct((SWA_HEADS, n, LANES), BF16),
            jax.ShapeDtypeStruct((n, S5_WIDTH), F32),
            jax.ShapeDtypeStruct((batch, VT_ROWS, seq), BF16),
            jax.ShapeDtypeStruct((n, CONV_WIDTH), BF16),
        ],
        scratch_shapes=[pltpu.VMEM((ROW_TILE + CONV_HALO, CONV_WIDTH), F32),
                        pltpu.VMEM((SUBLANES - 1, CONV_SHIFT_ROWS, CONV_WIDTH), F32)],
        compiler_params=_params("arbitrary"),
        name="mix_proj_conv",
    )(x, w_cu, w_att, w_q, w_su, w_v_t, conv_w, conv_b, conv_g, conv_beta)


def _swa_q_weight(w_aq):
    d = w_aq.shape[0]
    w = (w_aq * (LOG2E / math.sqrt(HEAD_DIM))).reshape(d, SWA_KV_HEADS, SWA_REP, 1, HEAD_DIM)
    sel = jnp.eye(SWA_KV_HEADS, dtype=w.dtype).reshape(1, SWA_KV_HEADS, 1, SWA_KV_HEADS, 1)
    return (w * sel).reshape(d, SWA_HEADS * LANES)


def _swa_kernel(sink_ref, bias_ref, q_ref, kc_ref, kp_ref, vc_ref, vp_ref, o_ref):
    i = pl.program_id(1)
    gw = SWA_REP * BLOCK
    k_all = jnp.concatenate([kp_ref[...], kc_ref[...]], axis=0)
    v_all = jnp.concatenate([vp_ref[...], vc_ref[...]], axis=1)
    vx_all = jnp.concatenate([v_all, jnp.ones((ONES_ROWS, v_all.shape[1]), BF16)], axis=0)
    no_prev = jnp.where(i == 0, NEG_INF, 0.0)

    def scores(blk, g):
        lo = blk * BLOCK
        qg = q_ref[g * SWA_REP:(g + 1) * SWA_REP, lo:lo + BLOCK, :].reshape(gw, LANES)
        s = _dot_nt(k_all[lo:lo + 2 * BLOCK], qg) + bias_ref[g]
        if blk == 0:
            s = jnp.concatenate([s[:BLOCK] + no_prev, s[BLOCK:]], axis=0)
        return s

    units = [(blk, g) for blk in range(SWA_TILE // BLOCK) for g in range(SWA_KV_HEADS)]
    s_next = scores(*units[0])
    outs = []
    for u, (blk, g) in enumerate(units):
        s = s_next
        if u + 1 < len(units):
            s_next = scores(*units[u + 1])
        lo = blk * BLOCK
        sink = sink_ref[g]
        m = jnp.maximum(jnp.max(s, axis=0, keepdims=True), sink)
        e = jnp.exp2(s - m).astype(BF16)
        pv = _dot(vx_all[:, lo:lo + 2 * BLOCK], e)
        den = pv[LANES:LANES + 1] + jnp.exp2(sink - m)
        og = pv[g * HEAD_DIM:(g + 1) * HEAD_DIM] * (1.0 / den)
        outs += [og[:, r * BLOCK:(r + 1) * BLOCK] for r in range(SWA_REP)]
        if g == SWA_KV_HEADS - 1:
            o_ref[lo:lo + BLOCK, :] = jnp.concatenate(outs, axis=0).T.astype(BF16)
            outs = []


def _swa_bias():
    kj = jnp.arange(2 * BLOCK)[:, None]
    qi = jnp.arange(BLOCK)[None, :]
    dist = BLOCK + qi - kj
    valid = (dist >= 0) & (dist < BLOCK)
    slopes = jnp.asarray([_alibi_slope(h, SWA_HEADS) * LOG2E for h in range(SWA_HEADS)], F32)
    b = jnp.where(valid[None], -slopes[:, None, None] * dist[None].astype(F32), NEG_INF)
    b = b.reshape(SWA_KV_HEADS, SWA_REP, 2 * BLOCK, BLOCK)
    return jnp.transpose(b, (0, 2, 1, 3)).reshape(SWA_KV_HEADS, 2 * BLOCK, SWA_REP * BLOCK)


def _swa(h_att, q_swa, v_t, sinks, batch, seq):
    h3 = h_att.reshape(batch, seq, ATT_COLS)
    ck = (2 * D_QK) // A_KV
    rv = D_V // A_KV
    per_seq = seq // SWA_TILE
    r = SWA_TILE // BLOCK
    gw = SWA_REP * BLOCK
    sink_rows = jnp.repeat(sinks.astype(F32).reshape(SWA_KV_HEADS, SWA_REP) * LOG2E, BLOCK, axis=1)
    prev = lambda i: jnp.maximum(i * r - 1, 0)
    return pl.pallas_call(
        _swa_kernel,
        grid=(batch, per_seq),
        in_specs=[
            _resident((SWA_KV_HEADS, 1, gw)),
            _resident((SWA_KV_HEADS, 2 * BLOCK, gw)),
            pl.BlockSpec((SWA_HEADS, SWA_TILE, LANES), lambda b, i: (0, b * per_seq + i, 0)),
            pl.BlockSpec((None, SWA_TILE, A_KV), lambda b, i: (b, i, ck)),
            pl.BlockSpec((None, BLOCK, A_KV), lambda b, i: (b, prev(i), ck)),
            pl.BlockSpec((None, A_KV, SWA_TILE), lambda b, i: (b, rv, i)),
            pl.BlockSpec((None, A_KV, BLOCK), lambda b, i: (b, rv, prev(i))),
        ],
        out_specs=pl.BlockSpec((None, SWA_TILE, A_Q), lambda b, i: (b, i, 0)),
        out_shape=jax.ShapeDtypeStruct((batch, seq, A_Q), BF16),
        compiler_params=_params("parallel", "parallel"),
        name="swa",
    )(sink_rows.reshape(SWA_KV_HEADS, 1, gw), _swa_bias(), q_swa, h3, h3, v_t, v_t).reshape(batch * seq, A_Q)


def _diff_kernel(sc_ref, q_ref, k_ref, vt_ref, g_ref, o_ref,
                 qs_ref, vx_ref, b_ref, s_ref, mx_ref, m_ref, acc_ref, *, n_tiles):
    h = pl.program_id(1)
    t = DIFF_TILE
    hf = t // 2
    v_dim = DIFF_V_DIM
    slope = jnp.float32(_alibi_slope(DIFF_HEADS - 1, DIFF_HEADS) * LOG2E)
    for hh in range(DIFF_HEADS - 1):
        slope = jnp.where(h == hh, jnp.float32(_alibi_slope(hh, DIFF_HEADS) * LOG2E), slope)

    vx_ref[:v_dim, :] = vt_ref[...]
    vx_ref[v_dim:, :] = jnp.ones((DIFF_VROWS - v_dim, vx_ref.shape[1]), BF16)
    krow = lax.broadcasted_iota(jnp.int32, (t, t), 0)
    qcol = lax.broadcasted_iota(jnp.int32, (t, t), 1)
    b_ref[0] = slope * krow.astype(F32)
    b_ref[1] = jnp.where(qcol >= krow, slope * krow.astype(F32), NEG_INF)
    lane = lax.broadcasted_iota(jnp.int32, (t, 2 * HEAD_DIM), 1)
    for qt in range(n_tiles):
        q = q_ref[qt * t:(qt + 1) * t, :].astype(F32) * (LOG2E / math.sqrt(HEAD_DIM))
        qs_ref[qt, 0] = jnp.where(lane < HEAD_DIM, q, 0.0).astype(BF16)
        qs_ref[qt, 1] = jnp.where(lane >= HEAD_DIM, q, 0.0).astype(BF16)

    def items_of(qt):
        out = []
        for j in range(qt):
            out += [(qt, j * t, t, 0, t, None, c) for c in range(2)]
        out += [(qt, qt * t, hf, 0, t, (0, 0), c) for c in range(2)]
        out += [(qt, qt * t + hf, hf, hf, hf, (hf, hf), c) for c in range(2)]
        return out

    def scores(item):
        qt, r0, nr, c0, nc, tri, c = item
        bias = b_ref[0, 0:nr, 0:nc] if tri is None else b_ref[1, tri[0]:tri[0] + nr, tri[1]:tri[1] + nc]
        s = _dot_nt(k_ref[r0:r0 + nr, :], qs_ref[qt, c, c0:c0 + nc, :]) + bias
        s_ref[c, 0:nr, 0:nc] = s
        mx_ref[c, :, 0:nc] = jnp.max(s, axis=0, keepdims=True)

    def accumulate(item, first):
        qt, r0, nr, c0, nc, tri, c = item
        mx = mx_ref[c, :, 0:nc]
        e = jnp.exp2(s_ref[c, 0:nr, 0:nc] - mx).astype(BF16)
        pv = _dot(vx_ref[:, r0:r0 + nr], e)
        mxo = mx + slope * float(r0 - (0 if tri is None else tri[0]))
        if first:
            m_ref[c, :, c0:c0 + nc] = mxo
            acc_ref[c, :, c0:c0 + nc] = pv
        else:
            m_old = m_ref[c, :, c0:c0 + nc]
            m_new = jnp.maximum(m_old, mxo)
            m_ref[c, :, c0:c0 + nc] = m_new
            acc_ref[c, :, c0:c0 + nc] = (jnp.exp2(m_old - m_new) * acc_ref[c, :, c0:c0 + nc]
                                         + jnp.exp2(mxo - m_new) * pv)

    def finalize(qt):
        def normalised(c):
            a = acc_ref[c]
            return a[:v_dim] * (1.0 / a[v_dim:v_dim + 1])

        o = (normalised(0) - sc_ref[0] * normalised(1)).T
        o = o * lax.rsqrt(jnp.mean(o * o, axis=-1, keepdims=True) + LN_EPS) * g_ref[...] * sc_ref[1]
        o_ref[qt * t:(qt + 1) * t, :] = o.astype(BF16)

    items = [it for qt in range(n_tiles) for it in items_of(qt)]
    scores(items[0])
    for n, item in enumerate(items):
        if n + 1 < len(items):
            scores(items[n + 1])
        qt = item[0]
        accumulate(item, first=(n < 2 or items[n - 2][0] != qt))
        if n + 1 == len(items) or items[n + 1][0] != qt:
            finalize(qt)


def _diff_attn(h_att, v_t, scalars, norm_g, batch, seq):
    h3 = h_att.reshape(batch, seq, ATT_COLS)
    w = 2 * HEAD_DIM
    t = DIFF_TILE
    n_tiles = seq // t
    return pl.pallas_call(
        functools.partial(_diff_kernel, n_tiles=n_tiles),
        grid=(batch, DIFF_HEADS),
        in_specs=[
            pl.BlockSpec(memory_space=pltpu.SMEM),
            pl.BlockSpec((None, seq, w), lambda b, h: (b, 0, h)),
            pl.BlockSpec((None, seq, w), lambda b, h: (b, 0, DIFF_HEADS + h)),
            pl.BlockSpec((None, DIFF_V_DIM, seq), lambda b, h: (b, h, 0)),
            _resident((1, DIFF_V_DIM)),
        ],
        out_specs=pl.BlockSpec((None, seq, w), lambda b, h: (b, 0, h)),
        out_shape=jax.ShapeDtypeStruct((batch, seq, D_V), BF16),
        scratch_shapes=[
            pltpu.VMEM((n_tiles, 2, t, w), BF16),
            pltpu.VMEM((DIFF_VROWS, seq), BF16),
            pltpu.VMEM((2, t, t), F32),
            pltpu.VMEM((2, t, t), F32),
            pltpu.VMEM((2, 1, t), F32),
            pltpu.VMEM((2, 1, t), F32),
            pltpu.VMEM((2, DIFF_VROWS, t), F32),
        ],
        compiler_params=_params("parallel", "parallel"),
        name="diff_attn",
    )(scalars, h3, h3, v_t, norm_g).reshape(batch * seq, D_V)


def _s5_kernel(u_ref, bw_ref, cw_ref, air_ref, aii_ref, apr_ref, api_ref, ah_ref, tri_ref,
               d_ref, gw_ref, gb_ref, o_ref, hr_ref, hi_ref):
    i = pl.program_id(1)

    @pl.when(i == 0)
    def _():
        hr_ref[...] = jnp.zeros_like(hr_ref)
        hi_ref[...] = jnp.zeros_like(hi_ref)

    tri = tri_ref[...]
    nq = S5_WIDTH // LANES
    sw = S5_NSTATE // nq
    tt = S5_TILE

    def in_proj(k, j):
        ub = u_ref[k * tt:(k + 1) * tt, j * LANES:(j + 1) * LANES].astype(BF16)
        return _dot(ub, bw_ref[j])

    def finish(k, ys):
        u = u_ref[k * tt:(k + 1) * tt, :]
        y = jnp.concatenate(ys, axis=1) + d_ref[...] * u
        y = jax.nn.gelu(y, approximate=True)
        gate = _dot(y.astype(BF16), gw_ref[...]) + gb_ref[...]
        o_ref[k * tt:(k + 1) * tt, :] = (y * jax.nn.sigmoid(gate)).astype(BF16)

    units = [(k, j) for k in range(S5_STEP // tt) for j in range(nq)]
    bu_next = in_proj(*units[0])
    prev = None
    ys = []
    for n, (k, j) in enumerate(units):
        sl = slice(j * sw, (j + 1) * sw)
        bu = bu_next
        if n + 1 < len(units):
            bu_next = in_proj(*units[n + 1])
        bur, bui = bu[:, :sw], bu[:, sw:]
        air, aii = air_ref[:, sl], aii_ref[:, sl]
        zr = (air * bur - aii * bui).astype(BF16)
        zi = (air * bui + aii * bur).astype(BF16)
        cr = _dot(tri, zr) + hr_ref[:, sl]
        ci = _dot(tri, zi) + hi_ref[:, sl]
        if prev is not None:
            ys.append(_dot(prev[2], cw_ref[prev[1]]))
            if prev[1] == nq - 1:
                finish(prev[0], ys)
                ys = []
        apr, api = apr_ref[:, sl], api_ref[:, sl]
        xr = apr * cr - api * ci
        xi = apr * ci + api * cr
        lr, li = xr[tt - 1:, :], xi[tt - 1:, :]
        ahr, ahi = ah_ref[0:1, sl], ah_ref[1:2, sl]
        hr_ref[:, sl] = ahr * lr - ahi * li
        hi_ref[:, sl] = ahr * li + ahi * lr
        prev = (k, j, jnp.concatenate([xr, xi], axis=1).astype(BF16))
    ys.append(_dot(prev[2], cw_ref[prev[1]]))
    finish(prev[0], ys)


def _s5_tables(a_re, a_im, log_step, b_re, b_im, c_re, c_im):
    g, p, c = S5_GROUPS, S5_STATE, S5_GROUP
    step = jnp.exp(log_step)[:, None]
    mag = jnp.exp(a_re * step)
    abar_r, abar_i = mag * jnp.cos(a_im * step), mag * jnp.sin(a_im * step)
    den = a_re * a_re + a_im * a_im
    nr, ni = abar_r - 1.0, abar_i
    coef_r = (nr * a_re + ni * a_im) / den
    coef_i = (ni * a_re - nr * a_im) / den
    bbar_r = coef_r[..., None] * b_re - coef_i[..., None] * b_im
    bbar_i = coef_r[..., None] * b_im + coef_i[..., None] * b_re
    nq = S5_WIDTH // LANES
    gq = g // nq
    eye = jnp.eye(gq, dtype=F32)

    def in_slab(bb):
        bb = bb.reshape(nq, gq, p, c)
        return jnp.einsum('qgpc,gh->qgchp', bb, eye).reshape(nq, gq * c, gq * p)

    bw = jnp.concatenate([in_slab(bbar_r), in_slab(bbar_i)], axis=2).astype(BF16)

    def out_slab(cc):
        cc = cc.reshape(nq, gq, c, p)
        return jnp.einsum('qgcp,gh->qgphc', cc, eye).reshape(nq, gq * p, gq * c)

    cw = jnp.concatenate([out_slab(c_re), -out_slab(c_im)], axis=1).astype(BF16)
    half = S5_TILE // 2
    tt = jnp.arange(1 - half, S5_TILE + 1 - half, dtype=F32)[:, None]
    la = (a_re * step).reshape(1, g * p)
    th = (a_im * step).reshape(1, g * p)
    pm, ang = jnp.exp(tt * la), tt * th
    im_ = jnp.exp(-tt * la)
    apr, api = pm * jnp.cos(ang), pm * jnp.sin(ang)
    air, aii = im_ * jnp.cos(ang), -im_ * jnp.sin(ang)
    hm = jnp.exp(half * la)
    ah = jnp.concatenate([hm * jnp.cos(half * th), hm * jnp.sin(half * th)], axis=0)
    return bw, cw, air, aii, apr, api, ah


def _s5(su, tables, d_skip, glu_w, glu_b, batch, seq):
    bw, cw, air, aii, apr, api, ah = tables
    h3 = su.reshape(batch, seq, S5_WIDTH)
    tri = jnp.tril(jnp.ones((S5_TILE, S5_TILE), F32)).astype(BF16)
    tab = (S5_TILE, S5_NSTATE)
    return pl.pallas_call(
        _s5_kernel,
        grid=(batch, seq // S5_STEP),
        in_specs=[
            pl.BlockSpec((None, S5_STEP, S5_WIDTH), lambda b, i: (b, i, 0)),
            _resident(bw.shape), _resident(cw.shape),
            _resident(tab), _resident(tab), _resident(tab), _resident(tab),
            _resident((2, S5_NSTATE)),
            _resident((S5_TILE, S5_TILE)),
            _resident((1, S5_WIDTH)),
            _resident((S5_WIDTH, S5_WIDTH)),
            _resident((1, S5_WIDTH)),
        ],
        out_specs=pl.BlockSpec((None, S5_STEP, S5_WIDTH), lambda b, i: (b, i, 0)),
        out_shape=jax.ShapeDtypeStruct((batch, seq, S5_WIDTH), BF16),
        scratch_shapes=[pltpu.VMEM((1, S5_NSTATE), F32), pltpu.VMEM((1, S5_NSTATE), F32)],
        compiler_params=_params("parallel", "arbitrary"),
        name="s5_ssm",
    )(h3, bw, cw, air, aii, apr, api, ah, tri, d_skip, glu_w, glu_b).reshape(batch * seq, S5_WIDTH)


def _merge_kernel(x_ref, oa_ref, ob_ref, oc_ref, od_ref, wg_ref, pa_ref, pb_ref, pc_ref, pd_ref,
                  wo_ref, g_ref, b_ref, o_ref):
    half = ROW_TILE // 2
    for r in range(2):
        rows = slice(r * half, (r + 1) * half)
        x = x_ref[rows, :]
        xb = x.astype(BF16)
        merged = jnp.zeros((half, D_MODEL), F32)
        for i, (br_ref, pr_ref) in enumerate(((oa_ref, pa_ref), (ob_ref, pb_ref), (oc_ref, pc_ref), (od_ref, pd_ref))):
            gl = _dot(xb, wg_ref[:, i * D_MODEL:(i + 1) * D_MODEL])
            merged = merged + jax.nn.sigmoid(gl) * _dot(br_ref[rows, :], pr_ref[...])
        m = _dot(merged.astype(BF16), wo_ref[...])
        o_ref[rows, :] = _layer_norm(DEEPNORM_ALPHA * x + m, g_ref[...], b_ref[...])


def _merge(x, oa, ob, oc, od, w_gate, pa, pb, pc, pd, w_out, g, b):
    n = x.shape[0]
    row = lambda w: pl.BlockSpec((ROW_TILE, w), lambda i: (i, 0))
    return pl.pallas_call(
        _merge_kernel,
        grid=(n // ROW_TILE,),
        in_specs=[
            row(D_MODEL), row(A_Q), row(S5_WIDTH), row(CONV_WIDTH), row(D_V),
            _resident((D_MODEL, GATE_COLS)),
            _resident((A_Q, D_MODEL)), _resident((S5_WIDTH, D_MODEL)),
            _resident((CONV_WIDTH, D_MODEL)), _resident((D_V, D_MODEL)),
            _resident((D_MODEL, D_MODEL)),
            _resident((1, D_MODEL)), _resident((1, D_MODEL)),
        ],
        out_specs=row(D_MODEL),
        out_shape=jax.ShapeDtypeStruct((n, D_MODEL), F32),
        compiler_params=_params("parallel"),
        name="gated_merge",
    )(x, oa, ob, oc, od, w_gate, pa, pb, pc, pd, w_out, g, b)


def _mem_kv_kernel(m_ref, g_ref, b_ref, w_ref, o_ref):
    mn = _layer_norm(m_ref[...], g_ref[...], b_ref[...])
    o_ref[...] = _dot(mn.astype(BF16), w_ref[...]).astype(BF16)


def _mem_kv(mem, g, b, wkv):
    n = mem.shape[0]
    return pl.pallas_call(
        _mem_kv_kernel,
        grid=(DEPTH, n // ROW_TILE),
        in_specs=[
            pl.BlockSpec((ROW_TILE, D_MODEL), lambda l, i: (i, 0)),
            _resident((1, D_MODEL)), _resident((1, D_MODEL)),
            pl.BlockSpec((None, D_MODEL, 2 * CROSS_WIDTH), lambda l, i: (l, 0, 0)),
        ],
        out_specs=pl.BlockSpec((None, ROW_TILE, 2 * CROSS_WIDTH), lambda l, i: (l, i, 0)),
        out_shape=jax.ShapeDtypeStruct((DEPTH, n, 2 * CROSS_WIDTH), BF16),
        compiler_params=_params("parallel", "parallel"),
        name="mem_kv",
    )(mem, g, b, wkv)


def _cross_kernel(x_ref, kv_ref, wq_ref, wo_ref, g_ref, b_ref, o_ref):
    x = x_ref[...]
    q = _dot(x.astype(BF16), wq_ref[...]).astype(BF16)
    kv = kv_ref[...]
    scale = 1.0 / math.sqrt(CROSS_HEAD_DIM)
    outs = []

    def scores(h):
        sl = slice(h * CROSS_HEAD_DIM, (h + 1) * CROSS_HEAD_DIM)
        return _dot_nt(q[:, sl], kv[:, sl]) * scale

    s_next = scores(0)
    for h in range(CROSS_HEADS):
        s = s_next
        if h + 1 < CROSS_HEADS:
            s_next = scores(h + 1)
        p = jnp.exp(s - jnp.max(s, axis=-1, keepdims=True))
        p = p / jnp.sum(p, axis=-1, keepdims=True)
        outs.append(_dot(p.astype(BF16), kv[:, CROSS_WIDTH + h * CROSS_HEAD_DIM:CROSS_WIDTH + (h + 1) * CROSS_HEAD_DIM]))
    o = jnp.concatenate(outs, axis=1).astype(BF16)
    c = _dot(o, wo_ref[...])
    o_ref[...] = _layer_norm(DEEPNORM_ALPHA * x + c, g_ref[...], b_ref[...])


def _cross_attn(x, kv, wq, wo, g, b, batch, seq):
    x3 = x.reshape(batch, seq, D_MODEL)
    kv3 = kv.reshape(batch, MEM_LEN, 2 * CROSS_WIDTH)
    return pl.pallas_call(
        _cross_kernel,
        grid=(batch, seq // CROSS_TILE),
        in_specs=[
            pl.BlockSpec((None, CROSS_TILE, D_MODEL), lambda b, i: (b, i, 0)),
            pl.BlockSpec((None, MEM_LEN, 2 * CROSS_WIDTH), lambda b, i: (b, 0, 0)),
            _resident((D_MODEL, CROSS_WIDTH)),
            _resident((CROSS_WIDTH, D_MODEL)),
            _resident((1, D_MODEL)), _resident((1, D_MODEL)),
        ],
        out_specs=pl.BlockSpec((None, CROSS_TILE, D_MODEL), lambda b, i: (b, i, 0)),
        out_shape=jax.ShapeDtypeStruct((batch, seq, D_MODEL), F32),
        compiler_params=_params("parallel", "parallel"),
        name="cross_attn",
    )(x3, kv3, wq, wo, g, b).reshape(batch * seq, D_MODEL)


def _row(v):
    return v.reshape(1, -1).astype(F32)


def kernel(x, mem, ffn1_w_in, ffn1_w_out, ffn1_ln_g, ffn1_ln_b, mix_w_in, swa_sinks, swa_proj, s5_a_re, s5_a_im, s5_log_step, s5_b_re, s5_b_im, s5_c_re, s5_c_im, s5_d, s5_glu_w, s5_glu_b, s5_proj, conv_w, conv_b, conv_ln_g, conv_ln_b, conv_proj, diff_lq1, diff_lk1, diff_lq2, diff_lk2, diff_norm_g, diff_proj, mix_w_out, mix_ln_g, mix_ln_b, mem_ln_g, mem_ln_b, cross_wq, cross_wkv, cross_wo, cross_ln_g, cross_ln_b, ffn2_w_in, ffn2_w_out, ffn2_ln_g, ffn2_ln_b):
    batch, seq, _ = x.shape
    n = batch * seq
    assert seq % DIFF_TILE == 0 and seq % ROW_TILE == 0 and seq % CROSS_TILE == 0
    assert seq % S5_STEP == 0 and seq % SWA_TILE == 0 and (batch * MEM_LEN) % ROW_TILE == 0
    h = x.reshape(n, D_MODEL)
    kv_all = _mem_kv(mem.reshape(batch * MEM_LEN, D_MODEL), _row(mem_ln_g), _row(mem_ln_b),
                     cross_wkv.astype(BF16))
    for l in range(DEPTH):
        h = _ffn_ln(h, ffn1_w_in[l].astype(BF16), ffn1_w_out[l].astype(BF16),
                    _row(ffn1_ln_g[l]), _row(ffn1_ln_b[l]))
        lambda_init = 0.8 - 0.6 * math.exp(-0.3 * l)
        w = mix_w_in[l]
        cols = lambda off, width: w[:, off:off + width]
        w_att = jnp.concatenate([cols(OFF_DQ, D_QK), cols(OFF_DK, D_QK), cols(OFF_AK, A_KV)], axis=1).astype(BF16)
        w_q = _swa_q_weight(cols(OFF_AQ, A_Q)).astype(BF16)
        w_v_t = jnp.concatenate([cols(OFF_DV, D_V), cols(OFF_AV, A_KV)], axis=1).T.astype(BF16)
        w_gate = cols(OFF_GL, GATE_COLS).astype(BF16)
        h_att, q_swa, su, v_t, o_c = _mix_proj(
            h, cols(OFF_CU, CONV_IN).astype(BF16), w_att, w_q, cols(OFF_SU, S5_WIDTH).astype(BF16), w_v_t,
            conv_w[l], _row(conv_b[l]), _row(conv_ln_g[l]), _row(conv_ln_b[l]), batch, seq)
        o_a = _swa(h_att, q_swa, v_t, swa_sinks[l], batch, seq)
        tables = _s5_tables(s5_a_re[l], s5_a_im[l], s5_log_step[l], s5_b_re[l], s5_b_im[l], s5_c_re[l], s5_c_im[l])
        o_b = _s5(su, tables, _row(s5_d[l]), s5_glu_w[l].astype(BF16), _row(s5_glu_b[l]), batch, seq)
        lam = (jnp.exp(jnp.sum(diff_lq1[l] * diff_lk1[l])) - jnp.exp(jnp.sum(diff_lq2[l] * diff_lk2[l]))
               + lambda_init).astype(F32)
        o_d = _diff_attn(h_att, v_t, jnp.stack([lam, jnp.float32(1.0 - lambda_init)]), _row(diff_norm_g[l]), batch, seq)
        h = _merge(h, o_a, o_b, o_c, o_d, w_gate, swa_proj[l].astype(BF16), s5_proj[l].astype(BF16),
                   conv_proj[l].astype(BF16), diff_proj[l].astype(BF16), mix_w_out[l].astype(BF16),
                   _row(mix_ln_g[l]), _row(mix_ln_b[l]))
        h = _cross_attn(h, kv_all[l], cross_wq[l].astype(BF16), cross_wo[l].astype(BF16),
                        _row(cross_ln_g[l]), _row(cross_ln_b[l]), batch, seq)
        h = _ffn_ln(h, ffn2_w_in[l].astype(BF16), ffn2_w_out[l].astype(BF16),
                    _row(ffn2_ln_g[l]), _row(ffn2_ln_b[l]))
    return h.reshape(batch, seq, D_MODEL)
```

```python
import functools
import math

import jax
import jax.numpy as jnp
from jax import lax
from jax.experimental import pallas as pl
from jax.experimental.pallas import tpu as pltpu

F32 = jnp.float32
BF16 = jnp.bfloat16

D_MODEL = 1024
DEPTH = 4
MEM_LEN = 256
HEAD_DIM = 64
BLOCK = 128
SWA_HEADS = 8
SWA_KV_HEADS = 2
SWA_REP = SWA_HEADS // SWA_KV_HEADS
S5_WIDTH = 512
S5_GROUP = 16
S5_GROUPS = S5_WIDTH // S5_GROUP
S5_STATE = 64
S5_NSTATE = S5_GROUPS * S5_STATE
CONV_WIDTH = 512
CONV_K = 31
DIFF_HEADS = 4
DIFF_V_DIM = 2 * HEAD_DIM
CROSS_HEADS = 4
CROSS_HEAD_DIM = 128
CROSS_WIDTH = CROSS_HEADS * CROSS_HEAD_DIM
FFN_DIM = 2816
N_BRANCHES = 4

A_Q = SWA_HEADS * HEAD_DIM
A_KV = SWA_KV_HEADS * HEAD_DIM
D_QK = DIFF_HEADS * 2 * HEAD_DIM
D_V = DIFF_HEADS * DIFF_V_DIM
CONV_IN = 2 * CONV_WIDTH
GATE_COLS = N_BRANCHES * D_MODEL
OFF_AQ = 0
OFF_AK = OFF_AQ + A_Q
OFF_AV = OFF_AK + A_KV
OFF_DQ = OFF_AV + A_KV
OFF_DK = OFF_DQ + D_QK
OFF_DV = OFF_DK + D_QK
OFF_SU = OFF_DV + D_V
OFF_CU = OFF_SU + S5_WIDTH
OFF_GL = OFF_CU + CONV_IN
ATT_COLS = 2 * D_QK + A_KV
VT_ROWS = D_V + A_KV

DEEPNORM_ALPHA = (2.0 * DEPTH) ** 0.25
LN_EPS = 1e-5
NEG_INF = -1e30

LANES = 128
SUBLANES = 8
MXU_WIDTH = 256
VMEM_LIMIT = 56 * 1024 * 1024

ROW_TILE = 512
FFN_CHUNK = MXU_WIDTH
FFN_ROWS = 1024
FFN_PART = 512
DIFF_TILE = 512
ONES_ROWS = 16
DIFF_VROWS = DIFF_V_DIM + ONES_ROWS
SWA_TILE = 512
LOG2E = math.log2(math.e)
CONV_HALO = 32
S5_TILE = 256
S5_STEP = 512
CROSS_TILE = 512


def _alibi_slope(h, n):
    return 2.0 ** (-8.0 * (h + 1) / n)


def _layer_norm(z, g, b):
    mu = jnp.mean(z, axis=-1, keepdims=True)
    zc = z - mu
    var = jnp.mean(zc * zc, axis=-1, keepdims=True)
    return zc * lax.rsqrt(var + LN_EPS) * g + b


def _dot(a, b):
    return jnp.dot(a, b, preferred_element_type=F32)


def _dot_nt(a, b):
    return lax.dot_general(a, b, (((1,), (1,)), ((), ())), preferred_element_type=F32)


def _params(*sem):
    return pltpu.CompilerParams(dimension_semantics=sem, vmem_limit_bytes=VMEM_LIMIT)


def _resident(shape):
    nd = len(shape)
    return pl.BlockSpec(shape, lambda *_: (0,) * nd, pipeline_mode=pl.Buffered(1))


def _ffn_ln_kernel(x_ref, wg_ref, wu_ref, wo_ref, g_ref, b_ref, o_ref, a_ref):
    half = FFN_PART
    for r in range(FFN_ROWS // FFN_PART):
        rows = slice(r * half, (r + 1) * half)
        x = x_ref[rows, :]
        xb = x.astype(BF16)
        for c in range(FFN_DIM // FFN_CHUNK):
            sl = slice(c * FFN_CHUNK, (c + 1) * FFN_CHUNK)
            gate = _dot(xb, wg_ref[:, sl])
            up = _dot(xb, wu_ref[:, sl])
            a_ref[rows, sl] = (gate * jax.nn.sigmoid(gate) * up).astype(BF16)
        f = _dot(a_ref[rows, :], wo_ref[...])
        o_ref[rows, :] = _layer_norm(DEEPNORM_ALPHA * x + 0.5 * f, g_ref[...], b_ref[...])


def _ffn_ln(x, w_in, w_out, g, b):
    n = x.shape[0]
    return pl.pallas_call(
        _ffn_ln_kernel,
        grid=(n // FFN_ROWS,),
        in_specs=[
            pl.BlockSpec((FFN_ROWS, D_MODEL), lambda i: (i, 0)),
            pl.BlockSpec((D_MODEL, FFN_DIM), lambda i: (0, 0), pipeline_mode=pl.Buffered(1)),
            pl.BlockSpec((D_MODEL, FFN_DIM), lambda i: (0, 1), pipeline_mode=pl.Buffered(1)),
            _resident((FFN_DIM, D_MODEL)),
            _resident((1, D_MODEL)),
            _resident((1, D_MODEL)),
        ],
        out_specs=pl.BlockSpec((FFN_ROWS, D_MODEL), lambda i: (i, 0)),
        out_shape=jax.ShapeDtypeStruct((n, D_MODEL), F32),
        scratch_shapes=[pltpu.VMEM((FFN_ROWS, FFN_DIM), BF16)],
        compiler_params=_params("parallel"),
        name="ffn_ln",
    )(x, w_in, w_in, w_out, g, b)


def _mix_proj_kernel(x_ref, wc_ref, wa_ref, wq_ref, ws_ref, wvt_ref, cw_ref, cb_ref, cg_ref, cbeta_ref,
                     ha_ref, q_ref, su_ref, vt_ref, oc_ref, buf_ref, *, per_seq):
    i = pl.program_id(0)

    @pl.when(i == 0)
    def _():
        buf_ref[ROW_TILE:, :] = jnp.zeros((CONV_HALO, CONV_WIDTH), F32)

    xb = x_ref[...].astype(BF16)
    cu = _dot(xb, wc_ref[...])
    ha_ref[...] = _dot(xb, wa_ref[...]).astype(BF16)
    q = _dot(xb, wq_ref[...])
    for h in range(SWA_HEADS):
        q_ref[h] = q[:, h * LANES:(h + 1) * LANES].astype(BF16)
    su_ref[...] = _dot(xb, ws_ref[...])
    vt_ref[...] = _dot_nt(wvt_ref[...], xb).astype(BF16)
    keep = (i % per_seq != 0).astype(F32)
    buf_ref[0:CONV_HALO, :] = buf_ref[ROW_TILE:, :] * keep
    buf_ref[CONV_HALO:, :] = cu[:, :CONV_WIDTH] * jax.nn.sigmoid(cu[:, CONV_WIDTH:])
    bufv = buf_ref[...]
    nrows = ROW_TILE + CONV_HALO
    shifted = [bufv] + [pltpu.roll(bufv, nrows - b, axis=0) for b in range(1, SUBLANES)]
    first = CONV_HALO - (CONV_K - 1)
    acc = jnp.zeros((ROW_TILE, CONV_WIDTH), F32)
    for k in range(CONV_K):
        a, b = divmod(first + k, SUBLANES)
        lo = a * SUBLANES
        acc = acc + cw_ref[k:k + 1, :] * shifted[b][lo:lo + ROW_TILE, :]
    y = _layer_norm(acc + cb_ref[...], cg_ref[...], cbeta_ref[...])
    oc_ref[...] = (y * jax.nn.sigmoid(y)).astype(BF16)


def _mix_proj(x, w_cu, w_att, w_q, w_su, w_v_t, conv_w, conv_b, conv_g, conv_beta, batch, seq):
    n = x.shape[0]
    per_seq = seq // ROW_TILE
    row = lambda w: pl.BlockSpec((ROW_TILE, w), lambda i: (i, 0))
    return pl.pallas_call(
        functools.partial(_mix_proj_kernel, per_seq=per_seq),
        grid=(n // ROW_TILE,),
        in_specs=[
            row(D_MODEL),
            _resident((D_MODEL, CONV_IN)),
            _resident((D_MODEL, ATT_COLS)),
            _resident((D_MODEL, SWA_HEADS * LANES)),
            _resident((D_MODEL, S5_WIDTH)),
            _resident((VT_ROWS, D_MODEL)),
            _resident((CONV_K, CONV_WIDTH)),
            _resident((1, CONV_WIDTH)), _resident((1, CONV_WIDTH)), _resident((1, CONV_WIDTH)),
        ],
        out_specs=[
            row(ATT_COLS),
            pl.BlockSpec((SWA_HEADS, ROW_TILE, LANES), lambda i: (0, i, 0)),
            row(S5_WIDTH),
            pl.BlockSpec((None, VT_ROWS, ROW_TILE), lambda i: (i // per_seq, 0, i % per_seq)),
            row(CONV_WIDTH),
        ],
        out_shape=[
            jax.ShapeDtypeStruct((n, ATT_COLS), BF16),
            jax.ShapeDtypeStruct((SWA_HEADS, n, LANES), BF16),
            jax.ShapeDtypeStruct((n, S5_WIDTH), F32),
            jax.ShapeDtypeStruct((batch, VT_ROWS, seq), BF16),
            jax.ShapeDtypeStruct((n, CONV_WIDTH), BF16),
        ],
        scratch_shapes=[pltpu.VMEM((ROW_TILE + CONV_HALO, CONV_WIDTH), F32)],
        compiler_params=_params("arbitrary"),
        name="mix_proj_conv",
    )(x, w_cu, w_att, w_q, w_su, w_v_t, conv_w, conv_b, conv_g, conv_beta)


def _swa_q_weight(w_aq):
    d = w_aq.shape[0]
    w = (w_aq * (LOG2E / math.sqrt(HEAD_DIM))).reshape(d, SWA_KV_HEADS, SWA_REP, 1, HEAD_DIM)
    sel = jnp.eye(SWA_KV_HEADS, dtype=w.dtype).reshape(1, SWA_KV_HEADS, 1, SWA_KV_HEADS, 1)
    return (w * sel).reshape(d, SWA_HEADS * LANES)


def _swa_kernel(sink_ref, bias_ref, q_ref, kc_ref, kp_ref, vc_ref, vp_ref, o_ref):
    i = pl.program_id(1)
    gw = SWA_REP * BLOCK
    k_all = jnp.concatenate([kp_ref[...], kc_ref[...]], axis=0)
    v_all = jnp.concatenate([vp_ref[...], vc_ref[...]], axis=1)
    vx_all = jnp.concatenate([v_all, jnp.ones((ONES_ROWS, v_all.shape[1]), BF16)], axis=0)
    no_prev = jnp.where(i == 0, NEG_INF, 0.0)

    def scores(blk, g):
        lo = blk * BLOCK
        qg = q_ref[g * SWA_REP:(g + 1) * SWA_REP, lo:lo + BLOCK, :].reshape(gw, LANES)
        s = _dot_nt(k_all[lo:lo + 2 * BLOCK], qg) + bias_ref[g]
        if blk == 0:
            s = jnp.concatenate([s[:BLOCK] + no_prev, s[BLOCK:]], axis=0)
        return s

    units = [(blk, g) for blk in range(SWA_TILE // BLOCK) for g in range(SWA_KV_HEADS)]
    s_next = scores(*units[0])
    outs = []
    for u, (blk, g) in enumerate(units):
        s = s_next
        if u + 1 < len(units):
            s_next = scores(*units[u + 1])
        lo = blk * BLOCK
        sink = sink_ref[g]
        m = jnp.maximum(jnp.max(s, axis=0, keepdims=True), sink)
        e = jnp.exp2(s - m).astype(BF16)
        pv = _dot(vx_all[:, lo:lo + 2 * BLOCK], e)
        den = pv[LANES:LANES + 1] + jnp.exp2(sink - m)
        og = pv[g * HEAD_DIM:(g + 1) * HEAD_DIM] * (1.0 / den)
        outs += [og[:, r * BLOCK:(r + 1) * BLOCK] for r in range(SWA_REP)]
        if g == SWA_KV_HEADS - 1:
            o_ref[lo:lo + BLOCK, :] = jnp.concatenate(outs, axis=0).T.astype(BF16)
            outs = []


def _swa_bias():
    kj = jnp.arange(2 * BLOCK)[:, None]
    qi = jnp.arange(BLOCK)[None, :]
    dist = BLOCK + qi - kj
    valid = (dist >= 0) & (dist < BLOCK)
    slopes = jnp.asarray([_alibi_slope(h, SWA_HEADS) * LOG2E for h in range(SWA_HEADS)], F32)
    b = jnp.where(valid[None], -slopes[:, None, None] * dist[None].astype(F32), NEG_INF)
    b = b.reshape(SWA_KV_HEADS, SWA_REP, 2 * BLOCK, BLOCK)
    return jnp.transpose(b, (0, 2, 1, 3)).reshape(SWA_KV_HEADS, 2 * BLOCK, SWA_REP * BLOCK)


def _swa(h_att, q_swa, v_t, sinks, batch, seq):
    h3 = h_att.reshape(batch, seq, ATT_COLS)
    ck = (2 * D_QK) // A_KV
    rv = D_V // A_KV
    per_seq = seq // SWA_TILE
    r = SWA_TILE // BLOCK
    gw = SWA_REP * BLOCK
    sink_rows = jnp.repeat(sinks.astype(F32).reshape(SWA_KV_HEADS, SWA_REP) * LOG2E, BLOCK, axis=1)
    prev = lambda i: jnp.maximum(i * r - 1, 0)
    return pl.pallas_call(
        _swa_kernel,
        grid=(batch, per_seq),
        in_specs=[
            _resident((SWA_KV_HEADS, 1, gw)),
            _resident((SWA_KV_HEADS, 2 * BLOCK, gw)),
            pl.BlockSpec((SWA_HEADS, SWA_TILE, LANES), lambda b, i: (0, b * per_seq + i, 0)),
            pl.BlockSpec((None, SWA_TILE, A_KV), lambda b, i: (b, i, ck)),
            pl.BlockSpec((None, BLOCK, A_KV), lambda b, i: (b, prev(i), ck)),
            pl.BlockSpec((None, A_KV, SWA_TILE), lambda b, i: (b, rv, i)),
            pl.BlockSpec((None, A_KV, BLOCK), lambda b, i: (b, rv, prev(i))),
        ],
        out_specs=pl.BlockSpec((None, SWA_TILE, A_Q), lambda b, i: (b, i, 0)),
        out_shape=jax.ShapeDtypeStruct((batch, seq, A_Q), BF16),
        compiler_params=_params("parallel", "parallel"),
        name="swa",
    )(sink_rows.reshape(SWA_KV_HEADS, 1, gw), _swa_bias(), q_swa, h3, h3, v_t, v_t).reshape(batch * seq, A_Q)


def _diff_kernel(sc_ref, q_ref, k_ref, vt_ref, g_ref, o_ref,
                 qs_ref, vx_ref, b_ref, s_ref, mx_ref, m_ref, acc_ref, *, n_tiles):
    h = pl.program_id(1)
    t = DIFF_TILE
    hf = t // 2
    v_dim = DIFF_V_DIM
    slope = jnp.float32(_alibi_slope(DIFF_HEADS - 1, DIFF_HEADS) * LOG2E)
    for hh in range(DIFF_HEADS - 1):
        slope = jnp.where(h == hh, jnp.float32(_alibi_slope(hh, DIFF_HEADS) * LOG2E), slope)

    vx_ref[:v_dim, :] = vt_ref[...]
    vx_ref[v_dim:, :] = jnp.ones((DIFF_VROWS - v_dim, vx_ref.shape[1]), BF16)
    krow = lax.broadcasted_iota(jnp.int32, (t, t), 0)
    qcol = lax.broadcasted_iota(jnp.int32, (t, t), 1)
    b_ref[0] = slope * krow.astype(F32)
    b_ref[1] = jnp.where(qcol >= krow, slope * krow.astype(F32), NEG_INF)
    lane = lax.broadcasted_iota(jnp.int32, (t, 2 * HEAD_DIM), 1)
    for qt in range(n_tiles):
        q = q_ref[qt * t:(qt + 1) * t, :].astype(F32) * (LOG2E / math.sqrt(HEAD_DIM))
        qs_ref[qt, 0] = jnp.where(lane < HEAD_DIM, q, 0.0).astype(BF16)
        qs_ref[qt, 1] = jnp.where(lane >= HEAD_DIM, q, 0.0).astype(BF16)

    def items_of(qt):
        out = []
        for j in range(qt):
            out += [(qt, j * t, t, 0, t, None, c) for c in range(2)]
        out += [(qt, qt * t, hf, 0, t, (0, 0), c) for c in range(2)]
        out += [(qt, qt * t + hf, hf, hf, hf, (hf, hf), c) for c in range(2)]
        return out

    def scores(item):
        qt, r0, nr, c0, nc, tri, c = item
        bias = b_ref[0, 0:nr, 0:nc] if tri is None else b_ref[1, tri[0]:tri[0] + nr, tri[1]:tri[1] + nc]
        s = _dot_nt(k_ref[r0:r0 + nr, :], qs_ref[qt, c, c0:c0 + nc, :]) + bias
        s_ref[c, 0:nr, 0:nc] = s
        mx_ref[c, :, 0:nc] = jnp.max(s, axis=0, keepdims=True)

    def accumulate(item, first):
        qt, r0, nr, c0, nc, tri, c = item
        mx = mx_ref[c, :, 0:nc]
        e = jnp.exp2(s_ref[c, 0:nr, 0:nc] - mx).astype(BF16)
        pv = _dot(vx_ref[:, r0:r0 + nr], e)
        mxo = mx + slope * float(r0 - (0 if tri is None else tri[0]))
        if first:
            m_ref[c, :, c0:c0 + nc] = mxo
            acc_ref[c, :, c0:c0 + nc] = pv
        else:
            m_old = m_ref[c, :, c0:c0 + nc]
            m_new = jnp.maximum(m_old, mxo)
            m_ref[c, :, c0:c0 + nc] = m_new
            acc_ref[c, :, c0:c0 + nc] = (jnp.exp2(m_old - m_new) * acc_ref[c, :, c0:c0 + nc]
                                         + jnp.exp2(mxo - m_new) * pv)

    def finalize(qt):
        def normalised(c):
            a = acc_ref[c]
            return a[:v_dim] * (1.0 / a[v_dim:v_dim + 1])

        o = (normalised(0) - sc_ref[0] * normalised(1)).T
        o = o * lax.rsqrt(jnp.mean(o * o, axis=-1, keepdims=True) + LN_EPS) * g_ref[...] * sc_ref[1]
        o_ref[qt * t:(qt + 1) * t, :] = o.astype(BF16)

    items = [it for qt in range(n_tiles) for it in items_of(qt)]
    scores(items[0])
    for n, item in enumerate(items):
        if n + 1 < len(items):
            scores(items[n + 1])
        qt = item[0]
        accumulate(item, first=(n < 2 or items[n - 2][0] != qt))
        if n + 1 == len(items) or items[n + 1][0] != qt:
            finalize(qt)


def _diff_attn(h_att, v_t, scalars, norm_g, batch, seq):
    h3 = h_att.reshape(batch, seq, ATT_COLS)
    w = 2 * HEAD_DIM
    t = DIFF_TILE
    n_tiles = seq // t
    return pl.pallas_call(
        functools.partial(_diff_kernel, n_tiles=n_tiles),
        grid=(batch, DIFF_HEADS),
        in_specs=[
            pl.BlockSpec(memory_space=pltpu.SMEM),
            pl.BlockSpec((None, seq, w), lambda b, h: (b, 0, h)),
            pl.BlockSpec((None, seq, w), lambda b, h: (b, 0, DIFF_HEADS + h)),
            pl.BlockSpec((None, DIFF_V_DIM, seq), lambda b, h: (b, h, 0)),
            _resident((1, DIFF_V_DIM)),
        ],
        out_specs=pl.BlockSpec((None, seq, w), lambda b, h: (b, 0, h)),
        out_shape=jax.ShapeDtypeStruct((batch, seq, D_V), BF16),
        scratch_shapes=[
            pltpu.VMEM((n_tiles, 2, t, w), BF16),
            pltpu.VMEM((DIFF_VROWS, seq), BF16),
            pltpu.VMEM((2, t, t), F32),
            pltpu.VMEM((2, t, t), F32),
            pltpu.VMEM((2, 1, t), F32),
            pltpu.VMEM((2, 1, t), F32),
            pltpu.VMEM((2, DIFF_VROWS, t), F32),
        ],
        compiler_params=_params("parallel", "parallel"),
        name="diff_attn",
    )(scalars, h3, h3, v_t, norm_g).reshape(batch * seq, D_V)


def _s5_kernel(u_ref, bw_ref, cw_ref, air_ref, aii_ref, apr_ref, api_ref, ah_ref, tri_ref,
               d_ref, gw_ref, gb_ref, o_ref, hr_ref, hi_ref):
    i = pl.program_id(1)

    @pl.when(i == 0)
    def _():
        hr_ref[...] = jnp.zeros_like(hr_ref)
        hi_ref[...] = jnp.zeros_like(hi_ref)

    tri = tri_ref[...]
    nq = S5_WIDTH // LANES
    sw = S5_NSTATE // nq
    tt = S5_TILE

    def in_proj(k, j):
        ub = u_ref[k * tt:(k + 1) * tt, j * LANES:(j + 1) * LANES].astype(BF16)
        return _dot(ub, bw_ref[j])

    def finish(k, ys):
        u = u_ref[k * tt:(k + 1) * tt, :]
        y = jnp.concatenate(ys, axis=1) + d_ref[...] * u
        y = jax.nn.gelu(y, approximate=True)
        gate = _dot(y.astype(BF16), gw_ref[...]) + gb_ref[...]
        o_ref[k * tt:(k + 1) * tt, :] = (y * jax.nn.sigmoid(gate)).astype(BF16)

    units = [(k, j) for k in range(S5_STEP // tt) for j in range(nq)]
    bu_next = in_proj(*units[0])
    prev = None
    ys = []
    for n, (k, j) in enumerate(units):
        sl = slice(j * sw, (j + 1) * sw)
        bu = bu_next
        if n + 1 < len(units):
            bu_next = in_proj(*units[n + 1])
        bur, bui = bu[:, :sw], bu[:, sw:]
        air, aii = air_ref[:, sl], aii_ref[:, sl]
        zr = (air * bur - aii * bui).astype(BF16)
        zi = (air * bui + aii * bur).astype(BF16)
        cr = _dot(tri, zr) + hr_ref[:, sl]
        ci = _dot(tri, zi) + hi_ref[:, sl]
        if prev is not None:
            ys.append(_dot(prev[2], cw_ref[prev[1]]))
            if prev[1] == nq - 1:
                finish(prev[0], ys)
                ys = []
        apr, api = apr_ref[:, sl], api_ref[:, sl]
        xr = apr * cr - api * ci
        xi = apr * ci + api * cr
        lr, li = xr[tt - 1:, :], xi[tt - 1:, :]
        ahr, ahi = ah_ref[0:1, sl], ah_ref[1:2, sl]
        hr_ref[:, sl] = ahr * lr - ahi * li
        hi_ref[:, sl] = ahr * li + ahi * lr
        prev = (k, j, jnp.concatenate([xr, xi], axis=1).astype(BF16))
    ys.append(_dot(prev[2], cw_ref[prev[1]]))
    finish(prev[0], ys)


def _s5_tables(a_re, a_im, log_step, b_re, b_im, c_re, c_im):
    g, p, c = S5_GROUPS, S5_STATE, S5_GROUP
    step = jnp.exp(log_step)[:, None]
    mag = jnp.exp(a_re * step)
    abar_r, abar_i = mag * jnp.cos(a_im * step), mag * jnp.sin(a_im * step)
    den = a_re * a_re + a_im * a_im
    nr, ni = abar_r - 1.0, abar_i
    coef_r = (nr * a_re + ni * a_im) / den
    coef_i = (ni * a_re - nr * a_im) / den
    bbar_r = coef_r[..., None] * b_re - coef_i[..., None] * b_im
    bbar_i = coef_r[..., None] * b_im + coef_i[..., None] * b_re
    nq = S5_WIDTH // LANES
    gq = g // nq
    eye = jnp.eye(gq, dtype=F32)

    def in_slab(bb):
        bb = bb.reshape(nq, gq, p, c)
        return jnp.einsum('qgpc,gh->qgchp', bb, eye).reshape(nq, gq * c, gq * p)

    bw = jnp.concatenate([in_slab(bbar_r), in_slab(bbar_i)], axis=2).astype(BF16)

    def out_slab(cc):
        cc = cc.reshape(nq, gq, c, p)
        return jnp.einsum('qgcp,gh->qgphc', cc, eye).reshape(nq, gq * p, gq * c)

    cw = jnp.concatenate([out_slab(c_re), -out_slab(c_im)], axis=1).astype(BF16)
    half = S5_TILE // 2
    tt = jnp.arange(1 - half, S5_TILE + 1 - half, dtype=F32)[:, None]
    la = (a_re * step).reshape(1, g * p)
    th = (a_im * step).reshape(1, g * p)
    pm, ang = jnp.exp(tt * la), tt * th
    im_ = jnp.exp(-tt * la)
    apr, api = pm * jnp.cos(ang), pm * jnp.sin(ang)
    air, aii = im_ * jnp.cos(ang), -im_ * jnp.sin(ang)
    hm = jnp.exp(half * la)
    ah = jnp.concatenate([hm * jnp.cos(half * th), hm * jnp.sin(half * th)], axis=0)
    return bw, cw, air, aii, apr, api, ah


def _s5(su, tables, d_skip, glu_w, glu_b, batch, seq):
    bw, cw, air, aii, apr, api, ah = tables
    h3 = su.reshape(batch, seq, S5_WIDTH)
    tri = jnp.tril(jnp.ones((S5_TILE, S5_TILE), F32)).astype(BF16)
    tab = (S5_TILE, S5_NSTATE)
    return pl.pallas_call(
        _s5_kernel,
        grid=(batch, seq // S5_STEP),
        in_specs=[
            pl.BlockSpec((None, S5_STEP, S5_WIDTH), lambda b, i: (b, i, 0)),
            _resident(bw.shape), _resident(cw.shape),
            _resident(tab), _resident(tab), _resident(tab), _resident(tab),
            _resident((2, S5_NSTATE)),
            _resident((S5_TILE, S5_TILE)),
            _resident((1, S5_WIDTH)),
            _resident((S5_WIDTH, S5_WIDTH)),
            _resident((1, S5_WIDTH)),
        ],
        out_specs=pl.BlockSpec((None, S5_STEP, S5_WIDTH), lambda b, i: (b, i, 0)),
        out_shape=jax.ShapeDtypeStruct((batch, seq, S5_WIDTH), BF16),
        scratch_shapes=[pltpu.VMEM((1, S5_NSTATE), F32), pltpu.VMEM((1, S5_NSTATE), F32)],
        compiler_params=_params("parallel", "arbitrary"),
        name="s5_ssm",
    )(h3, bw, cw, air, aii, apr, api, ah, tri, d_skip, glu_w, glu_b).reshape(batch * seq, S5_WIDTH)


def _merge_kernel(x_ref, oa_ref, ob_ref, oc_ref, od_ref, wg_ref, pa_ref, pb_ref, pc_ref, pd_ref,
                  wo_ref, g_ref, b_ref, o_ref):
    half = ROW_TILE // 2
    for r in range(2):
        rows = slice(r * half, (r + 1) * half)
        x = x_ref[rows, :]
        xb = x.astype(BF16)
        merged = jnp.zeros((half, D_MODEL), F32)
        for i, (br_ref, pr_ref) in enumerate(((oa_ref, pa_ref), (ob_ref, pb_ref), (oc_ref, pc_ref), (od_ref, pd_ref))):
            gl = _dot(xb, wg_ref[:, i * D_MODEL:(i + 1) * D_MODEL])
            merged = merged + jax.nn.sigmoid(gl) * _dot(br_ref[rows, :], pr_ref[...])
        m = _dot(merged.astype(BF16), wo_ref[...])
        o_ref[rows, :] = _layer_norm(DEEPNORM_ALPHA * x + m, g_ref[...], b_ref[...])


def _merge(x, oa, ob, oc, od, w_gate, pa, pb, pc, pd, w_out, g, b):
    n = x.shape[0]
    row = lambda w: pl.BlockSpec((ROW_TILE, w), lambda i: (i, 0))
    return pl.pallas_call(
        _merge_kernel,
        grid=(n // ROW_TILE,),
        in_specs=[
            row(D_MODEL), row(A_Q), row(S5_WIDTH), row(CONV_WIDTH), row(D_V),
            _resident((D_MODEL, GATE_COLS)),
            _resident((A_Q, D_MODEL)), _resident((S5_WIDTH, D_MODEL)),
            _resident((CONV_WIDTH, D_MODEL)), _resident((D_V, D_MODEL)),
            _resident((D_MODEL, D_MODEL)),
            _resident((1, D_MODEL)), _resident((1, D_MODEL)),
        ],
        out_specs=row(D_MODEL),
        out_shape=jax.ShapeDtypeStruct((n, D_MODEL), F32),
        compiler_params=_params("parallel"),
        name="gated_merge",
    )(x, oa, ob, oc, od, w_gate, pa, pb, pc, pd, w_out, g, b)


def _mem_kv_kernel(m_ref, g_ref, b_ref, w_ref, o_ref):
    mn = _layer_norm(m_ref[...], g_ref[...], b_ref[...])
    o_ref[...] = _dot(mn.astype(BF16), w_ref[...]).astype(BF16)


def _mem_kv(mem, g, b, wkv):
    n = mem.shape[0]
    return pl.pallas_call(
        _mem_kv_kernel,
        grid=(DEPTH, n // ROW_TILE),
        in_specs=[
            pl.BlockSpec((ROW_TILE, D_MODEL), lambda l, i: (i, 0)),
            _resident((1, D_MODEL)), _resident((1, D_MODEL)),
            pl.BlockSpec((None, D_MODEL, 2 * CROSS_WIDTH), lambda l, i: (l, 0, 0)),
        ],
        out_specs=pl.BlockSpec((None, ROW_TILE, 2 * CROSS_WIDTH), lambda l, i: (l, i, 0)),
        out_shape=jax.ShapeDtypeStruct((DEPTH, n, 2 * CROSS_WIDTH), BF16),
        compiler_params=_params("parallel", "parallel"),
        name="mem_kv",
    )(mem, g, b, wkv)


def _cross_kernel(x_ref, kv_ref, wq_ref, wo_ref, g_ref, b_ref, o_ref):
    x = x_ref[...]
    q = _dot(x.astype(BF16), wq_ref[...]).astype(BF16)
    kv = kv_ref[...]
    scale = 1.0 / math.sqrt(CROSS_HEAD_DIM)
    outs = []

    def scores(h):
        sl = slice(h * CROSS_HEAD_DIM, (h + 1) * CROSS_HEAD_DIM)
        return _dot_nt(q[:, sl], kv[:, sl]) * scale

    s_next = scores(0)
    for h in range(CROSS_HEADS):
        s = s_next
        if h + 1 < CROSS_HEADS:
            s_next = scores(h + 1)
        p = jnp.exp(s - jnp.max(s, axis=-1, keepdims=True))
        p = p / jnp.sum(p, axis=-1, keepdims=True)
        outs.append(_dot(p.astype(BF16), kv[:, CROSS_WIDTH + h * CROSS_HEAD_DIM:CROSS_WIDTH + (h + 1) * CROSS_HEAD_DIM]))
    o = jnp.concatenate(outs, axis=1).astype(BF16)
    c = _dot(o, wo_ref[...])
    o_ref[...] = _layer_norm(DEEPNORM_ALPHA * x + c, g_ref[...], b_ref[...])


def _cross_attn(x, kv, wq, wo, g, b, batch, seq):
    x3 = x.reshape(batch, seq, D_MODEL)
    kv3 = kv.reshape(batch, MEM_LEN, 2 * CROSS_WIDTH)
    return pl.pallas_call(
        _cross_kernel,
        grid=(batch, seq // CROSS_TILE),
        in_specs=[
            pl.BlockSpec((None, CROSS_TILE, D_MODEL), lambda b, i: (b, i, 0)),
            pl.BlockSpec((None, MEM_LEN, 2 * CROSS_WIDTH), lambda b, i: (b, 0, 0)),
            _resident((D_MODEL, CROSS_WIDTH)),
            _resident((CROSS_WIDTH, D_MODEL)),
            _resident((1, D_MODEL)), _resident((1, D_MODEL)),
        ],
        out_specs=pl.BlockSpec((None, CROSS_TILE, D_MODEL), lambda b, i: (b, i, 0)),
        out_shape=jax.ShapeDtypeStruct((batch, seq, D_MODEL), F32),
        compiler_params=_params("parallel", "parallel"),
        name="cross_attn",
    )(x3, kv3, wq, wo, g, b).reshape(batch * seq, D_MODEL)


def _row(v):
    return v.reshape(1, -1).astype(F32)


def kernel(x, mem, ffn1_w_in, ffn1_w_out, ffn1_ln_g, ffn1_ln_b, mix_w_in, swa_sinks, swa_proj, s5_a_re, s5_a_im, s5_log_step, s5_b_re, s5_b_im, s5_c_re, s5_c_im, s5_d, s5_glu_w, s5_glu_b, s5_proj, conv_w, conv_b, conv_ln_g, conv_ln_b, conv_proj, diff_lq1, diff_lk1, diff_lq2, diff_lk2, diff_norm_g, diff_proj, mix_w_out, mix_ln_g, mix_ln_b, mem_ln_g, mem_ln_b, cross_wq, cross_wkv, cross_wo, cross_ln_g, cross_ln_b, ffn2_w_in, ffn2_w_out, ffn2_ln_g, ffn2_ln_b):
    batch, seq, _ = x.shape
    n = batch * seq
    assert seq % DIFF_TILE == 0 and seq % ROW_TILE == 0 and seq % CROSS_TILE == 0
    assert seq % S5_STEP == 0 and seq % SWA_TILE == 0 and (batch * MEM_LEN) % ROW_TILE == 0
    h = x.reshape(n, D_MODEL)
    kv_all = _mem_kv(mem.reshape(batch * MEM_LEN, D_MODEL), _row(mem_ln_g), _row(mem_ln_b),
                     cross_wkv.astype(BF16))
    for l in range(DEPTH):
        h = _ffn_ln(h, ffn1_w_in[l].astype(BF16), ffn1_w_out[l].astype(BF16),
                    _row(ffn1_ln_g[l]), _row(ffn1_ln_b[l]))
        lambda_init = 0.8 - 0.6 * math.exp(-0.3 * l)
        w = mix_w_in[l]
        cols = lambda off, width: w[:, off:off + width]
        w_att = jnp.concatenate([cols(OFF_DQ, D_QK), cols(OFF_DK, D_QK), cols(OFF_AK, A_KV)], axis=1).astype(BF16)
        w_q = _swa_q_weight(cols(OFF_AQ, A_Q)).astype(BF16)
        w_v_t = jnp.concatenate([cols(OFF_DV, D_V), cols(OFF_AV, A_KV)], axis=1).T.astype(BF16)
        w_gate = cols(OFF_GL, GATE_COLS).astype(BF16)
        h_att, q_swa, su, v_t, o_c = _mix_proj(
            h, cols(OFF_CU, CONV_IN).astype(BF16), w_att, w_q, cols(OFF_SU, S5_WIDTH).astype(BF16), w_v_t,
            conv_w[l], _row(conv_b[l]), _row(conv_ln_g[l]), _row(conv_ln_b[l]), batch, seq)
        o_a = _swa(h_att, q_swa, v_t, swa_sinks[l], batch, seq)
        tables = _s5_tables(s5_a_re[l], s5_a_im[l], s5_log_step[l], s5_b_re[l], s5_b_im[l], s5_c_re[l], s5_c_im[l])
        o_b = _s5(su, tables, _row(s5_d[l]), s5_glu_w[l].astype(BF16), _row(s5_glu_b[l]), batch, seq)
        lam = (jnp.exp(jnp.sum(diff_lq1[l] * diff_lk1[l])) - jnp.exp(jnp.sum(diff_lq2[l] * diff_lk2[l]))
               + lambda_init).astype(F32)
        o_d = _diff_attn(h_att, v_t, jnp.stack([lam, jnp.float32(1.0 - lambda_init)]), _row(diff_norm_g[l]), batch, seq)
        h = _merge(h, o_a, o_b, o_c, o_d, w_gate, swa_proj[l].astype(BF16), s5_proj[l].astype(BF16),
                   conv_proj[l].astype(BF16), diff_proj[l].astype(BF16), mix_w_out[l].astype(BF16),
                   _row(mix_ln_g[l]), _row(mix_ln_b[l]))
        h = _cross_attn(h, kv_all[l], cross_wq[l].astype(BF16), cross_wo[l].astype(BF16),
                        _row(cross_ln_g[l]), _row(cross_ln_b[l]), batch, seq)
        h = _ffn_ln(h, ffn2_w_in[l].astype(BF16), ffn2_w_out[l].astype(BF16),
                    _row(ffn2_ln_g[l]), _row(ffn2_ln_b[l]))
    return h.reshape(batch, seq, D_MODEL)
```

```python
import functools
import math

import jax
import jax.numpy as jnp
from jax import lax
from jax.experimental import pallas as pl
from jax.experimental.pallas import tpu as pltpu

F32 = jnp.float32
BF16 = jnp.bfloat16

D_MODEL = 1024
DEPTH = 4
MEM_LEN = 256
HEAD_DIM = 64
BLOCK = 128
SWA_HEADS = 8
SWA_KV_HEADS = 2
SWA_REP = SWA_HEADS // SWA_KV_HEADS
S5_WIDTH = 512
S5_GROUP = 16
S5_GROUPS = S5_WIDTH // S5_GROUP
S5_STATE = 64
S5_NSTATE = S5_GROUPS * S5_STATE
CONV_WIDTH = 512
CONV_K = 31
DIFF_HEADS = 4
DIFF_V_DIM = 2 * HEAD_DIM
CROSS_HEADS = 4
CROSS_HEAD_DIM = 128
CROSS_WIDTH = CROSS_HEADS * CROSS_HEAD_DIM
FFN_DIM = 2816
N_BRANCHES = 4

A_Q = SWA_HEADS * HEAD_DIM
A_KV = SWA_KV_HEADS * HEAD_DIM
D_QK = DIFF_HEADS * 2 * HEAD_DIM
D_V = DIFF_HEADS * DIFF_V_DIM
CONV_IN = 2 * CONV_WIDTH
GATE_COLS = N_BRANCHES * D_MODEL
OFF_AQ = 0
OFF_AK = OFF_AQ + A_Q
OFF_AV = OFF_AK + A_KV
OFF_DQ = OFF_AV + A_KV
OFF_DK = OFF_DQ + D_QK
OFF_DV = OFF_DK + D_QK
OFF_SU = OFF_DV + D_V
OFF_CU = OFF_SU + S5_WIDTH
OFF_GL = OFF_CU + CONV_IN
ATT_COLS = 2 * D_QK + A_KV
VT_ROWS = D_V + A_KV

DEEPNORM_ALPHA = (2.0 * DEPTH) ** 0.25
LN_EPS = 1e-5
NEG_INF = -1e30

LANES = 128
SUBLANES = 8
MXU_WIDTH = 256
VMEM_LIMIT = 56 * 1024 * 1024

ROW_TILE = 512
FFN_CHUNK = MXU_WIDTH
FFN_ROWS = 1024
FFN_PART = 512
DIFF_TILE = 512
ONES_ROWS = 16
DIFF_VROWS = DIFF_V_DIM + ONES_ROWS
SWA_TILE = 512
LOG2E = math.log2(math.e)
CONV_HALO = 32
S5_TILE = 256
S5_STEP = 512
CROSS_TILE = 512


def _alibi_slope(h, n):
    return 2.0 ** (-8.0 * (h + 1) / n)


def _layer_norm(z, g, b):
    mu = jnp.mean(z, axis=-1, keepdims=True)
    zc = z - mu
    var = jnp.mean(zc * zc, axis=-1, keepdims=True)
    return zc * lax.rsqrt(var + LN_EPS) * g + b


def _dot(a, b):
    return jnp.dot(a, b, preferred_element_type=F32)


def _dot_nt(a, b):
    return lax.dot_general(a, b, (((1,), (1,)), ((), ())), preferred_element_type=F32)


def _params(*sem):
    return pltpu.CompilerParams(dimension_semantics=sem, vmem_limit_bytes=VMEM_LIMIT)


def _resident(shape):
    nd = len(shape)
    return pl.BlockSpec(shape, lambda *_: (0,) * nd, pipeline_mode=pl.Buffered(1))


def _ffn_ln_kernel(x_ref, wg_ref, wu_ref, wo_ref, g_ref, b_ref, o_ref, a_ref):
    half = FFN_PART
    for r in range(FFN_ROWS // FFN_PART):
        rows = slice(r * half, (r + 1) * half)
        x = x_ref[rows, :]
        xb = x.astype(BF16)
        for c in range(FFN_DIM // FFN_CHUNK):
            sl = slice(c * FFN_CHUNK, (c + 1) * FFN_CHUNK)
            gate = _dot(xb, wg_ref[:, sl])
            up = _dot(xb, wu_ref[:, sl])
            a_ref[rows, sl] = (gate * jax.nn.sigmoid(gate) * up).astype(BF16)
        f = _dot(a_ref[rows, :], wo_ref[...])
        o_ref[rows, :] = _layer_norm(DEEPNORM_ALPHA * x + 0.5 * f, g_ref[...], b_ref[...])


def _ffn_ln(x, w_in, w_out, g, b):
    n = x.shape[0]
    return pl.pallas_call(
        _ffn_ln_kernel,
        grid=(n // FFN_ROWS,),
        in_specs=[
            pl.BlockSpec((FFN_ROWS, D_MODEL), lambda i: (i, 0)),
            pl.BlockSpec((D_MODEL, FFN_DIM), lambda i: (0, 0), pipeline_mode=pl.Buffered(1)),
            pl.BlockSpec((D_MODEL, FFN_DIM), lambda i: (0, 1), pipeline_mode=pl.Buffered(1)),
            _resident((FFN_DIM, D_MODEL)),
            _resident((1, D_MODEL)),
            _resident((1, D_MODEL)),
        ],
        out_specs=pl.BlockSpec((FFN_ROWS, D_MODEL), lambda i: (i, 0)),
        out_shape=jax.ShapeDtypeStruct((n, D_MODEL), F32),
        scratch_shapes=[pltpu.VMEM((FFN_ROWS, FFN_DIM), BF16)],
        compiler_params=_params("parallel"),
        name="ffn_ln",
    )(x, w_in, w_in, w_out, g, b)


def _mix_proj_kernel(x_ref, wc_ref, wa_ref, wq_ref, ws_ref, wvt_ref, cw_ref, cb_ref, cg_ref, cbeta_ref,
                     qk_ref, ak_ref, q_ref, su_ref, vt_ref, oc_ref, buf_ref, *, per_seq):
    i = pl.program_id(0)

    @pl.when(i == 0)
    def _():
        buf_ref[ROW_TILE:, :] = jnp.zeros((CONV_HALO, CONV_WIDTH), F32)

    xb = x_ref[...].astype(BF16)
    cu = _dot(xb, wc_ref[...])
    ha = _dot(xb, wa_ref[...])
    for j in range(2 * DIFF_HEADS):
        qk_ref[j] = ha[:, j * LANES:(j + 1) * LANES].astype(BF16)
    ak_ref[...] = ha[:, 2 * D_QK:].astype(BF16)
    q = _dot(xb, wq_ref[...])
    for h in range(SWA_HEADS):
        q_ref[h] = q[:, h * LANES:(h + 1) * LANES].astype(BF16)
    su_ref[...] = _dot(xb, ws_ref[...])
    vt_ref[...] = _dot_nt(wvt_ref[...], xb).astype(BF16)
    keep = (i % per_seq != 0).astype(F32)
    buf_ref[0:CONV_HALO, :] = buf_ref[ROW_TILE:, :] * keep
    buf_ref[CONV_HALO:, :] = cu[:, :CONV_WIDTH] * jax.nn.sigmoid(cu[:, CONV_WIDTH:])
    bufv = buf_ref[...]
    nrows = ROW_TILE + CONV_HALO
    shifted = [bufv] + [pltpu.roll(bufv, nrows - b, axis=0) for b in range(1, SUBLANES)]
    first = CONV_HALO - (CONV_K - 1)
    acc = jnp.zeros((ROW_TILE, CONV_WIDTH), F32)
    for k in range(CONV_K):
        a, b = divmod(first + k, SUBLANES)
        lo = a * SUBLANES
        acc = acc + cw_ref[k:k + 1, :] * shifted[b][lo:lo + ROW_TILE, :]
    y = _layer_norm(acc + cb_ref[...], cg_ref[...], cbeta_ref[...])
    oc_ref[...] = (y * jax.nn.sigmoid(y)).astype(BF16)


def _mix_proj(x, w_cu, w_att, w_q, w_su, w_v_t, conv_w, conv_b, conv_g, conv_beta, batch, seq):
    n = x.shape[0]
    per_seq = seq // ROW_TILE
    row = lambda w: pl.BlockSpec((ROW_TILE, w), lambda i: (i, 0))
    return pl.pallas_call(
        functools.partial(_mix_proj_kernel, per_seq=per_seq),
        grid=(n // ROW_TILE,),
        in_specs=[
            row(D_MODEL),
            _resident((D_MODEL, CONV_IN)),
            _resident((D_MODEL, ATT_COLS)),
            _resident((D_MODEL, SWA_HEADS * LANES)),
            _resident((D_MODEL, S5_WIDTH)),
            _resident((VT_ROWS, D_MODEL)),
            _resident((CONV_K, CONV_WIDTH)),
            _resident((1, CONV_WIDTH)), _resident((1, CONV_WIDTH)), _resident((1, CONV_WIDTH)),
        ],
        out_specs=[
            pl.BlockSpec((2 * DIFF_HEADS, ROW_TILE, LANES), lambda i: (0, i, 0)),
            row(A_KV),
            pl.BlockSpec((SWA_HEADS, ROW_TILE, LANES), lambda i: (0, i, 0)),
            row(S5_WIDTH),
            pl.BlockSpec((None, VT_ROWS, ROW_TILE), lambda i: (i // per_seq, 0, i % per_seq)),
            row(CONV_WIDTH),
        ],
        out_shape=[
            jax.ShapeDtypeStruct((2 * DIFF_HEADS, n, LANES), BF16),
            jax.ShapeDtypeStruct((n, A_KV), BF16),
            jax.ShapeDtypeStruct((SWA_HEADS, n, LANES), BF16),
            jax.ShapeDtypeStruct((n, S5_WIDTH), F32),
            jax.ShapeDtypeStruct((batch, VT_ROWS, seq), BF16),
            jax.ShapeDtypeStruct((n, CONV_WIDTH), BF16),
        ],
        scratch_shapes=[pltpu.VMEM((ROW_TILE + CONV_HALO, CONV_WIDTH), F32)],
        compiler_params=_params("arbitrary"),
        name="mix_proj_conv",
    )(x, w_cu, w_att, w_q, w_su, w_v_t, conv_w, conv_b, conv_g, conv_beta)


def _swa_q_weight(w_aq):
    d = w_aq.shape[0]
    w = (w_aq * (LOG2E / math.sqrt(HEAD_DIM))).reshape(d, SWA_KV_HEADS, SWA_REP, 1, HEAD_DIM)
    sel = jnp.eye(SWA_KV_HEADS, dtype=w.dtype).reshape(1, SWA_KV_HEADS, 1, SWA_KV_HEADS, 1)
    return (w * sel).reshape(d, SWA_HEADS * LANES)


def _swa_kernel(sink_ref, bias_ref, q_ref, kc_ref, kp_ref, vc_ref, vp_ref, o_ref):
    i = pl.program_id(1)
    gw = SWA_REP * BLOCK
    k_all = jnp.concatenate([kp_ref[...], kc_ref[...]], axis=0)
    v_all = jnp.concatenate([vp_ref[...], vc_ref[...]], axis=1)
    vx_all = jnp.concatenate([v_all, jnp.ones((ONES_ROWS, v_all.shape[1]), BF16)], axis=0)
    no_prev = jnp.where(i == 0, NEG_INF, 0.0)

    def scores(blk, g):
        lo = blk * BLOCK
        qg = q_ref[g * SWA_REP:(g + 1) * SWA_REP, lo:lo + BLOCK, :].reshape(gw, LANES)
        s = _dot_nt(k_all[lo:lo + 2 * BLOCK], qg) + bias_ref[g]
        if blk == 0:
            s = jnp.concatenate([s[:BLOCK] + no_prev, s[BLOCK:]], axis=0)
        return s

    units = [(blk, g) for blk in range(SWA_TILE // BLOCK) for g in range(SWA_KV_HEADS)]
    s_next = scores(*units[0])
    outs = []
    for u, (blk, g) in enumerate(units):
        s = s_next
        if u + 1 < len(units):
            s_next = scores(*units[u + 1])
        lo = blk * BLOCK
        sink = sink_ref[g]
        m = jnp.maximum(jnp.max(s, axis=0, keepdims=True), sink)
        e = jnp.exp2(s - m).astype(BF16)
        pv = _dot(vx_all[:, lo:lo + 2 * BLOCK], e)
        den = pv[LANES:LANES + 1] + jnp.exp2(sink - m)
        og = pv[g * HEAD_DIM:(g + 1) * HEAD_DIM] * (1.0 / den)
        outs += [og[:, r * BLOCK:(r + 1) * BLOCK] for r in range(SWA_REP)]
        if g == SWA_KV_HEADS - 1:
            o_ref[lo:lo + BLOCK, :] = jnp.concatenate(outs, axis=0).T.astype(BF16)
            outs = []


def _swa_bias():
    kj = jnp.arange(2 * BLOCK)[:, None]
    qi = jnp.arange(BLOCK)[None, :]
    dist = BLOCK + qi - kj
    valid = (dist >= 0) & (dist < BLOCK)
    slopes = jnp.asarray([_alibi_slope(h, SWA_HEADS) * LOG2E for h in range(SWA_HEADS)], F32)
    b = jnp.where(valid[None], -slopes[:, None, None] * dist[None].astype(F32), NEG_INF)
    b = b.reshape(SWA_KV_HEADS, SWA_REP, 2 * BLOCK, BLOCK)
    return jnp.transpose(b, (0, 2, 1, 3)).reshape(SWA_KV_HEADS, 2 * BLOCK, SWA_REP * BLOCK)


def _swa(ak, q_swa, v_t, sinks, batch, seq):
    h3 = ak.reshape(batch, seq, A_KV)
    rv = D_V // A_KV
    per_seq = seq // SWA_TILE
    r = SWA_TILE // BLOCK
    gw = SWA_REP * BLOCK
    sink_rows = jnp.repeat(sinks.astype(F32).reshape(SWA_KV_HEADS, SWA_REP) * LOG2E, BLOCK, axis=1)
    prev = lambda i: jnp.maximum(i * r - 1, 0)
    return pl.pallas_call(
        _swa_kernel,
        grid=(batch, per_seq),
        in_specs=[
            _resident((SWA_KV_HEADS, 1, gw)),
            _resident((SWA_KV_HEADS, 2 * BLOCK, gw)),
            pl.BlockSpec((SWA_HEADS, SWA_TILE, LANES), lambda b, i: (0, b * per_seq + i, 0)),
            pl.BlockSpec((None, SWA_TILE, A_KV), lambda b, i: (b, i, 0)),
            pl.BlockSpec((None, BLOCK, A_KV), lambda b, i: (b, prev(i), 0)),
            pl.BlockSpec((None, A_KV, SWA_TILE), lambda b, i: (b, rv, i)),
            pl.BlockSpec((None, A_KV, BLOCK), lambda b, i: (b, rv, prev(i))),
        ],
        out_specs=pl.BlockSpec((None, SWA_TILE, A_Q), lambda b, i: (b, i, 0)),
        out_shape=jax.ShapeDtypeStruct((batch, seq, A_Q), BF16),
        compiler_params=_params("parallel", "parallel"),
        name="swa",
    )(sink_rows.reshape(SWA_KV_HEADS, 1, gw), _swa_bias(), q_swa, h3, h3, v_t, v_t).reshape(batch * seq, A_Q)


def _diff_kernel(sc_ref, q_ref, k_ref, vt_ref, g_ref, o_ref,
                 qs_ref, vx_ref, b_ref, s_ref, mx_ref, m_ref, acc_ref, *, n_tiles):
    h = pl.program_id(1)
    t = DIFF_TILE
    hf = t // 2
    v_dim = DIFF_V_DIM
    slope = jnp.float32(_alibi_slope(DIFF_HEADS - 1, DIFF_HEADS) * LOG2E)
    for hh in range(DIFF_HEADS - 1):
        slope = jnp.where(h == hh, jnp.float32(_alibi_slope(hh, DIFF_HEADS) * LOG2E), slope)

    vx_ref[:v_dim, :] = vt_ref[...]
    vx_ref[v_dim:, :] = jnp.ones((DIFF_VROWS - v_dim, vx_ref.shape[1]), BF16)
    krow = lax.broadcasted_iota(jnp.int32, (t, t), 0)
    qcol = lax.broadcasted_iota(jnp.int32, (t, t), 1)
    b_ref[0] = slope * krow.astype(F32)
    b_ref[1] = jnp.where(qcol >= krow, slope * krow.astype(F32), NEG_INF)
    lane = lax.broadcasted_iota(jnp.int32, (t, 2 * HEAD_DIM), 1)
    for qt in range(n_tiles):
        q = q_ref[qt * t:(qt + 1) * t, :].astype(F32) * (LOG2E / math.sqrt(HEAD_DIM))
        qs_ref[qt, 0] = jnp.where(lane < HEAD_DIM, q, 0.0).astype(BF16)
        qs_ref[qt, 1] = jnp.where(lane >= HEAD_DIM, q, 0.0).astype(BF16)

    def items_of(qt):
        out = []
        for j in range(qt):
            out += [(qt, j * t, t, 0, t, None, c) for c in range(2)]
        out += [(qt, qt * t, hf, 0, t, (0, 0), c) for c in range(2)]
        out += [(qt, qt * t + hf, hf, hf, hf, (hf, hf), c) for c in range(2)]
        return out

    def scores(item):
        qt, r0, nr, c0, nc, tri, c = item
        bias = b_ref[0, 0:nr, 0:nc] if tri is None else b_ref[1, tri[0]:tri[0] + nr, tri[1]:tri[1] + nc]
        s = _dot_nt(k_ref[r0:r0 + nr, :], qs_ref[qt, c, c0:c0 + nc, :]) + bias
        s_ref[c, 0:nr, 0:nc] = s
        mx_ref[c, :, 0:nc] = jnp.max(s, axis=0, keepdims=True)

    def accumulate(item, first):
        qt, r0, nr, c0, nc, tri, c = item
        mx = mx_ref[c, :, 0:nc]
        e = jnp.exp2(s_ref[c, 0:nr, 0:nc] - mx).astype(BF16)
        pv = _dot(vx_ref[:, r0:r0 + nr], e)
        mxo = mx + slope * float(r0 - (0 if tri is None else tri[0]))
        if first:
            m_ref[c, :, c0:c0 + nc] = mxo
            acc_ref[c, :, c0:c0 + nc] = pv
        else:
            m_old = m_ref[c, :, c0:c0 + nc]
            m_new = jnp.maximum(m_old, mxo)
            m_ref[c, :, c0:c0 + nc] = m_new
            acc_ref[c, :, c0:c0 + nc] = (jnp.exp2(m_old - m_new) * acc_ref[c, :, c0:c0 + nc]
                                         + jnp.exp2(mxo - m_new) * pv)

    def finalize(qt):
        def normalised(c):
            a = acc_ref[c]
            return a[:v_dim] * (1.0 / a[v_dim:v_dim + 1])

        o = (normalised(0) - sc_ref[0] * normalised(1)).T
        o = o * lax.rsqrt(jnp.mean(o * o, axis=-1, keepdims=True) + LN_EPS) * g_ref[...] * sc_ref[1]
        o_ref[qt * t:(qt + 1) * t, :] = o.astype(BF16)

    items = [it for qt in range(n_tiles) for it in items_of(qt)]
    scores(items[0])
    for n, item in enumerate(items):
        if n + 1 < len(items):
            scores(items[n + 1])
        qt = item[0]
        accumulate(item, first=(n < 2 or items[n - 2][0] != qt))
        if n + 1 == len(items) or items[n + 1][0] != qt:
            finalize(qt)


def _diff_attn(qk, v_t, scalars, norm_g, batch, seq):
    h3 = qk.reshape(2 * DIFF_HEADS, batch, seq, LANES)
    w = 2 * HEAD_DIM
    t = DIFF_TILE
    n_tiles = seq // t
    return pl.pallas_call(
        functools.partial(_diff_kernel, n_tiles=n_tiles),
        grid=(batch, DIFF_HEADS),
        in_specs=[
            pl.BlockSpec(memory_space=pltpu.SMEM),
            pl.BlockSpec((None, None, seq, w), lambda b, h: (h, b, 0, 0)),
            pl.BlockSpec((None, None, seq, w), lambda b, h: (DIFF_HEADS + h, b, 0, 0)),
            pl.BlockSpec((None, DIFF_V_DIM, seq), lambda b, h: (b, h, 0)),
            _resident((1, DIFF_V_DIM)),
        ],
        out_specs=pl.BlockSpec((None, None, seq, w), lambda b, h: (h, b, 0, 0)),
        out_shape=jax.ShapeDtypeStruct((DIFF_HEADS, batch, seq, w), BF16),
        scratch_shapes=[
            pltpu.VMEM((n_tiles, 2, t, w), BF16),
            pltpu.VMEM((DIFF_VROWS, seq), BF16),
            pltpu.VMEM((2, t, t), F32),
            pltpu.VMEM((2, t, t), F32),
            pltpu.VMEM((2, 1, t), F32),
            pltpu.VMEM((2, 1, t), F32),
            pltpu.VMEM((2, DIFF_VROWS, t), F32),
        ],
        compiler_params=_params("parallel", "parallel"),
        name="diff_attn",
    )(scalars, h3, h3, v_t, norm_g).reshape(DIFF_HEADS, batch * seq, w)


def _s5_kernel(u_ref, bw_ref, cw_ref, air_ref, aii_ref, apr_ref, api_ref, ah_ref, tri_ref,
               d_ref, gw_ref, gb_ref, o_ref, hr_ref, hi_ref):
    i = pl.program_id(1)

    @pl.when(i == 0)
    def _():
        hr_ref[...] = jnp.zeros_like(hr_ref)
        hi_ref[...] = jnp.zeros_like(hi_ref)

    tri = tri_ref[...]
    nq = S5_WIDTH // LANES
    sw = S5_NSTATE // nq
    tt = S5_TILE

    def in_proj(k, j):
        ub = u_ref[k * tt:(k + 1) * tt, j * LANES:(j + 1) * LANES].astype(BF16)
        return _dot(ub, bw_ref[j])

    def finish(k, ys):
        u = u_ref[k * tt:(k + 1) * tt, :]
        y = jnp.concatenate(ys, axis=1) + d_ref[...] * u
        y = jax.nn.gelu(y, approximate=True)
        gate = _dot(y.astype(BF16), gw_ref[...]) + gb_ref[...]
        o_ref[k * tt:(k + 1) * tt, :] = (y * jax.nn.sigmoid(gate)).astype(BF16)

    units = [(k, j) for k in range(S5_STEP // tt) for j in range(nq)]
    bu_next = in_proj(*units[0])
    prev = None
    ys = []
    for n, (k, j) in enumerate(units):
        sl = slice(j * sw, (j + 1) * sw)
        bu = bu_next
        if n + 1 < len(units):
            bu_next = in_proj(*units[n + 1])
        bur, bui = bu[:, :sw], bu[:, sw:]
        air, aii = air_ref[:, sl], aii_ref[:, sl]
        zr = (air * bur - aii * bui).astype(BF16)
        zi = (air * bui + aii * bur).astype(BF16)
        cr = _dot(tri, zr) + hr_ref[:, sl]
        ci = _dot(tri, zi) + hi_ref[:, sl]
        if prev is not None:
            ys.append(_dot(prev[2], cw_ref[prev[1]]))
            if prev[1] == nq - 1:
                finish(prev[0], ys)
                ys = []
        apr, api = apr_ref[:, sl], api_ref[:, sl]
        xr = apr * cr - api * ci
        xi = apr * ci + api * cr
        lr, li = xr[tt - 1:, :], xi[tt - 1:, :]
        ahr, ahi = ah_ref[0:1, sl], ah_ref[1:2, sl]
        hr_ref[:, sl] = ahr * lr - ahi * li
        hi_ref[:, sl] = ahr * li + ahi * lr
        prev = (k, j, jnp.concatenate([xr, xi], axis=1).astype(BF16))
    ys.append(_dot(prev[2], cw_ref[prev[1]]))
    finish(prev[0], ys)


def _s5_tables(a_re, a_im, log_step, b_re, b_im, c_re, c_im):
    g, p, c = S5_GROUPS, S5_STATE, S5_GROUP
    step = jnp.exp(log_step)[:, None]
    mag = jnp.exp(a_re * step)
    abar_r, abar_i = mag * jnp.cos(a_im * step), mag * jnp.sin(a_im * step)
    den = a_re * a_re + a_im * a_im
    nr, ni = abar_r - 1.0, abar_i
    coef_r = (nr * a_re + ni * a_im) / den
    coef_i = (ni * a_re - nr * a_im) / den
    bbar_r = coef_r[..., None] * b_re - coef_i[..., None] * b_im
    bbar_i = coef_r[..., None] * b_im + coef_i[..., None] * b_re
    nq = S5_WIDTH // LANES
    gq = g // nq
    eye = jnp.eye(gq, dtype=F32)

    def in_slab(bb):
        bb = bb.reshape(nq, gq, p, c)
        return jnp.einsum('qgpc,gh->qgchp', bb, eye).reshape(nq, gq * c, gq * p)

    bw = jnp.concatenate([in_slab(bbar_r), in_slab(bbar_i)], axis=2).astype(BF16)

    def out_slab(cc):
        cc = cc.reshape(nq, gq, c, p)
        return jnp.einsum('qgcp,gh->qgphc', cc, eye).reshape(nq, gq * p, gq * c)

    cw = jnp.concatenate([out_slab(c_re), -out_slab(c_im)], axis=1).astype(BF16)
    half = S5_TILE // 2
    tt = jnp.arange(1 - half, S5_TILE + 1 - half, dtype=F32)[:, None]
    la = (a_re * step).reshape(1, g * p)
    th = (a_im * step).reshape(1, g * p)
    pm, ang = jnp.exp(tt * la), tt * th
    im_ = jnp.exp(-tt * la)
    apr, api = pm * jnp.cos(ang), pm * jnp.sin(ang)
    air, aii = im_ * jnp.cos(ang), -im_ * jnp.sin(ang)
    hm = jnp.exp(half * la)
    ah = jnp.concatenate([hm * jnp.cos(half * th), hm * jnp.sin(half * th)], axis=0)
    return bw, cw, air, aii, apr, api, ah


def _s5(su, tables, d_skip, glu_w, glu_b, batch, seq):
    bw, cw, air, aii, apr, api, ah = tables
    h3 = su.reshape(batch, seq, S5_WIDTH)
    tri = jnp.tril(jnp.ones((S5_TILE, S5_TILE), F32)).astype(BF16)
    tab = (S5_TILE, S5_NSTATE)
    return pl.pallas_call(
        _s5_kernel,
        grid=(batch, seq // S5_STEP),
        in_specs=[
            pl.BlockSpec((None, S5_STEP, S5_WIDTH), lambda b, i: (b, i, 0)),
            _resident(bw.shape), _resident(cw.shape),
            _resident(tab), _resident(tab), _resident(tab), _resident(tab),
            _resident((2, S5_NSTATE)),
            _resident((S5_TILE, S5_TILE)),
            _resident((1, S5_WIDTH)),
            _resident((S5_WIDTH, S5_WIDTH)),
            _resident((1, S5_WIDTH)),
        ],
        out_specs=pl.BlockSpec((None, S5_STEP, S5_WIDTH), lambda b, i: (b, i, 0)),
        out_shape=jax.ShapeDtypeStruct((batch, seq, S5_WIDTH), BF16),
        scratch_shapes=[pltpu.VMEM((1, S5_NSTATE), F32), pltpu.VMEM((1, S5_NSTATE), F32)],
        compiler_params=_params("parallel", "arbitrary"),
        name="s5_ssm",
    )(h3, bw, cw, air, aii, apr, api, ah, tri, d_skip, glu_w, glu_b).reshape(batch * seq, S5_WIDTH)


def _merge_kernel(x_ref, oa_ref, ob_ref, oc_ref, od_ref, wg_ref, pa_ref, pb_ref, pc_ref, pd_ref,
                  wo_ref, g_ref, b_ref, o_ref):
    half = ROW_TILE // 2
    for r in range(2):
        rows = slice(r * half, (r + 1) * half)
        x = x_ref[rows, :]
        xb = x.astype(BF16)
        merged = jnp.zeros((half, D_MODEL), F32)
        od = jnp.concatenate([od_ref[hd, rows, :] for hd in range(DIFF_HEADS)], axis=1)
        branches = ((oa_ref[rows, :], pa_ref), (ob_ref[rows, :], pb_ref), (oc_ref[rows, :], pc_ref), (od, pd_ref))
        for i, (br, pr_ref) in enumerate(branches):
            gl = _dot(xb, wg_ref[:, i * D_MODEL:(i + 1) * D_MODEL])
            merged = merged + jax.nn.sigmoid(gl) * _dot(br, pr_ref[...])
        m = _dot(merged.astype(BF16), wo_ref[...])
        o_ref[rows, :] = _layer_norm(DEEPNORM_ALPHA * x + m, g_ref[...], b_ref[...])


def _merge(x, oa, ob, oc, od, w_gate, pa, pb, pc, pd, w_out, g, b):
    n = x.shape[0]
    row = lambda w: pl.BlockSpec((ROW_TILE, w), lambda i: (i, 0))
    return pl.pallas_call(
        _merge_kernel,
        grid=(n // ROW_TILE,),
        in_specs=[
            row(D_MODEL), row(A_Q), row(S5_WIDTH), row(CONV_WIDTH),
            pl.BlockSpec((DIFF_HEADS, ROW_TILE, DIFF_V_DIM), lambda i: (0, i, 0)),
            _resident((D_MODEL, GATE_COLS)),
            _resident((A_Q, D_MODEL)), _resident((S5_WIDTH, D_MODEL)),
            _resident((CONV_WIDTH, D_MODEL)), _resident((D_V, D_MODEL)),
            _resident((D_MODEL, D_MODEL)),
            _resident((1, D_MODEL)), _resident((1, D_MODEL)),
        ],
        out_specs=row(D_MODEL),
        out_shape=jax.ShapeDtypeStruct((n, D_MODEL), F32),
        compiler_params=_params("parallel"),
        name="gated_merge",
    )(x, oa, ob, oc, od, w_gate, pa, pb, pc, pd, w_out, g, b)


def _mem_kv_kernel(m_ref, g_ref, b_ref, w_ref, o_ref):
    mn = _layer_norm(m_ref[...], g_ref[...], b_ref[...])
    o_ref[...] = _dot(mn.astype(BF16), w_ref[...]).astype(BF16)


def _mem_kv(mem, g, b, wkv):
    n = mem.shape[0]
    return pl.pallas_call(
        _mem_kv_kernel,
        grid=(DEPTH, n // ROW_TILE),
        in_specs=[
            pl.BlockSpec((ROW_TILE, D_MODEL), lambda l, i: (i, 0)),
            _resident((1, D_MODEL)), _resident((1, D_MODEL)),
            pl.BlockSpec((None, D_MODEL, 2 * CROSS_WIDTH), lambda l, i: (l, 0, 0)),
        ],
        out_specs=pl.BlockSpec((None, ROW_TILE, 2 * CROSS_WIDTH), lambda l, i: (l, i, 0)),
        out_shape=jax.ShapeDtypeStruct((DEPTH, n, 2 * CROSS_WIDTH), BF16),
        compiler_params=_params("parallel", "parallel"),
        name="mem_kv",
    )(mem, g, b, wkv)


def _cross_kernel(x_ref, kv_ref, wq_ref, wo_ref, g_ref, b_ref, o_ref):
    x = x_ref[...]
    q = _dot(x.astype(BF16), wq_ref[...]).astype(BF16)
    kv = kv_ref[...]
    scale = 1.0 / math.sqrt(CROSS_HEAD_DIM)
    outs = []

    def scores(h):
        sl = slice(h * CROSS_HEAD_DIM, (h + 1) * CROSS_HEAD_DIM)
        return _dot_nt(q[:, sl], kv[:, sl]) * scale

    s_next = scores(0)
    for h in range(CROSS_HEADS):
        s = s_next
        if h + 1 < CROSS_HEADS:
            s_next = scores(h + 1)
        p = jnp.exp(s - jnp.max(s, axis=-1, keepdims=True))
        p = p / jnp.sum(p, axis=-1, keepdims=True)
        outs.append(_dot(p.astype(BF16), kv[:, CROSS_WIDTH + h * CROSS_HEAD_DIM:CROSS_WIDTH + (h + 1) * CROSS_HEAD_DIM]))
    o = jnp.concatenate(outs, axis=1).astype(BF16)
    c = _dot(o, wo_ref[...])
    o_ref[...] = _layer_norm(DEEPNORM_ALPHA * x + c, g_ref[...], b_ref[...])


def _cross_attn(x, kv, wq, wo, g, b, batch, seq):
    x3 = x.reshape(batch, seq, D_MODEL)
    kv3 = kv.reshape(batch, MEM_LEN, 2 * CROSS_WIDTH)
    return pl.pallas_call(
        _cross_kernel,
        grid=(batch, seq // CROSS_TILE),
        in_specs=[
            pl.BlockSpec((None, CROSS_TILE, D_MODEL), lambda b, i: (b, i, 0)),
            pl.BlockSpec((None, MEM_LEN, 2 * CROSS_WIDTH), lambda b, i: (b, 0, 0)),
            _resident((D_MODEL, CROSS_WIDTH)),
            _resident((CROSS_WIDTH, D_MODEL)),
            _resident((1, D_MODEL)), _resident((1, D_MODEL)),
        ],
        out_specs=pl.BlockSpec((None, CROSS_TILE, D_MODEL), lambda b, i: (b, i, 0)),
        out_shape=jax.ShapeDtypeStruct((batch, seq, D_MODEL), F32),
        compiler_params=_params("parallel", "parallel"),
        name="cross_attn",
    )(x3, kv3, wq, wo, g, b).reshape(batch * seq, D_MODEL)


def _row(v):
    return v.reshape(1, -1).astype(F32)


def kernel(x, mem, ffn1_w_in, ffn1_w_out, ffn1_ln_g, ffn1_ln_b, mix_w_in, swa_sinks, swa_proj, s5_a_re, s5_a_im, s5_log_step, s5_b_re, s5_b_im, s5_c_re, s5_c_im, s5_d, s5_glu_w, s5_glu_b, s5_proj, conv_w, conv_b, conv_ln_g, conv_ln_b, conv_proj, diff_lq1, diff_lk1, diff_lq2, diff_lk2, diff_norm_g, diff_proj, mix_w_out, mix_ln_g, mix_ln_b, mem_ln_g, mem_ln_b, cross_wq, cross_wkv, cross_wo, cross_ln_g, cross_ln_b, ffn2_w_in, ffn2_w_out, ffn2_ln_g, ffn2_ln_b):
    batch, seq, _ = x.shape
    n = batch * seq
    assert seq % DIFF_TILE == 0 and seq % ROW_TILE == 0 and seq % CROSS_TILE == 0
    assert seq % S5_STEP == 0 and seq % SWA_TILE == 0 and (batch * MEM_LEN) % ROW_TILE == 0
    h = x.reshape(n, D_MODEL)
    kv_all = _mem_kv(mem.reshape(batch * MEM_LEN, D_MODEL), _row(mem_ln_g), _row(mem_ln_b),
                     cross_wkv.astype(BF16))
    for l in range(DEPTH):
        h = _ffn_ln(h, ffn1_w_in[l].astype(BF16), ffn1_w_out[l].astype(BF16),
                    _row(ffn1_ln_g[l]), _row(ffn1_ln_b[l]))
        lambda_init = 0.8 - 0.6 * math.exp(-0.3 * l)
        w = mix_w_in[l]
        cols = lambda off, width: w[:, off:off + width]
        w_att = jnp.concatenate([cols(OFF_DQ, D_QK), cols(OFF_DK, D_QK), cols(OFF_AK, A_KV)], axis=1).astype(BF16)
        w_q = _swa_q_weight(cols(OFF_AQ, A_Q)).astype(BF16)
        w_v_t = jnp.concatenate([cols(OFF_DV, D_V), cols(OFF_AV, A_KV)], axis=1).T.astype(BF16)
        w_gate = cols(OFF_GL, GATE_COLS).astype(BF16)
        qk, ak, q_swa, su, v_t, o_c = _mix_proj(
            h, cols(OFF_CU, CONV_IN).astype(BF16), w_att, w_q, cols(OFF_SU, S5_WIDTH).astype(BF16), w_v_t,
            conv_w[l], _row(conv_b[l]), _row(conv_ln_g[l]), _row(conv_ln_b[l]), batch, seq)
        o_a = _swa(ak, q_swa, v_t, swa_sinks[l], batch, seq)
        tables = _s5_tables(s5_a_re[l], s5_a_im[l], s5_log_step[l], s5_b_re[l], s5_b_im[l], s5_c_re[l], s5_c_im[l])
        o_b = _s5(su, tables, _row(s5_d[l]), s5_glu_w[l].astype(BF16), _row(s5_glu_b[l]), batch, seq)
        lam = (jnp.exp(jnp.sum(diff_lq1[l] * diff_lk1[l])) - jnp.exp(jnp.sum(diff_lq2[l] * diff_lk2[l]))
               + lambda_init).astype(F32)
        o_d = _diff_attn(qk, v_t, jnp.stack([lam, jnp.float32(1.0 - lambda_init)]), _row(diff_norm_g[l]), batch, seq)
        h = _merge(h, o_a, o_b, o_c, o_d, w_gate, swa_proj[l].astype(BF16), s5_proj[l].astype(BF16),
                   conv_proj[l].astype(BF16), diff_proj[l].astype(BF16), mix_w_out[l].astype(BF16),
                   _row(mix_ln_g[l]), _row(mix_ln_b[l]))
        h = _cross_attn(h, kv_all[l], cross_wq[l].astype(BF16), cross_wo[l].astype(BF16),
                        _row(cross_ln_g[l]), _row(cross_ln_b[l]), batch, seq)
        h = _ffn_ln(h, ffn2_w_in[l].astype(BF16), ffn2_w_out[l].astype(BF16),
                    _row(ffn2_ln_g[l]), _row(ffn2_ln_b[l]))
    return h.reshape(batch, seq, D_MODEL)
```

```python
import functools
import math

import jax
import jax.numpy as jnp
from jax import lax
from jax.experimental import pallas as pl
from jax.experimental.pallas import tpu as pltpu

F32 = jnp.float32
BF16 = jnp.bfloat16

D_MODEL = 1024
DEPTH = 4
MEM_LEN = 256
HEAD_DIM = 64
BLOCK = 128
SWA_HEADS = 8
SWA_KV_HEADS = 2
SWA_REP = SWA_HEADS // SWA_KV_HEADS
S5_WIDTH = 512
S5_GROUP = 16
S5_GROUPS = S5_WIDTH // S5_GROUP
S5_STATE = 64
S5_NSTATE = S5_GROUPS * S5_STATE
CONV_WIDTH = 512
CONV_K = 31
DIFF_HEADS = 4
DIFF_V_DIM = 2 * HEAD_DIM
CROSS_HEADS = 4
CROSS_HEAD_DIM = 128
CROSS_WIDTH = CROSS_HEADS * CROSS_HEAD_DIM
FFN_DIM = 2816
N_BRANCHES = 4

A_Q = SWA_HEADS * HEAD_DIM
A_KV = SWA_KV_HEADS * HEAD_DIM
D_QK = DIFF_HEADS * 2 * HEAD_DIM
D_V = DIFF_HEADS * DIFF_V_DIM
CONV_IN = 2 * CONV_WIDTH
GATE_COLS = N_BRANCHES * D_MODEL
OFF_AQ = 0
OFF_AK = OFF_AQ + A_Q
OFF_AV = OFF_AK + A_KV
OFF_DQ = OFF_AV + A_KV
OFF_DK = OFF_DQ + D_QK
OFF_DV = OFF_DK + D_QK
OFF_SU = OFF_DV + D_V
OFF_CU = OFF_SU + S5_WIDTH
OFF_GL = OFF_CU + CONV_IN
ATT_COLS = 2 * D_QK + A_KV
VT_ROWS = D_V + A_KV

DEEPNORM_ALPHA = (2.0 * DEPTH) ** 0.25
LN_EPS = 1e-5
NEG_INF = -1e30

LANES = 128
SUBLANES = 8
MXU_WIDTH = 256
VMEM_LIMIT = 56 * 1024 * 1024

ROW_TILE = 512
FFN_CHUNK = MXU_WIDTH
MERGE_ROWS = 1024
MERGE_PART = 256
FFN_ROWS = 1024
FFN_PART = 256
DIFF_TILE = 512
ONES_ROWS = 16
DIFF_VROWS = DIFF_V_DIM + ONES_ROWS
SWA_TILE = 1024
LOG2E = math.log2(math.e)
CONV_HALO = 32
S5_TILE = 256
S5_STEP = 1024
CROSS_TILE = 1024


def _alibi_slope(h, n):
    return 2.0 ** (-8.0 * (h + 1) / n)


def _layer_norm(z, g, b):
    mu = jnp.mean(z, axis=-1, keepdims=True)
    zc = z - mu
    var = jnp.mean(zc * zc, axis=-1, keepdims=True)
    return zc * lax.rsqrt(var + LN_EPS) * g + b


def _dot(a, b):
    return jnp.dot(a, b, preferred_element_type=F32)


def _dot_nt(a, b):
    return lax.dot_general(a, b, (((1,), (1,)), ((), ())), preferred_element_type=F32)


def _params(*sem):
    return pltpu.CompilerParams(dimension_semantics=sem, vmem_limit_bytes=VMEM_LIMIT)


def _resident(shape):
    nd = len(shape)
    return pl.BlockSpec(shape, lambda *_: (0,) * nd, pipeline_mode=pl.Buffered(1))


def _ffn_ln_kernel(x_ref, wg_ref, wu_ref, wo_ref, g_ref, b_ref, o_ref, a_ref):
    half = FFN_PART
    for r in range(FFN_ROWS // FFN_PART):
        rows = slice(r * half, (r + 1) * half)
        x = x_ref[rows, :]
        xb = x.astype(BF16)
        for c in range(FFN_DIM // FFN_CHUNK):
            sl = slice(c * FFN_CHUNK, (c + 1) * FFN_CHUNK)
            gate = _dot(xb, wg_ref[:, sl])
            up = _dot(xb, wu_ref[:, sl])
            a_ref[rows, sl] = (gate * jax.nn.sigmoid(gate) * up).astype(BF16)
        f = _dot(a_ref[rows, :], wo_ref[...])
        o_ref[rows, :] = _layer_norm(DEEPNORM_ALPHA * x + 0.5 * f, g_ref[...], b_ref[...])


def _ffn_ln(x, w_in, w_out, g, b):
    n = x.shape[0]
    return pl.pallas_call(
        _ffn_ln_kernel,
        grid=(n // FFN_ROWS,),
        in_specs=[
            pl.BlockSpec((FFN_ROWS, D_MODEL), lambda i: (i, 0)),
            pl.BlockSpec((D_MODEL, FFN_DIM), lambda i: (0, 0), pipeline_mode=pl.Buffered(1)),
            pl.BlockSpec((D_MODEL, FFN_DIM), lambda i: (0, 1), pipeline_mode=pl.Buffered(1)),
            _resident((FFN_DIM, D_MODEL)),
            _resident((1, D_MODEL)),
            _resident((1, D_MODEL)),
        ],
        out_specs=pl.BlockSpec((FFN_ROWS, D_MODEL), lambda i: (i, 0)),
        out_shape=jax.ShapeDtypeStruct((n, D_MODEL), F32),
        scratch_shapes=[pltpu.VMEM((FFN_ROWS, FFN_DIM), BF16)],
        compiler_params=_params("parallel"),
        name="ffn_ln",
    )(x, w_in, w_in, w_out, g, b)


def _mix_proj_kernel(x_ref, wc_ref, wa_ref, wq_ref, ws_ref, wvt_ref, cw_ref, cb_ref, cg_ref, cbeta_ref,
                     qk_ref, ak_ref, q_ref, su_ref, vt_ref, oc_ref, buf_ref, *, per_seq):
    i = pl.program_id(0)

    @pl.when(i == 0)
    def _():
        buf_ref[ROW_TILE:, :] = jnp.zeros((CONV_HALO, CONV_WIDTH), F32)

    xb = x_ref[...].astype(BF16)
    cu = _dot(xb, wc_ref[...])
    ha = _dot(xb, wa_ref[...])
    for j in range(2 * DIFF_HEADS):
        qk_ref[j] = ha[:, j * LANES:(j + 1) * LANES].astype(BF16)
    ak_ref[...] = ha[:, 2 * D_QK:].astype(BF16)
    q = _dot(xb, wq_ref[...])
    for h in range(SWA_HEADS):
        q_ref[h] = q[:, h * LANES:(h + 1) * LANES].astype(BF16)
    su_ref[...] = _dot(xb, ws_ref[...])
    vt_ref[...] = _dot_nt(wvt_ref[...], xb).astype(BF16)
    keep = (i % per_seq != 0).astype(F32)
    buf_ref[0:CONV_HALO, :] = buf_ref[ROW_TILE:, :] * keep
    buf_ref[CONV_HALO:, :] = cu[:, :CONV_WIDTH] * jax.nn.sigmoid(cu[:, CONV_WIDTH:])
    bufv = buf_ref[...]
    nrows = ROW_TILE + CONV_HALO
    shifted = [bufv] + [pltpu.roll(bufv, nrows - b, axis=0) for b in range(1, SUBLANES)]
    first = CONV_HALO - (CONV_K - 1)
    acc = jnp.zeros((ROW_TILE, CONV_WIDTH), F32)
    for k in range(CONV_K):
        a, b = divmod(first + k, SUBLANES)
        lo = a * SUBLANES
        acc = acc + cw_ref[k:k + 1, :] * shifted[b][lo:lo + ROW_TILE, :]
    y = _layer_norm(acc + cb_ref[...], cg_ref[...], cbeta_ref[...])
    oc_ref[...] = (y * jax.nn.sigmoid(y)).astype(BF16)


def _mix_proj(x, w_cu, w_att, w_q, w_su, w_v_t, conv_w, conv_b, conv_g, conv_beta, batch, seq):
    n = x.shape[0]
    per_seq = seq // ROW_TILE
    row = lambda w: pl.BlockSpec((ROW_TILE, w), lambda i: (i, 0))
    return pl.pallas_call(
        functools.partial(_mix_proj_kernel, per_seq=per_seq),
        grid=(n // ROW_TILE,),
        in_specs=[
            row(D_MODEL),
            _resident((D_MODEL, CONV_IN)),
            _resident((D_MODEL, ATT_COLS)),
            _resident((D_MODEL, SWA_HEADS * LANES)),
            _resident((D_MODEL, S5_WIDTH)),
            _resident((VT_ROWS, D_MODEL)),
            _resident((CONV_K, CONV_WIDTH)),
            _resident((1, CONV_WIDTH)), _resident((1, CONV_WIDTH)), _resident((1, CONV_WIDTH)),
        ],
        out_specs=[
            pl.BlockSpec((2 * DIFF_HEADS, ROW_TILE, LANES), lambda i: (0, i, 0)),
            row(A_KV),
            pl.BlockSpec((SWA_HEADS, ROW_TILE, LANES), lambda i: (0, i, 0)),
            row(S5_WIDTH),
            pl.BlockSpec((None, VT_ROWS, ROW_TILE), lambda i: (i // per_seq, 0, i % per_seq)),
            row(CONV_WIDTH),
        ],
        out_shape=[
            jax.ShapeDtypeStruct((2 * DIFF_HEADS, n, LANES), BF16),
            jax.ShapeDtypeStruct((n, A_KV), BF16),
            jax.ShapeDtypeStruct((SWA_HEADS, n, LANES), BF16),
            jax.ShapeDtypeStruct((n, S5_WIDTH), F32),
            jax.ShapeDtypeStruct((batch, VT_ROWS, seq), BF16),
            jax.ShapeDtypeStruct((n, CONV_WIDTH), BF16),
        ],
        scratch_shapes=[pltpu.VMEM((ROW_TILE + CONV_HALO, CONV_WIDTH), F32)],
        compiler_params=_params("arbitrary"),
        name="mix_proj_conv",
    )(x, w_cu, w_att, w_q, w_su, w_v_t, conv_w, conv_b, conv_g, conv_beta)


def _swa_q_weight(w_aq):
    d = w_aq.shape[0]
    w = (w_aq * (LOG2E / math.sqrt(HEAD_DIM))).reshape(d, SWA_KV_HEADS, SWA_REP, 1, HEAD_DIM)
    sel = jnp.eye(SWA_KV_HEADS, dtype=w.dtype).reshape(1, SWA_KV_HEADS, 1, SWA_KV_HEADS, 1)
    return (w * sel).reshape(d, SWA_HEADS * LANES)


def _swa_kernel(sink_ref, bias_ref, q_ref, kc_ref, kp_ref, vc_ref, vp_ref, o_ref):
    i = pl.program_id(1)
    gw = SWA_REP * BLOCK
    k_all = jnp.concatenate([kp_ref[...], kc_ref[...]], axis=0)
    v_all = jnp.concatenate([vp_ref[...], vc_ref[...]], axis=1)
    vx_all = jnp.concatenate([v_all, jnp.ones((ONES_ROWS, v_all.shape[1]), BF16)], axis=0)
    no_prev = jnp.where(i == 0, NEG_INF, 0.0)

    def scores(blk, g):
        lo = blk * BLOCK
        qg = q_ref[g * SWA_REP:(g + 1) * SWA_REP, lo:lo + BLOCK, :].reshape(gw, LANES)
        s = _dot_nt(k_all[lo:lo + 2 * BLOCK], qg) + bias_ref[g]
        if blk == 0:
            s = jnp.concatenate([s[:BLOCK] + no_prev, s[BLOCK:]], axis=0)
        return s

    units = [(blk, g) for blk in range(SWA_TILE // BLOCK) for g in range(SWA_KV_HEADS)]
    s_next = scores(*units[0])
    outs = []
    for u, (blk, g) in enumerate(units):
        s = s_next
        if u + 1 < len(units):
            s_next = scores(*units[u + 1])
        lo = blk * BLOCK
        sink = sink_ref[g]
        m = jnp.maximum(jnp.max(s, axis=0, keepdims=True), sink)
        e = jnp.exp2(s - m).astype(BF16)
        pv = _dot(vx_all[:, lo:lo + 2 * BLOCK], e)
        den = pv[LANES:LANES + 1] + jnp.exp2(sink - m)
        og = pv[g * HEAD_DIM:(g + 1) * HEAD_DIM] * (1.0 / den)
        outs += [og[:, r * BLOCK:(r + 1) * BLOCK] for r in range(SWA_REP)]
        if g == SWA_KV_HEADS - 1:
            o_ref[lo:lo + BLOCK, :] = jnp.concatenate(outs, axis=0).T.astype(BF16)
            outs = []


def _swa_bias():
    kj = jnp.arange(2 * BLOCK)[:, None]
    qi = jnp.arange(BLOCK)[None, :]
    dist = BLOCK + qi - kj
    valid = (dist >= 0) & (dist < BLOCK)
    slopes = jnp.asarray([_alibi_slope(h, SWA_HEADS) * LOG2E for h in range(SWA_HEADS)], F32)
    b = jnp.where(valid[None], -slopes[:, None, None] * dist[None].astype(F32), NEG_INF)
    b = b.reshape(SWA_KV_HEADS, SWA_REP, 2 * BLOCK, BLOCK)
    return jnp.transpose(b, (0, 2, 1, 3)).reshape(SWA_KV_HEADS, 2 * BLOCK, SWA_REP * BLOCK)


def _swa(ak, q_swa, v_t, sinks, batch, seq):
    h3 = ak.reshape(batch, seq, A_KV)
    rv = D_V // A_KV
    per_seq = seq // SWA_TILE
    r = SWA_TILE // BLOCK
    gw = SWA_REP * BLOCK
    sink_rows = jnp.repeat(sinks.astype(F32).reshape(SWA_KV_HEADS, SWA_REP) * LOG2E, BLOCK, axis=1)
    prev = lambda i: jnp.maximum(i * r - 1, 0)
    return pl.pallas_call(
        _swa_kernel,
        grid=(batch, per_seq),
        in_specs=[
            _resident((SWA_KV_HEADS, 1, gw)),
            _resident((SWA_KV_HEADS, 2 * BLOCK, gw)),
            pl.BlockSpec((SWA_HEADS, SWA_TILE, LANES), lambda b, i: (0, b * per_seq + i, 0)),
            pl.BlockSpec((None, SWA_TILE, A_KV), lambda b, i: (b, i, 0)),
            pl.BlockSpec((None, BLOCK, A_KV), lambda b, i: (b, prev(i), 0)),
            pl.BlockSpec((None, A_KV, SWA_TILE), lambda b, i: (b, rv, i)),
            pl.BlockSpec((None, A_KV, BLOCK), lambda b, i: (b, rv, prev(i))),
        ],
        out_specs=pl.BlockSpec((None, SWA_TILE, A_Q), lambda b, i: (b, i, 0)),
        out_shape=jax.ShapeDtypeStruct((batch, seq, A_Q), BF16),
        compiler_params=_params("parallel", "parallel"),
        name="swa",
    )(sink_rows.reshape(SWA_KV_HEADS, 1, gw), _swa_bias(), q_swa, h3, h3, v_t, v_t).reshape(batch * seq, A_Q)


def _diff_kernel(sc_ref, q_ref, k_ref, vt_ref, g_ref, o_ref,
                 qs_ref, vx_ref, b_ref, s_ref, mx_ref, m_ref, acc_ref, *, n_tiles):
    h = pl.program_id(1)
    t = DIFF_TILE
    hf = t // 2
    v_dim = DIFF_V_DIM
    slope = jnp.float32(_alibi_slope(DIFF_HEADS - 1, DIFF_HEADS) * LOG2E)
    for hh in range(DIFF_HEADS - 1):
        slope = jnp.where(h == hh, jnp.float32(_alibi_slope(hh, DIFF_HEADS) * LOG2E), slope)

    vx_ref[:v_dim, :] = vt_ref[...]
    vx_ref[v_dim:, :] = jnp.ones((DIFF_VROWS - v_dim, vx_ref.shape[1]), BF16)
    krow = lax.broadcasted_iota(jnp.int32, (t, t), 0)
    qcol = lax.broadcasted_iota(jnp.int32, (t, t), 1)
    b_ref[0] = slope * krow.astype(F32)
    b_ref[1] = jnp.where(qcol >= krow, slope * krow.astype(F32), NEG_INF)
    lane = lax.broadcasted_iota(jnp.int32, (t, 2 * HEAD_DIM), 1)
    for qt in range(n_tiles):
        q = q_ref[qt * t:(qt + 1) * t, :].astype(F32) * (LOG2E / math.sqrt(HEAD_DIM))
        qs_ref[qt, 0] = jnp.where(lane < HEAD_DIM, q, 0.0).astype(BF16)
        qs_ref[qt, 1] = jnp.where(lane >= HEAD_DIM, q, 0.0).astype(BF16)

    def items_of(qt):
        out = []
        for j in range(qt):
            out += [(qt, j * t, t, 0, t, None, c) for c in range(2)]
        out += [(qt, qt * t, hf, 0, t, (0, 0), c) for c in range(2)]
        out += [(qt, qt * t + hf, hf, hf, hf, (hf, hf), c) for c in range(2)]
        return out

    def scores(item):
        qt, r0, nr, c0, nc, tri, c = item
        bias = b_ref[0, 0:nr, 0:nc] if tri is None else b_ref[1, tri[0]:tri[0] + nr, tri[1]:tri[1] + nc]
        s = _dot_nt(k_ref[r0:r0 + nr, :], qs_ref[qt, c, c0:c0 + nc, :]) + bias
        s_ref[c, 0:nr, 0:nc] = s
        mx_ref[c, :, 0:nc] = jnp.max(s, axis=0, keepdims=True)

    def accumulate(item, first):
        qt, r0, nr, c0, nc, tri, c = item
        mx = mx_ref[c, :, 0:nc]
        e = jnp.exp2(s_ref[c, 0:nr, 0:nc] - mx).astype(BF16)
        pv = _dot(vx_ref[:, r0:r0 + nr], e)
        mxo = mx + slope * float(r0 - (0 if tri is None else tri[0]))
        if first:
            m_ref[c, :, c0:c0 + nc] = mxo
            acc_ref[c, :, c0:c0 + nc] = pv
        else:
            m_old = m_ref[c, :, c0:c0 + nc]
            m_new = jnp.maximum(m_old, mxo)
            m_ref[c, :, c0:c0 + nc] = m_new
            acc_ref[c, :, c0:c0 + nc] = (jnp.exp2(m_old - m_new) * acc_ref[c, :, c0:c0 + nc]
                                         + jnp.exp2(mxo - m_new) * pv)

    def finalize(qt):
        def normalised(c):
            a = acc_ref[c]
            return a[:v_dim] * (1.0 / a[v_dim:v_dim + 1])

        o = (normalised(0) - sc_ref[0] * normalised(1)).T
        o = o * lax.rsqrt(jnp.mean(o * o, axis=-1, keepdims=True) + LN_EPS) * g_ref[...] * sc_ref[1]
        o_ref[qt * t:(qt + 1) * t, :] = o.astype(BF16)

    items = [it for qt in range(n_tiles) for it in items_of(qt)]
    scores(items[0])
    for n, item in enumerate(items):
        if n + 1 < len(items):
            scores(items[n + 1])
        qt = item[0]
        accumulate(item, first=(n < 2 or items[n - 2][0] != qt))
        if n + 1 == len(items) or items[n + 1][0] != qt:
            finalize(qt)


def _diff_attn(qk, v_t, scalars, norm_g, batch, seq):
    h3 = qk.reshape(2 * DIFF_HEADS, batch, seq, LANES)
    w = 2 * HEAD_DIM
    t = DIFF_TILE
    n_tiles = seq // t
    return pl.pallas_call(
        functools.partial(_diff_kernel, n_tiles=n_tiles),
        grid=(batch, DIFF_HEADS),
        in_specs=[
            pl.BlockSpec(memory_space=pltpu.SMEM),
            pl.BlockSpec((None, None, seq, w), lambda b, h: (h, b, 0, 0)),
            pl.BlockSpec((None, None, seq, w), lambda b, h: (DIFF_HEADS + h, b, 0, 0)),
            pl.BlockSpec((None, DIFF_V_DIM, seq), lambda b, h: (b, h, 0)),
            _resident((1, DIFF_V_DIM)),
        ],
        out_specs=pl.BlockSpec((None, None, seq, w), lambda b, h: (h, b, 0, 0)),
        out_shape=jax.ShapeDtypeStruct((DIFF_HEADS, batch, seq, w), BF16),
        scratch_shapes=[
            pltpu.VMEM((n_tiles, 2, t, w), BF16),
            pltpu.VMEM((DIFF_VROWS, seq), BF16),
            pltpu.VMEM((2, t, t), F32),
            pltpu.VMEM((2, t, t), F32),
            pltpu.VMEM((2, 1, t), F32),
            pltpu.VMEM((2, 1, t), F32),
            pltpu.VMEM((2, DIFF_VROWS, t), F32),
        ],
        compiler_params=_params("parallel", "parallel"),
        name="diff_attn",
    )(scalars, h3, h3, v_t, norm_g).reshape(DIFF_HEADS, batch * seq, w)


def _s5_kernel(u_ref, bw_ref, cw_ref, air_ref, aii_ref, apr_ref, api_ref, ah_ref, tri_ref,
               d_ref, gw_ref, gb_ref, o_ref, hr_ref, hi_ref):
    i = pl.program_id(1)

    @pl.when(i == 0)
    def _():
        hr_ref[...] = jnp.zeros_like(hr_ref)
        hi_ref[...] = jnp.zeros_like(hi_ref)

    tri = tri_ref[...]
    nq = S5_WIDTH // LANES
    sw = S5_NSTATE // nq
    tt = S5_TILE

    def in_proj(k, j):
        ub = u_ref[k * tt:(k + 1) * tt, j * LANES:(j + 1) * LANES].astype(BF16)
        return _dot(ub, bw_ref[j])

    def finish(k, ys):
        u = u_ref[k * tt:(k + 1) * tt, :]
        y = jnp.concatenate(ys, axis=1) + d_ref[...] * u
        y = jax.nn.gelu(y, approximate=True)
        gate = _dot(y.astype(BF16), gw_ref[...]) + gb_ref[...]
        o_ref[k * tt:(k + 1) * tt, :] = (y * jax.nn.sigmoid(gate)).astype(BF16)

    units = [(k, j) for k in range(S5_STEP // tt) for j in range(nq)]
    bu_next = in_proj(*units[0])
    prev = None
    ys = []
    for n, (k, j) in enumerate(units):
        sl = slice(j * sw, (j + 1) * sw)
        bu = bu_next
        if n + 1 < len(units):
            bu_next = in_proj(*units[n + 1])
        bur, bui = bu[:, :sw], bu[:, sw:]
        air, aii = air_ref[:, sl], aii_ref[:, sl]
        zr = (air * bur - aii * bui).astype(BF16)
        zi = (air * bui + aii * bur).astype(BF16)
        cr = _dot(tri, zr) + hr_ref[:, sl]
        ci = _dot(tri, zi) + hi_ref[:, sl]
        if prev is not None:
            ys.append(_dot(prev[2], cw_ref[prev[1]]))
            if prev[1] == nq - 1:
                finish(prev[0], ys)
                ys = []
        apr, api = apr_ref[:, sl], api_ref[:, sl]
        xr = apr * cr - api * ci
        xi = apr * ci + api * cr
        lr, li = xr[tt - 1:, :], xi[tt - 1:, :]
        ahr, ahi = ah_ref[0:1, sl], ah_ref[1:2, sl]
        hr_ref[:, sl] = ahr * lr - ahi * li
        hi_ref[:, sl] = ahr * li + ahi * lr
        prev = (k, j, jnp.concatenate([xr, xi], axis=1).astype(BF16))
    ys.append(_dot(prev[2], cw_ref[prev[1]]))
    finish(prev[0], ys)


def _s5_tables(a_re, a_im, log_step, b_re, b_im, c_re, c_im):
    g, p, c = S5_GROUPS, S5_STATE, S5_GROUP
    step = jnp.exp(log_step)[:, None]
    mag = jnp.exp(a_re * step)
    abar_r, abar_i = mag * jnp.cos(a_im * step), mag * jnp.sin(a_im * step)
    den = a_re * a_re + a_im * a_im
    nr, ni = abar_r - 1.0, abar_i
    coef_r = (nr * a_re + ni * a_im) / den
    coef_i = (ni * a_re - nr * a_im) / den
    bbar_r = coef_r[..., None] * b_re - coef_i[..., None] * b_im
    bbar_i = coef_r[..., None] * b_im + coef_i[..., None] * b_re
    nq = S5_WIDTH // LANES
    gq = g // nq
    eye = jnp.eye(gq, dtype=F32)

    def in_slab(bb):
        bb = bb.reshape(nq, gq, p, c)
        return jnp.einsum('qgpc,gh->qgchp', bb, eye).reshape(nq, gq * c, gq * p)

    bw = jnp.concatenate([in_slab(bbar_r), in_slab(bbar_i)], axis=2).astype(BF16)

    def out_slab(cc):
        cc = cc.reshape(nq, gq, c, p)
        return jnp.einsum('qgcp,gh->qgphc', cc, eye).reshape(nq, gq * p, gq * c)

    cw = jnp.concatenate([out_slab(c_re), -out_slab(c_im)], axis=1).astype(BF16)
    half = S5_TILE // 2
    tt = jnp.arange(1 - half, S5_TILE + 1 - half, dtype=F32)[:, None]
    la = (a_re * step).reshape(1, g * p)
    th = (a_im * step).reshape(1, g * p)
    pm, ang = jnp.exp(tt * la), tt * th
    im_ = jnp.exp(-tt * la)
    apr, api = pm * jnp.cos(ang), pm * jnp.sin(ang)
    air, aii = im_ * jnp.cos(ang), -im_ * jnp.sin(ang)
    hm = jnp.exp(half * la)
    ah = jnp.concatenate([hm * jnp.cos(half * th), hm * jnp.sin(half * th)], axis=0)
    return bw, cw, air, aii, apr, api, ah


def _s5(su, tables, d_skip, glu_w, glu_b, batch, seq):
    bw, cw, air, aii, apr, api, ah = tables
    h3 = su.reshape(batch, seq, S5_WIDTH)
    tri = jnp.tril(jnp.ones((S5_TILE, S5_TILE), F32)).astype(BF16)
    tab = (S5_TILE, S5_NSTATE)
    return pl.pallas_call(
        _s5_kernel,
        grid=(batch, seq // S5_STEP),
        in_specs=[
            pl.BlockSpec((None, S5_STEP, S5_WIDTH), lambda b, i: (b, i, 0)),
            _resident(bw.shape), _resident(cw.shape),
            _resident(tab), _resident(tab), _resident(tab), _resident(tab),
            _resident((2, S5_NSTATE)),
            _resident((S5_TILE, S5_TILE)),
            _resident((1, S5_WIDTH)),
            _resident((S5_WIDTH, S5_WIDTH)),
            _resident((1, S5_WIDTH)),
        ],
        out_specs=pl.BlockSpec((None, S5_STEP, S5_WIDTH), lambda b, i: (b, i, 0)),
        out_shape=jax.ShapeDtypeStruct((batch, seq, S5_WIDTH), BF16),
        scratch_shapes=[pltpu.VMEM((1, S5_NSTATE), F32), pltpu.VMEM((1, S5_NSTATE), F32)],
        compiler_params=_params("parallel", "arbitrary"),
        name="s5_ssm",
    )(h3, bw, cw, air, aii, apr, api, ah, tri, d_skip, glu_w, glu_b).reshape(batch * seq, S5_WIDTH)


def _merge_kernel(x_ref, oa_ref, ob_ref, oc_ref, od_ref, wg_ref, pa_ref, pb_ref, pc_ref, pd_ref,
                  wo_ref, g_ref, b_ref, o_ref):
    half = MERGE_PART
    for r in range(MERGE_ROWS // MERGE_PART):
        rows = slice(r * half, (r + 1) * half)
        x = x_ref[rows, :]
        xb = x.astype(BF16)
        merged = jnp.zeros((half, D_MODEL), F32)
        od = jnp.concatenate([od_ref[hd, rows, :] for hd in range(DIFF_HEADS)], axis=1)
        branches = ((oa_ref[rows, :], pa_ref), (ob_ref[rows, :], pb_ref), (oc_ref[rows, :], pc_ref), (od, pd_ref))
        for i, (br, pr_ref) in enumerate(branches):
            gl = _dot(xb, wg_ref[:, i * D_MODEL:(i + 1) * D_MODEL])
            merged = merged + jax.nn.sigmoid(gl) * _dot(br, pr_ref[...])
        m = _dot(merged.astype(BF16), wo_ref[...])
        o_ref[rows, :] = _layer_norm(DEEPNORM_ALPHA * x + m, g_ref[...], b_ref[...])


def _merge(x, oa, ob, oc, od, w_gate, pa, pb, pc, pd, w_out, g, b):
    n = x.shape[0]
    row = lambda w: pl.BlockSpec((MERGE_ROWS, w), lambda i: (i, 0))
    return pl.pallas_call(
        _merge_kernel,
        grid=(n // MERGE_ROWS,),
        in_specs=[
            row(D_MODEL), row(A_Q), row(S5_WIDTH), row(CONV_WIDTH),
            pl.BlockSpec((DIFF_HEADS, MERGE_ROWS, DIFF_V_DIM), lambda i: (0, i, 0)),
            _resident((D_MODEL, GATE_COLS)),
            _resident((A_Q, D_MODEL)), _resident((S5_WIDTH, D_MODEL)),
            _resident((CONV_WIDTH, D_MODEL)), _resident((D_V, D_MODEL)),
            _resident((D_MODEL, D_MODEL)),
            _resident((1, D_MODEL)), _resident((1, D_MODEL)),
        ],
        out_specs=row(D_MODEL),
        out_shape=jax.ShapeDtypeStruct((n, D_MODEL), F32),
        compiler_params=_params("parallel"),
        name="gated_merge",
    )(x, oa, ob, oc, od, w_gate, pa, pb, pc, pd, w_out, g, b)


def _mem_kv_kernel(m_ref, g_ref, b_ref, w_ref, o_ref):
    mn = _layer_norm(m_ref[...], g_ref[...], b_ref[...])
    o_ref[...] = _dot(mn.astype(BF16), w_ref[...]).astype(BF16)


def _mem_kv(mem, g, b, wkv):
    n = mem.shape[0]
    return pl.pallas_call(
        _mem_kv_kernel,
        grid=(DEPTH, n // ROW_TILE),
        in_specs=[
            pl.BlockSpec((ROW_TILE, D_MODEL), lambda l, i: (i, 0)),
            _resident((1, D_MODEL)), _resident((1, D_MODEL)),
            pl.BlockSpec((None, D_MODEL, 2 * CROSS_WIDTH), lambda l, i: (l, 0, 0)),
        ],
        out_specs=pl.BlockSpec((None, ROW_TILE, 2 * CROSS_WIDTH), lambda l, i: (l, i, 0)),
        out_shape=jax.ShapeDtypeStruct((DEPTH, n, 2 * CROSS_WIDTH), BF16),
        compiler_params=_params("parallel", "parallel"),
        name="mem_kv",
    )(mem, g, b, wkv)


def _cross_kernel(x_ref, kv_ref, wq_ref, wo_ref, g_ref, b_ref, o_ref):
    x = x_ref[...]
    q = _dot(x.astype(BF16), wq_ref[...]).astype(BF16)
    kv = kv_ref[...]
    scale = 1.0 / math.sqrt(CROSS_HEAD_DIM)
    outs = []

    def scores(h):
        sl = slice(h * CROSS_HEAD_DIM, (h + 1) * CROSS_HEAD_DIM)
        return _dot_nt(q[:, sl], kv[:, sl]) * scale

    s_next = scores(0)
    for h in range(CROSS_HEADS):
        s = s_next
        if h + 1 < CROSS_HEADS:
            s_next = scores(h + 1)
        p = jnp.exp(s - jnp.max(s, axis=-1, keepdims=True))
        p = p / jnp.sum(p, axis=-1, keepdims=True)
        outs.append(_dot(p.astype(BF16), kv[:, CROSS_WIDTH + h * CROSS_HEAD_DIM:CROSS_WIDTH + (h + 1) * CROSS_HEAD_DIM]))
    o = jnp.concatenate(outs, axis=1).astype(BF16)
    c = _dot(o, wo_ref[...])
    o_ref[...] = _layer_norm(DEEPNORM_ALPHA * x + c, g_ref[...], b_ref[...])


def _cross_attn(x, kv, wq, wo, g, b, batch, seq):
    x3 = x.reshape(batch, seq, D_MODEL)
    kv3 = kv.reshape(batch, MEM_LEN, 2 * CROSS_WIDTH)
    return pl.pallas_call(
        _cross_kernel,
        grid=(batch, seq // CROSS_TILE),
        in_specs=[
            pl.BlockSpec((None, CROSS_TILE, D_MODEL), lambda b, i: (b, i, 0)),
            pl.BlockSpec((None, MEM_LEN, 2 * CROSS_WIDTH), lambda b, i: (b, 0, 0)),
            _resident((D_MODEL, CROSS_WIDTH)),
            _resident((CROSS_WIDTH, D_MODEL)),
            _resident((1, D_MODEL)), _resident((1, D_MODEL)),
        ],
        out_specs=pl.BlockSpec((None, CROSS_TILE, D_MODEL), lambda b, i: (b, i, 0)),
        out_shape=jax.ShapeDtypeStruct((batch, seq, D_MODEL), F32),
        compiler_params=_params("parallel", "parallel"),
        name="cross_attn",
    )(x3, kv3, wq, wo, g, b).reshape(batch * seq, D_MODEL)


def _row(v):
    return v.reshape(1, -1).astype(F32)


def kernel(x, mem, ffn1_w_in, ffn1_w_out, ffn1_ln_g, ffn1_ln_b, mix_w_in, swa_sinks, swa_proj, s5_a_re, s5_a_im, s5_log_step, s5_b_re, s5_b_im, s5_c_re, s5_c_im, s5_d, s5_glu_w, s5_glu_b, s5_proj, conv_w, conv_b, conv_ln_g, conv_ln_b, conv_proj, diff_lq1, diff_lk1, diff_lq2, diff_lk2, diff_norm_g, diff_proj, mix_w_out, mix_ln_g, mix_ln_b, mem_ln_g, mem_ln_b, cross_wq, cross_wkv, cross_wo, cross_ln_g, cross_ln_b, ffn2_w_in, ffn2_w_out, ffn2_ln_g, ffn2_ln_b):
    batch, seq, _ = x.shape
    n = batch * seq
    assert seq % DIFF_TILE == 0 and seq % ROW_TILE == 0 and seq % CROSS_TILE == 0
    assert seq % S5_STEP == 0 and seq % SWA_TILE == 0 and (batch * MEM_LEN) % ROW_TILE == 0
    h = x.reshape(n, D_MODEL)
    kv_all = _mem_kv(mem.reshape(batch * MEM_LEN, D_MODEL), _row(mem_ln_g), _row(mem_ln_b),
                     cross_wkv.astype(BF16))
    for l in range(DEPTH):
        h = _ffn_ln(h, ffn1_w_in[l].astype(BF16), ffn1_w_out[l].astype(BF16),
                    _row(ffn1_ln_g[l]), _row(ffn1_ln_b[l]))
        lambda_init = 0.8 - 0.6 * math.exp(-0.3 * l)
        w = mix_w_in[l]
        cols = lambda off, width: w[:, off:off + width]
        w_att = jnp.concatenate([cols(OFF_DQ, D_QK), cols(OFF_DK, D_QK), cols(OFF_AK, A_KV)], axis=1).astype(BF16)
        w_q = _swa_q_weight(cols(OFF_AQ, A_Q)).astype(BF16)
        w_v_t = jnp.concatenate([cols(OFF_DV, D_V), cols(OFF_AV, A_KV)], axis=1).T.astype(BF16)
        w_gate = cols(OFF_GL, GATE_COLS).astype(BF16)
        qk, ak, q_swa, su, v_t, o_c = _mix_proj(
            h, cols(OFF_CU, CONV_IN).astype(BF16), w_att, w_q, cols(OFF_SU, S5_WIDTH).astype(BF16), w_v_t,
            conv_w[l], _row(conv_b[l]), _row(conv_ln_g[l]), _row(conv_ln_b[l]), batch, seq)
        o_a = _swa(ak, q_swa, v_t, swa_sinks[l], batch, seq)
        tables = _s5_tables(s5_a_re[l], s5_a_im[l], s5_log_step[l], s5_b_re[l], s5_b_im[l], s5_c_re[l], s5_c_im[l])
        o_b = _s5(su, tables, _row(s5_d[l]), s5_glu_w[l].astype(BF16), _row(s5_glu_b[l]), batch, seq)
        lam = (jnp.exp(jnp.sum(diff_lq1[l] * diff_lk1[l])) - jnp.exp(jnp.sum(diff_lq2[l] * diff_lk2[l]))
               + lambda_init).astype(F32)
        o_d = _diff_attn(qk, v_t, jnp.stack([lam, jnp.float32(1.0 - lambda_init)]), _row(diff_norm_g[l]), batch, seq)
        h = _merge(h, o_a, o_b, o_c, o_d, w_gate, swa_proj[l].astype(BF16), s5_proj[l].astype(BF16),
                   conv_proj[l].astype(BF16), diff_proj[l].astype(BF16), mix_w_out[l].astype(BF16),
                   _row(mix_ln_g[l]), _row(mix_ln_b[l]))
        h = _cross_attn(h, kv_all[l], cross_wq[l].astype(BF16), cross_wo[l].astype(BF16),
                        _row(cross_ln_g[l]), _row(cross_ln_b[l]), batch, seq)
        h = _ffn_ln(h, ffn2_w_in[l].astype(BF16), ffn2_w_out[l].astype(BF16),
                    _row(ffn2_ln_g[l]), _row(ffn2_ln_b[l]))
    return h.reshape(batch, seq, D_MODEL)
```

```python
import functools
import math

import jax
import jax.numpy as jnp
from jax import lax
from jax.experimental import pallas as pl
from jax.experimental.pallas import tpu as pltpu

F32 = jnp.float32
BF16 = jnp.bfloat16

D_MODEL = 1024
DEPTH = 4
MEM_LEN = 256
HEAD_DIM = 64
BLOCK = 128
SWA_HEADS = 8
SWA_KV_HEADS = 2
SWA_REP = SWA_HEADS // SWA_KV_HEADS
S5_WIDTH = 512
S5_GROUP = 16
S5_GROUPS = S5_WIDTH // S5_GROUP
S5_STATE = 64
S5_NSTATE = S5_GROUPS * S5_STATE
CONV_WIDTH = 512
CONV_K = 31
DIFF_HEADS = 4
DIFF_V_DIM = 2 * HEAD_DIM
CROSS_HEADS = 4
CROSS_HEAD_DIM = 128
CROSS_WIDTH = CROSS_HEADS * CROSS_HEAD_DIM
FFN_DIM = 2816
N_BRANCHES = 4

A_Q = SWA_HEADS * HEAD_DIM
A_KV = SWA_KV_HEADS * HEAD_DIM
D_QK = DIFF_HEADS * 2 * HEAD_DIM
D_V = DIFF_HEADS * DIFF_V_DIM
CONV_IN = 2 * CONV_WIDTH
GATE_COLS = N_BRANCHES * D_MODEL
OFF_AQ = 0
OFF_AK = OFF_AQ + A_Q
OFF_AV = OFF_AK + A_KV
OFF_DQ = OFF_AV + A_KV
OFF_DK = OFF_DQ + D_QK
OFF_DV = OFF_DK + D_QK
OFF_SU = OFF_DV + D_V
OFF_CU = OFF_SU + S5_WIDTH
OFF_GL = OFF_CU + CONV_IN
ATT_COLS = 2 * D_QK + A_KV
VT_ROWS = D_V + A_KV

DEEPNORM_ALPHA = (2.0 * DEPTH) ** 0.25
LN_EPS = 1e-5
NEG_INF = -1e30

LANES = 128
SUBLANES = 8
MXU_WIDTH = 256
VMEM_LIMIT = 56 * 1024 * 1024

ROW_TILE = 1024
FFN_CHUNK = MXU_WIDTH
MERGE_ROWS = 1024
MERGE_PART = 256
FFN_ROWS = 1024
FFN_PART = 256
DIFF_TILE = 512
ONES_ROWS = 16
DIFF_VROWS = DIFF_V_DIM + ONES_ROWS
SWA_TILE = 1024
LOG2E = math.log2(math.e)
CONV_HALO = 32
S5_TILE = 256
S5_STEP = 1024
CROSS_TILE = 1024


def _alibi_slope(h, n):
    return 2.0 ** (-8.0 * (h + 1) / n)


def _layer_norm(z, g, b):
    mu = jnp.mean(z, axis=-1, keepdims=True)
    zc = z - mu
    var = jnp.mean(zc * zc, axis=-1, keepdims=True)
    return zc * lax.rsqrt(var + LN_EPS) * g + b


def _dot(a, b):
    return jnp.dot(a, b, preferred_element_type=F32)


def _dot_nt(a, b):
    return lax.dot_general(a, b, (((1,), (1,)), ((), ())), preferred_element_type=F32)


def _params(*sem):
    return pltpu.CompilerParams(dimension_semantics=sem, vmem_limit_bytes=VMEM_LIMIT)


def _resident(shape):
    nd = len(shape)
    return pl.BlockSpec(shape, lambda *_: (0,) * nd, pipeline_mode=pl.Buffered(1))


def _ffn_ln_kernel(x_ref, wg_ref, wu_ref, wo_ref, g_ref, b_ref, o_ref, a_ref):
    half = FFN_PART
    for r in range(FFN_ROWS // FFN_PART):
        rows = slice(r * half, (r + 1) * half)
        x = x_ref[rows, :]
        xb = x.astype(BF16)
        for c in range(FFN_DIM // FFN_CHUNK):
            sl = slice(c * FFN_CHUNK, (c + 1) * FFN_CHUNK)
            gate = _dot(xb, wg_ref[:, sl])
            up = _dot(xb, wu_ref[:, sl])
            a_ref[rows, sl] = (gate * jax.nn.sigmoid(gate) * up).astype(BF16)
        f = _dot(a_ref[rows, :], wo_ref[...])
        o_ref[rows, :] = _layer_norm(DEEPNORM_ALPHA * x + 0.5 * f, g_ref[...], b_ref[...])


def _ffn_ln(x, w_in, w_out, g, b):
    n = x.shape[0]
    return pl.pallas_call(
        _ffn_ln_kernel,
        grid=(n // FFN_ROWS,),
        in_specs=[
            pl.BlockSpec((FFN_ROWS, D_MODEL), lambda i: (i, 0)),
            pl.BlockSpec((D_MODEL, FFN_DIM), lambda i: (0, 0), pipeline_mode=pl.Buffered(1)),
            pl.BlockSpec((D_MODEL, FFN_DIM), lambda i: (0, 1), pipeline_mode=pl.Buffered(1)),
            _resident((FFN_DIM, D_MODEL)),
            _resident((1, D_MODEL)),
            _resident((1, D_MODEL)),
        ],
        out_specs=pl.BlockSpec((FFN_ROWS, D_MODEL), lambda i: (i, 0)),
        out_shape=jax.ShapeDtypeStruct((n, D_MODEL), F32),
        scratch_shapes=[pltpu.VMEM((FFN_ROWS, FFN_DIM), BF16)],
        compiler_params=_params("parallel"),
        name="ffn_ln",
    )(x, w_in, w_in, w_out, g, b)


def _mix_proj_kernel(x_ref, wc_ref, wa_ref, wq_ref, ws_ref, wvt_ref, cw_ref, cb_ref, cg_ref, cbeta_ref,
                     qk_ref, ak_ref, q_ref, su_ref, vt_ref, oc_ref, buf_ref, *, per_seq):
    i = pl.program_id(0)

    @pl.when(i == 0)
    def _():
        buf_ref[ROW_TILE:, :] = jnp.zeros((CONV_HALO, CONV_WIDTH), F32)

    xb = x_ref[...].astype(BF16)
    cu = _dot(xb, wc_ref[...])
    ha = _dot(xb, wa_ref[...])
    for j in range(2 * DIFF_HEADS):
        qk_ref[j] = ha[:, j * LANES:(j + 1) * LANES].astype(BF16)
    ak_ref[...] = ha[:, 2 * D_QK:].astype(BF16)
    q = _dot(xb, wq_ref[...])
    for h in range(SWA_HEADS):
        q_ref[h] = q[:, h * LANES:(h + 1) * LANES].astype(BF16)
    su_ref[...] = _dot(xb, ws_ref[...])
    vt_ref[...] = _dot_nt(wvt_ref[...], xb).astype(BF16)
    keep = (i % per_seq != 0).astype(F32)
    buf_ref[0:CONV_HALO, :] = buf_ref[ROW_TILE:, :] * keep
    buf_ref[CONV_HALO:, :] = cu[:, :CONV_WIDTH] * jax.nn.sigmoid(cu[:, CONV_WIDTH:])
    bufv = buf_ref[...]
    nrows = ROW_TILE + CONV_HALO
    shifted = [bufv] + [pltpu.roll(bufv, nrows - b, axis=0) for b in range(1, SUBLANES)]
    first = CONV_HALO - (CONV_K - 1)
    acc = jnp.zeros((ROW_TILE, CONV_WIDTH), F32)
    for k in range(CONV_K):
        a, b = divmod(first + k, SUBLANES)
        lo = a * SUBLANES
        acc = acc + cw_ref[k:k + 1, :] * shifted[b][lo:lo + ROW_TILE, :]
    y = _layer_norm(acc + cb_ref[...], cg_ref[...], cbeta_ref[...])
    oc_ref[...] = (y * jax.nn.sigmoid(y)).astype(BF16)


def _mix_proj(x, w_cu, w_att, w_q, w_su, w_v_t, conv_w, conv_b, conv_g, conv_beta, batch, seq):
    n = x.shape[0]
    per_seq = seq // ROW_TILE
    row = lambda w: pl.BlockSpec((ROW_TILE, w), lambda i: (i, 0))
    return pl.pallas_call(
        functools.partial(_mix_proj_kernel, per_seq=per_seq),
        grid=(n // ROW_TILE,),
        in_specs=[
            row(D_MODEL),
            _resident((D_MODEL, CONV_IN)),
            _resident((D_MODEL, ATT_COLS)),
            _resident((D_MODEL, SWA_HEADS * LANES)),
            _resident((D_MODEL, S5_WIDTH)),
            _resident((VT_ROWS, D_MODEL)),
            _resident((CONV_K, CONV_WIDTH)),
            _resident((1, CONV_WIDTH)), _resident((1, CONV_WIDTH)), _resident((1, CONV_WIDTH)),
        ],
        out_specs=[
            pl.BlockSpec((2 * DIFF_HEADS, ROW_TILE, LANES), lambda i: (0, i, 0)),
            row(A_KV),
            pl.BlockSpec((SWA_HEADS, ROW_TILE, LANES), lambda i: (0, i, 0)),
            row(S5_WIDTH),
            pl.BlockSpec((None, VT_ROWS, ROW_TILE), lambda i: (i // per_seq, 0, i % per_seq)),
            row(CONV_WIDTH),
        ],
        out_shape=[
            jax.ShapeDtypeStruct((2 * DIFF_HEADS, n, LANES), BF16),
            jax.ShapeDtypeStruct((n, A_KV), BF16),
            jax.ShapeDtypeStruct((SWA_HEADS, n, LANES), BF16),
            jax.ShapeDtypeStruct((n, S5_WIDTH), F32),
            jax.ShapeDtypeStruct((batch, VT_ROWS, seq), BF16),
            jax.ShapeDtypeStruct((n, CONV_WIDTH), BF16),
        ],
        scratch_shapes=[pltpu.VMEM((ROW_TILE + CONV_HALO, CONV_WIDTH), F32)],
        compiler_params=_params("arbitrary"),
        name="mix_proj_conv",
    )(x, w_cu, w_att, w_q, w_su, w_v_t, conv_w, conv_b, conv_g, conv_beta)


def _swa_q_weight(w_aq):
    d = w_aq.shape[0]
    w = (w_aq * (LOG2E / math.sqrt(HEAD_DIM))).reshape(d, SWA_KV_HEADS, SWA_REP, 1, HEAD_DIM)
    sel = jnp.eye(SWA_KV_HEADS, dtype=w.dtype).reshape(1, SWA_KV_HEADS, 1, SWA_KV_HEADS, 1)
    return (w * sel).reshape(d, SWA_HEADS * LANES)


def _swa_kernel(sink_ref, bias_ref, q_ref, kc_ref, kp_ref, vc_ref, vp_ref, o_ref):
    i = pl.program_id(1)
    gw = SWA_REP * BLOCK
    k_all = jnp.concatenate([kp_ref[...], kc_ref[...]], axis=0)
    v_all = jnp.concatenate([vp_ref[...], vc_ref[...]], axis=1)
    vx_all = jnp.concatenate([v_all, jnp.ones((ONES_ROWS, v_all.shape[1]), BF16)], axis=0)
    no_prev = jnp.where(i == 0, NEG_INF, 0.0)

    def scores(blk, g):
        lo = blk * BLOCK
        qg = q_ref[g * SWA_REP:(g + 1) * SWA_REP, lo:lo + BLOCK, :].reshape(gw, LANES)
        s = _dot_nt(k_all[lo:lo + 2 * BLOCK], qg) + bias_ref[g]
        if blk == 0:
            s = jnp.concatenate([s[:BLOCK] + no_prev, s[BLOCK:]], axis=0)
        return s

    units = [(blk, g) for blk in range(SWA_TILE // BLOCK) for g in range(SWA_KV_HEADS)]
    s_next = scores(*units[0])
    outs = []
    for u, (blk, g) in enumerate(units):
        s = s_next
        if u + 1 < len(units):
            s_next = scores(*units[u + 1])
        lo = blk * BLOCK
        sink = sink_ref[g]
        m = jnp.maximum(jnp.max(s, axis=0, keepdims=True), sink)
        e = jnp.exp2(s - m).astype(BF16)
        pv = _dot(vx_all[:, lo:lo + 2 * BLOCK], e)
        den = pv[LANES:LANES + 1] + jnp.exp2(sink - m)
        og = pv[g * HEAD_DIM:(g + 1) * HEAD_DIM] * (1.0 / den)
        outs += [og[:, r * BLOCK:(r + 1) * BLOCK] for r in range(SWA_REP)]
        if g == SWA_KV_HEADS - 1:
            o_ref[lo:lo + BLOCK, :] = jnp.concatenate(outs, axis=0).T.astype(BF16)
            outs = []


def _swa_bias():
    kj = jnp.arange(2 * BLOCK)[:, None]
    qi = jnp.arange(BLOCK)[None, :]
    dist = BLOCK + qi - kj
    valid = (dist >= 0) & (dist < BLOCK)
    slopes = jnp.asarray([_alibi_slope(h, SWA_HEADS) * LOG2E for h in range(SWA_HEADS)], F32)
    b = jnp.where(valid[None], -slopes[:, None, None] * dist[None].astype(F32), NEG_INF)
    b = b.reshape(SWA_KV_HEADS, SWA_REP, 2 * BLOCK, BLOCK)
    return jnp.transpose(b, (0, 2, 1, 3)).reshape(SWA_KV_HEADS, 2 * BLOCK, SWA_REP * BLOCK)


def _swa(ak, q_swa, v_t, sinks, batch, seq):
    h3 = ak.reshape(batch, seq, A_KV)
    rv = D_V // A_KV
    per_seq = seq // SWA_TILE
    r = SWA_TILE // BLOCK
    gw = SWA_REP * BLOCK
    sink_rows = jnp.repeat(sinks.astype(F32).reshape(SWA_KV_HEADS, SWA_REP) * LOG2E, BLOCK, axis=1)
    prev = lambda i: jnp.maximum(i * r - 1, 0)
    return pl.pallas_call(
        _swa_kernel,
        grid=(batch, per_seq),
        in_specs=[
            _resident((SWA_KV_HEADS, 1, gw)),
            _resident((SWA_KV_HEADS, 2 * BLOCK, gw)),
            pl.BlockSpec((SWA_HEADS, SWA_TILE, LANES), lambda b, i: (0, b * per_seq + i, 0)),
            pl.BlockSpec((None, SWA_TILE, A_KV), lambda b, i: (b, i, 0)),
            pl.BlockSpec((None, BLOCK, A_KV), lambda b, i: (b, prev(i), 0)),
            pl.BlockSpec((None, A_KV, SWA_TILE), lambda b, i: (b, rv, i)),
            pl.BlockSpec((None, A_KV, BLOCK), lambda b, i: (b, rv, prev(i))),
        ],
        out_specs=pl.BlockSpec((None, SWA_TILE, A_Q), lambda b, i: (b, i, 0)),
        out_shape=jax.ShapeDtypeStruct((batch, seq, A_Q), BF16),
        compiler_params=_params("parallel", "parallel"),
        name="swa",
    )(sink_rows.reshape(SWA_KV_HEADS, 1, gw), _swa_bias(), q_swa, h3, h3, v_t, v_t).reshape(batch * seq, A_Q)


def _diff_kernel(sc_ref, q_ref, k_ref, vt_ref, g_ref, o_ref,
                 qs_ref, vx_ref, b_ref, s_ref, mx_ref, m_ref, acc_ref, *, n_tiles):
    h = pl.program_id(1)
    t = DIFF_TILE
    hf = t // 2
    v_dim = DIFF_V_DIM
    slope = jnp.float32(_alibi_slope(DIFF_HEADS - 1, DIFF_HEADS) * LOG2E)
    for hh in range(DIFF_HEADS - 1):
        slope = jnp.where(h == hh, jnp.float32(_alibi_slope(hh, DIFF_HEADS) * LOG2E), slope)

    vx_ref[:v_dim, :] = vt_ref[...]
    vx_ref[v_dim:, :] = jnp.ones((DIFF_VROWS - v_dim, vx_ref.shape[1]), BF16)
    krow = lax.broadcasted_iota(jnp.int32, (t, t), 0)
    qcol = lax.broadcasted_iota(jnp.int32, (t, t), 1)
    b_ref[0] = slope * krow.astype(F32)
    b_ref[1] = jnp.where(qcol >= krow, slope * krow.astype(F32), NEG_INF)
    lane = lax.broadcasted_iota(jnp.int32, (t, 2 * HEAD_DIM), 1)
    for qt in range(n_tiles):
        q = q_ref[qt * t:(qt + 1) * t, :].astype(F32) * (LOG2E / math.sqrt(HEAD_DIM))
        qs_ref[qt, 0] = jnp.where(lane < HEAD_DIM, q, 0.0).astype(BF16)
        qs_ref[qt, 1] = jnp.where(lane >= HEAD_DIM, q, 0.0).astype(BF16)

    def items_of(qt):
        out = []
        for j in range(qt):
            out += [(qt, j * t, t, 0, t, None, c) for c in range(2)]
        out += [(qt, qt * t, hf, 0, t, (0, 0), c) for c in range(2)]
        out += [(qt, qt * t + hf, hf, hf, hf, (hf, hf), c) for c in range(2)]
        return out

    def scores(item):
        qt, r0, nr, c0, nc, tri, c = item
        bias = b_ref[0, 0:nr, 0:nc] if tri is None else b_ref[1, tri[0]:tri[0] + nr, tri[1]:tri[1] + nc]
        s = _dot_nt(k_ref[r0:r0 + nr, :], qs_ref[qt, c, c0:c0 + nc, :]) + bias
        s_ref[c, 0:nr, 0:nc] = s
        mx_ref[c, :, 0:nc] = jnp.max(s, axis=0, keepdims=True)

    def accumulate(item, first):
        qt, r0, nr, c0, nc, tri, c = item
        mx = mx_ref[c, :, 0:nc]
        e = jnp.exp2(s_ref[c, 0:nr, 0:nc] - mx).astype(BF16)
        pv = _dot(vx_ref[:, r0:r0 + nr], e)
        mxo = mx + slope * float(r0 - (0 if tri is None else tri[0]))
        if first:
            m_ref[c, :, c0:c0 + nc] = mxo
            acc_ref[c, :, c0:c0 + nc] = pv
        else:
            m_old = m_ref[c, :, c0:c0 + nc]
            m_new = jnp.maximum(m_old, mxo)
            m_ref[c, :, c0:c0 + nc] = m_new
            acc_ref[c, :, c0:c0 + nc] = (jnp.exp2(m_old - m_new) * acc_ref[c, :, c0:c0 + nc]
                                         + jnp.exp2(mxo - m_new) * pv)

    def finalize(qt):
        def normalised(c):
            a = acc_ref[c]
            return a[:v_dim] * (1.0 / a[v_dim:v_dim + 1])

        o = (normalised(0) - sc_ref[0] * normalised(1)).T
        o = o * lax.rsqrt(jnp.mean(o * o, axis=-1, keepdims=True) + LN_EPS) * g_ref[...] * sc_ref[1]
        o_ref[qt * t:(qt + 1) * t, :] = o.astype(BF16)

    items = [it for qt in range(n_tiles) for it in items_of(qt)]
    scores(items[0])
    for n, item in enumerate(items):
        if n + 1 < len(items):
            scores(items[n + 1])
        qt = item[0]
        accumulate(item, first=(n < 2 or items[n - 2][0] != qt))
        if n + 1 == len(items) or items[n + 1][0] != qt:
            finalize(qt)


def _diff_attn(qk, v_t, scalars, norm_g, batch, seq):
    h3 = qk.reshape(2 * DIFF_HEADS, batch, seq, LANES)
    w = 2 * HEAD_DIM
    t = DIFF_TILE
    n_tiles = seq // t
    return pl.pallas_call(
        functools.partial(_diff_kernel, n_tiles=n_tiles),
        grid=(batch, DIFF_HEADS),
        in_specs=[
            pl.BlockSpec(memory_space=pltpu.SMEM),
            pl.BlockSpec((None, None, seq, w), lambda b, h: (h, b, 0, 0)),
            pl.BlockSpec((None, None, seq, w), lambda b, h: (DIFF_HEADS + h, b, 0, 0)),
            pl.BlockSpec((None, DIFF_V_DIM, seq), lambda b, h: (b, h, 0)),
            _resident((1, DIFF_V_DIM)),
        ],
        out_specs=pl.BlockSpec((None, None, seq, w), lambda b, h: (h, b, 0, 0)),
        out_shape=jax.ShapeDtypeStruct((DIFF_HEADS, batch, seq, w), BF16),
        scratch_shapes=[
            pltpu.VMEM((n_tiles, 2, t, w), BF16),
            pltpu.VMEM((DIFF_VROWS, seq), BF16),
            pltpu.VMEM((2, t, t), F32),
            pltpu.VMEM((2, t, t), F32),
            pltpu.VMEM((2, 1, t), F32),
            pltpu.VMEM((2, 1, t), F32),
            pltpu.VMEM((2, DIFF_VROWS, t), F32),
        ],
        compiler_params=_params("parallel", "parallel"),
        name="diff_attn",
    )(scalars, h3, h3, v_t, norm_g).reshape(DIFF_HEADS, batch * seq, w)


def _s5_kernel(u_ref, bw_ref, cw_ref, air_ref, aii_ref, apr_ref, api_ref, ah_ref, tri_ref,
               d_ref, gw_ref, gb_ref, o_ref, hr_ref, hi_ref):
    i = pl.program_id(1)

    @pl.when(i == 0)
    def _():
        hr_ref[...] = jnp.zeros_like(hr_ref)
        hi_ref[...] = jnp.zeros_like(hi_ref)

    tri = tri_ref[...]
    nq = S5_WIDTH // LANES
    sw = S5_NSTATE // nq
    tt = S5_TILE

    def in_proj(k, j):
        ub = u_ref[k * tt:(k + 1) * tt, j * LANES:(j + 1) * LANES].astype(BF16)
        return _dot(ub, bw_ref[j])

    def finish(k, ys):
        u = u_ref[k * tt:(k + 1) * tt, :]
        y = jnp.concatenate(ys, axis=1) + d_ref[...] * u
        y = jax.nn.gelu(y, approximate=True)
        gate = _dot(y.astype(BF16), gw_ref[...]) + gb_ref[...]
        o_ref[k * tt:(k + 1) * tt, :] = (y * jax.nn.sigmoid(gate)).astype(BF16)

    units = [(k, j) for k in range(S5_STEP // tt) for j in range(nq)]
    bu_next = in_proj(*units[0])
    prev = None
    ys = []
    for n, (k, j) in enumerate(units):
        sl = slice(j * sw, (j + 1) * sw)
        bu = bu_next
        if n + 1 < len(units):
            bu_next = in_proj(*units[n + 1])
        bur, bui = bu[:, :sw], bu[:, sw:]
        air, aii = air_ref[:, sl], aii_ref[:, sl]
        zr = (air * bur - aii * bui).astype(BF16)
        zi = (air * bui + aii * bur).astype(BF16)
        cr = _dot(tri, zr) + hr_ref[:, sl]
        ci = _dot(tri, zi) + hi_ref[:, sl]
        if prev is not None:
            ys.append(_dot(prev[2], cw_ref[prev[1]]))
            if prev[1] == nq - 1:
                finish(prev[0], ys)
                ys = []
        apr, api = apr_ref[:, sl], api_ref[:, sl]
        xr = apr * cr - api * ci
        xi = apr * ci + api * cr
        lr, li = xr[tt - 1:, :], xi[tt - 1:, :]
        ahr, ahi = ah_ref[0:1, sl], ah_ref[1:2, sl]
        hr_ref[:, sl] = ahr * lr - ahi * li
        hi_ref[:, sl] = ahr * li + ahi * lr
        prev = (k, j, jnp.concatenate([xr, xi], axis=1).astype(BF16))
    ys.append(_dot(prev[2], cw_ref[prev[1]]))
    finish(prev[0], ys)


def _s5_tables(a_re, a_im, log_step, b_re, b_im, c_re, c_im):
    g, p, c = S5_GROUPS, S5_STATE, S5_GROUP
    step = jnp.exp(log_step)[:, None]
    mag = jnp.exp(a_re * step)
    abar_r, abar_i = mag * jnp.cos(a_im * step), mag * jnp.sin(a_im * step)
    den = a_re * a_re + a_im * a_im
    nr, ni = abar_r - 1.0, abar_i
    coef_r = (nr * a_re + ni * a_im) / den
    coef_i = (ni * a_re - nr * a_im) / den
    bbar_r = coef_r[..., None] * b_re - coef_i[..., None] * b_im
    bbar_i = coef_r[..., None] * b_im + coef_i[..., None] * b_re
    nq = S5_WIDTH // LANES
    gq = g // nq
    eye = jnp.eye(gq, dtype=F32)

    def in_slab(bb):
        bb = bb.reshape(nq, gq, p, c)
        return jnp.einsum('qgpc,gh->qgchp', bb, eye).reshape(nq, gq * c, gq * p)

    bw = jnp.concatenate([in_slab(bbar_r), in_slab(bbar_i)], axis=2).astype(BF16)

    def out_slab(cc):
        cc = cc.reshape(nq, gq, c, p)
        return jnp.einsum('qgcp,gh->qgphc', cc, eye).reshape(nq, gq * p, gq * c)

    cw = jnp.concatenate([out_slab(c_re), -out_slab(c_im)], axis=1).astype(BF16)
    half = S5_TILE // 2
    tt = jnp.arange(1 - half, S5_TILE + 1 - half, dtype=F32)[:, None]
    la = (a_re * step).reshape(1, g * p)
    th = (a_im * step).reshape(1, g * p)
    pm, ang = jnp.exp(tt * la), tt * th
    im_ = jnp.exp(-tt * la)
    apr, api = pm * jnp.cos(ang), pm * jnp.sin(ang)
    air, aii = im_ * jnp.cos(ang), -im_ * jnp.sin(ang)
    hm = jnp.exp(half * la)
    ah = jnp.concatenate([hm * jnp.cos(half * th), hm * jnp.sin(half * th)], axis=0)
    return bw, cw, air, aii, apr, api, ah


def _s5(su, tables, d_skip, glu_w, glu_b, batch, seq):
    bw, cw, air, aii, apr, api, ah = tables
    h3 = su.reshape(batch, seq, S5_WIDTH)
    tri = jnp.tril(jnp.ones((S5_TILE, S5_TILE), F32)).astype(BF16)
    tab = (S5_TILE, S5_NSTATE)
    return pl.pallas_call(
        _s5_kernel,
        grid=(batch, seq // S5_STEP),
        in_specs=[
            pl.BlockSpec((None, S5_STEP, S5_WIDTH), lambda b, i: (b, i, 0)),
            _resident(bw.shape), _resident(cw.shape),
            _resident(tab), _resident(tab), _resident(tab), _resident(tab),
            _resident((2, S5_NSTATE)),
            _resident((S5_TILE, S5_TILE)),
            _resident((1, S5_WIDTH)),
            _resident((S5_WIDTH, S5_WIDTH)),
            _resident((1, S5_WIDTH)),
        ],
        out_specs=pl.BlockSpec((None, S5_STEP, S5_WIDTH), lambda b, i: (b, i, 0)),
        out_shape=jax.ShapeDtypeStruct((batch, seq, S5_WIDTH), BF16),
        scratch_shapes=[pltpu.VMEM((1, S5_NSTATE), F32), pltpu.VMEM((1, S5_NSTATE), F32)],
        compiler_params=_params("parallel", "arbitrary"),
        name="s5_ssm",
    )(h3, bw, cw, air, aii, apr, api, ah, tri, d_skip, glu_w, glu_b).reshape(batch * seq, S5_WIDTH)


def _merge_kernel(x_ref, oa_ref, ob_ref, oc_ref, od_ref, wg_ref, pa_ref, pb_ref, pc_ref, pd_ref,
                  wo_ref, g_ref, b_ref, o_ref):
    half = MERGE_PART
    for r in range(MERGE_ROWS // MERGE_PART):
        rows = slice(r * half, (r + 1) * half)
        x = x_ref[rows, :]
        xb = x.astype(BF16)
        merged = jnp.zeros((half, D_MODEL), F32)
        od = jnp.concatenate([od_ref[hd, rows, :] for hd in range(DIFF_HEADS)], axis=1)
        branches = ((oa_ref[rows, :], pa_ref), (ob_ref[rows, :], pb_ref), (oc_ref[rows, :], pc_ref), (od, pd_ref))
        for i, (br, pr_ref) in enumerate(branches):
            gl = _dot(xb, wg_ref[:, i * D_MODEL:(i + 1) * D_MODEL])
            merged = merged + jax.nn.sigmoid(gl) * _dot(br, pr_ref[...])
        m = _dot(merged.astype(BF16), wo_ref[...])
        o_ref[rows, :] = _layer_norm(DEEPNORM_ALPHA * x + m, g_ref[...], b_ref[...])


def _merge(x, oa, ob, oc, od, w_gate, pa, pb, pc, pd, w_out, g, b):
    n = x.shape[0]
    row = lambda w: pl.BlockSpec((MERGE_ROWS, w), lambda i: (i, 0))
    return pl.pallas_call(
        _merge_kernel,
        grid=(n // MERGE_ROWS,),
        in_specs=[
            row(D_MODEL), row(A_Q), row(S5_WIDTH), row(CONV_WIDTH),
            pl.BlockSpec((DIFF_HEADS, MERGE_ROWS, DIFF_V_DIM), lambda i: (0, i, 0)),
            _resident((D_MODEL, GATE_COLS)),
            _resident((A_Q, D_MODEL)), _resident((S5_WIDTH, D_MODEL)),
            _resident((CONV_WIDTH, D_MODEL)), _resident((D_V, D_MODEL)),
            _resident((D_MODEL, D_MODEL)),
            _resident((1, D_MODEL)), _resident((1, D_MODEL)),
        ],
        out_specs=row(D_MODEL),
        out_shape=jax.ShapeDtypeStruct((n, D_MODEL), F32),
        compiler_params=_params("parallel"),
        name="gated_merge",
    )(x, oa, ob, oc, od, w_gate, pa, pb, pc, pd, w_out, g, b)


def _mem_kv_kernel(m_ref, g_ref, b_ref, w_ref, o_ref):
    mn = _layer_norm(m_ref[...], g_ref[...], b_ref[...])
    o_ref[...] = _dot(mn.astype(BF16), w_ref[...]).astype(BF16)


def _mem_kv(mem, g, b, wkv):
    n = mem.shape[0]
    return pl.pallas_call(
        _mem_kv_kernel,
        grid=(DEPTH, n // ROW_TILE),
        in_specs=[
            pl.BlockSpec((ROW_TILE, D_MODEL), lambda l, i: (i, 0)),
            _resident((1, D_MODEL)), _resident((1, D_MODEL)),
            pl.BlockSpec((None, D_MODEL, 2 * CROSS_WIDTH), lambda l, i: (l, 0, 0)),
        ],
        out_specs=pl.BlockSpec((None, ROW_TILE, 2 * CROSS_WIDTH), lambda l, i: (l, i, 0)),
        out_shape=jax.ShapeDtypeStruct((DEPTH, n, 2 * CROSS_WIDTH), BF16),
        compiler_params=_params("parallel", "parallel"),
        name="mem_kv",
    )(mem, g, b, wkv)


def _cross_kernel(x_ref, kv_ref, wq_ref, wo_ref, g_ref, b_ref, o_ref):
    x = x_ref[...]
    q = _dot(x.astype(BF16), wq_ref[...]).astype(BF16)
    kv = kv_ref[...]
    scale = 1.0 / math.sqrt(CROSS_HEAD_DIM)
    outs = []

    def scores(h):
        sl = slice(h * CROSS_HEAD_DIM, (h + 1) * CROSS_HEAD_DIM)
        return _dot_nt(q[:, sl], kv[:, sl]) * scale

    s_next = scores(0)
    for h in range(CROSS_HEADS):
        s = s_next
        if h + 1 < CROSS_HEADS:
            s_next = scores(h + 1)
        p = jnp.exp(s - jnp.max(s, axis=-1, keepdims=True))
        p = p / jnp.sum(p, axis=-1, keepdims=True)
        outs.append(_dot(p.astype(BF16), kv[:, CROSS_WIDTH + h * CROSS_HEAD_DIM:CROSS_WIDTH + (h + 1) * CROSS_HEAD_DIM]))
    o = jnp.concatenate(outs, axis=1).astype(BF16)
    c = _dot(o, wo_ref[...])
    o_ref[...] = _layer_norm(DEEPNORM_ALPHA * x + c, g_ref[...], b_ref[...])


def _cross_attn(x, kv, wq, wo, g, b, batch, seq):
    x3 = x.reshape(batch, seq, D_MODEL)
    kv3 = kv.reshape(batch, MEM_LEN, 2 * CROSS_WIDTH)
    return pl.pallas_call(
        _cross_kernel,
        grid=(batch, seq // CROSS_TILE),
        in_specs=[
            pl.BlockSpec((None, CROSS_TILE, D_MODEL), lambda b, i: (b, i, 0)),
            pl.BlockSpec((None, MEM_LEN, 2 * CROSS_WIDTH), lambda b, i: (b, 0, 0)),
            _resident((D_MODEL, CROSS_WIDTH)),
            _resident((CROSS_WIDTH, D_MODEL)),
            _resident((1, D_MODEL)), _resident((1, D_MODEL)),
        ],
        out_specs=pl.BlockSpec((None, CROSS_TILE, D_MODEL), lambda b, i: (b, i, 0)),
        out_shape=jax.ShapeDtypeStruct((batch, seq, D_MODEL), F32),
        compiler_params=_params("parallel", "parallel"),
        name="cross_attn",
    )(x3, kv3, wq, wo, g, b).reshape(batch * seq, D_MODEL)


def _row(v):
    return v.reshape(1, -1).astype(F32)


def kernel(x, mem, ffn1_w_in, ffn1_w_out, ffn1_ln_g, ffn1_ln_b, mix_w_in, swa_sinks, swa_proj, s5_a_re, s5_a_im, s5_log_step, s5_b_re, s5_b_im, s5_c_re, s5_c_im, s5_d, s5_glu_w, s5_glu_b, s5_proj, conv_w, conv_b, conv_ln_g, conv_ln_b, conv_proj, diff_lq1, diff_lk1, diff_lq2, diff_lk2, diff_norm_g, diff_proj, mix_w_out, mix_ln_g, mix_ln_b, mem_ln_g, mem_ln_b, cross_wq, cross_wkv, cross_wo, cross_ln_g, cross_ln_b, ffn2_w_in, ffn2_w_out, ffn2_ln_g, ffn2_ln_b):
    batch, seq, _ = x.shape
    n = batch * seq
    assert seq % DIFF_TILE == 0 and seq % ROW_TILE == 0 and seq % CROSS_TILE == 0
    assert seq % S5_STEP == 0 and seq % SWA_TILE == 0 and (batch * MEM_LEN) % ROW_TILE == 0
    h = x.reshape(n, D_MODEL)
    kv_all = _mem_kv(mem.reshape(batch * MEM_LEN, D_MODEL), _row(mem_ln_g), _row(mem_ln_b),
                     cross_wkv.astype(BF16))
    for l in range(DEPTH):
        h = _ffn_ln(h, ffn1_w_in[l].astype(BF16), ffn1_w_out[l].astype(BF16),
                    _row(ffn1_ln_g[l]), _row(ffn1_ln_b[l]))
        lambda_init = 0.8 - 0.6 * math.exp(-0.3 * l)
        w = mix_w_in[l]
        cols = lambda off, width: w[:, off:off + width]
        w_att = jnp.concatenate([cols(OFF_DQ, D_QK), cols(OFF_DK, D_QK), cols(OFF_AK, A_KV)], axis=1).astype(BF16)
        w_q = _swa_q_weight(cols(OFF_AQ, A_Q)).astype(BF16)
        w_v_t = jnp.concatenate([cols(OFF_DV, D_V), cols(OFF_AV, A_KV)], axis=1).T.astype(BF16)
        w_gate = cols(OFF_GL, GATE_COLS).astype(BF16)
        qk, ak, q_swa, su, v_t, o_c = _mix_proj(
            h, cols(OFF_CU, CONV_IN).astype(BF16), w_att, w_q, cols(OFF_SU, S5_WIDTH).astype(BF16), w_v_t,
            conv_w[l], _row(conv_b[l]), _row(conv_ln_g[l]), _row(conv_ln_b[l]), batch, seq)
        o_a = _swa(ak, q_swa, v_t, swa_sinks[l], batch, seq)
        tables = _s5_tables(s5_a_re[l], s5_a_im[l], s5_log_step[l], s5_b_re[l], s5_b_im[l], s5_c_re[l], s5_c_im[l])
        o_b = _s5(su, tables, _row(s5_d[l]), s5_glu_w[l].astype(BF16), _row(s5_glu_b[l]), batch, seq)
        lam = (jnp.exp(jnp.sum(diff_lq1[l] * diff_lk1[l])) - jnp.exp(jnp.sum(diff_lq2[l] * diff_lk2[l]))
               + lambda_init).astype(F32)
        o_d = _diff_attn(qk, v_t, jnp.stack([lam, jnp.float32(1.0 - lambda_init)]), _row(diff_norm_g[l]), batch, seq)
        h = _merge(h, o_a, o_b, o_c, o_d, w_gate, swa_proj[l].astype(BF16), s5_proj[l].astype(BF16),
                   conv_proj[l].astype(BF16), diff_proj[l].astype(BF16), mix_w_out[l].astype(BF16),
                   _row(mix_ln_g[l]), _row(mix_ln_b[l]))
        h = _cross_attn(h, kv_all[l], cross_wq[l].astype(BF16), cross_wo[l].astype(BF16),
                        _row(cross_ln_g[l]), _row(cross_ln_b[l]), batch, seq)
        h = _ffn_ln(h, ffn2_w_in[l].astype(BF16), ffn2_w_out[l].astype(BF16),
                    _row(ffn2_ln_g[l]), _row(ffn2_ln_b[l]))
    return h.reshape(batch, seq, D_MODEL)
```

```python
import functools
import math

import jax
import jax.numpy as jnp
from jax import lax
from jax.experimental import pallas as pl
from jax.experimental.pallas import tpu as pltpu

F32 = jnp.float32
BF16 = jnp.bfloat16

D_MODEL = 1024
DEPTH = 4
MEM_LEN = 256
HEAD_DIM = 64
BLOCK = 128
SWA_HEADS = 8
SWA_KV_HEADS = 2
SWA_REP = SWA_HEADS // SWA_KV_HEADS
S5_WIDTH = 512
S5_GROUP = 16
S5_GROUPS = S5_WIDTH // S5_GROUP
S5_STATE = 64
S5_NSTATE = S5_GROUPS * S5_STATE
CONV_WIDTH = 512
CONV_K = 31
DIFF_HEADS = 4
DIFF_V_DIM = 2 * HEAD_DIM
CROSS_HEADS = 4
CROSS_HEAD_DIM = 128
CROSS_WIDTH = CROSS_HEADS * CROSS_HEAD_DIM
FFN_DIM = 2816
N_BRANCHES = 4

A_Q = SWA_HEADS * HEAD_DIM
A_KV = SWA_KV_HEADS * HEAD_DIM
D_QK = DIFF_HEADS * 2 * HEAD_DIM
D_V = DIFF_HEADS * DIFF_V_DIM
CONV_IN = 2 * CONV_WIDTH
GATE_COLS = N_BRANCHES * D_MODEL
OFF_AQ = 0
OFF_AK = OFF_AQ + A_Q
OFF_AV = OFF_AK + A_KV
OFF_DQ = OFF_AV + A_KV
OFF_DK = OFF_DQ + D_QK
OFF_DV = OFF_DK + D_QK
OFF_SU = OFF_DV + D_V
OFF_CU = OFF_SU + S5_WIDTH
OFF_GL = OFF_CU + CONV_IN
ATT_COLS = 2 * D_QK + A_KV
VT_ROWS = D_V + A_KV

DEEPNORM_ALPHA = (2.0 * DEPTH) ** 0.25
LN_EPS = 1e-5
NEG_INF = -1e30

LANES = 128
SUBLANES = 8
MXU_WIDTH = 256
VMEM_LIMIT = 56 * 1024 * 1024

ROW_TILE = 1024
FFN_CHUNK = MXU_WIDTH
MERGE_ROWS = 1024
MERGE_PART = 256
FFN_ROWS = 1024
FFN_PART = 256
DIFF_TILE = 512
ONES_ROWS = 16
DIFF_VROWS = DIFF_V_DIM + ONES_ROWS
SWA_TILE = 1024
LOG2E = math.log2(math.e)
CONV_HALO = 32
S5_TILE = 256
S5_STEP = 1024
CROSS_TILE = 1024


def _alibi_slope(h, n):
    return 2.0 ** (-8.0 * (h + 1) / n)


def _layer_norm(z, g, b):
    mu = jnp.mean(z, axis=-1, keepdims=True)
    zc = z - mu
    var = jnp.mean(zc * zc, axis=-1, keepdims=True)
    return zc * lax.rsqrt(var + LN_EPS) * g + b


def _dot(a, b):
    return jnp.dot(a, b, preferred_element_type=F32)


def _dot_nt(a, b):
    return lax.dot_general(a, b, (((1,), (1,)), ((), ())), preferred_element_type=F32)


def _params(*sem):
    return pltpu.CompilerParams(dimension_semantics=sem, vmem_limit_bytes=VMEM_LIMIT)


def _resident(shape):
    nd = len(shape)
    return pl.BlockSpec(shape, lambda *_: (0,) * nd, pipeline_mode=pl.Buffered(1))


def _ffn_part(rows, x_ref, wg_ref, wu_ref, wo_ref, g_ref, b_ref, a_ref):
    x = x_ref[rows, :]
    xb = x.astype(BF16)
    for c in range(FFN_DIM // FFN_CHUNK):
        sl = slice(c * FFN_CHUNK, (c + 1) * FFN_CHUNK)
        gate = _dot(xb, wg_ref[:, sl])
        up = _dot(xb, wu_ref[:, sl])
        a_ref[rows, sl] = (gate * jax.nn.sigmoid(gate) * up).astype(BF16)
    f = _dot(a_ref[rows, :], wo_ref[...])
    return _layer_norm(DEEPNORM_ALPHA * x + 0.5 * f, g_ref[...], b_ref[...])


def _ffn_ln_kernel(x_ref, wg_ref, wu_ref, wo_ref, g_ref, b_ref, o_ref, a_ref):
    for r in range(FFN_ROWS // FFN_PART):
        rows = slice(r * FFN_PART, (r + 1) * FFN_PART)
        o_ref[rows, :] = _ffn_part(rows, x_ref, wg_ref, wu_ref, wo_ref, g_ref, b_ref, a_ref)


def _ffn_ln(x, w_in, w_out, g, b):
    n = x.shape[0]
    return pl.pallas_call(
        _ffn_ln_kernel,
        grid=(n // FFN_ROWS,),
        in_specs=[
            pl.BlockSpec((FFN_ROWS, D_MODEL), lambda i: (i, 0)),
            pl.BlockSpec((D_MODEL, FFN_DIM), lambda i: (0, 0), pipeline_mode=pl.Buffered(1)),
            pl.BlockSpec((D_MODEL, FFN_DIM), lambda i: (0, 1), pipeline_mode=pl.Buffered(1)),
            _resident((FFN_DIM, D_MODEL)),
            _resident((1, D_MODEL)),
            _resident((1, D_MODEL)),
        ],
        out_specs=pl.BlockSpec((FFN_ROWS, D_MODEL), lambda i: (i, 0)),
        out_shape=jax.ShapeDtypeStruct((n, D_MODEL), F32),
        scratch_shapes=[pltpu.VMEM((FFN_ROWS, FFN_DIM), BF16)],
        compiler_params=_params("parallel"),
        name="ffn_ln",
    )(x, w_in, w_in, w_out, g, b)


def _ffn_conv_kernel(x_ref, wg_ref, wu_ref, wo_ref, g_ref, b_ref, wc_ref, cw_ref, cb_ref, cg_ref, cbeta_ref,
                     o_ref, oc_ref, a_ref, buf_ref, *, per_seq):
    i = pl.program_id(0)

    @pl.when(i == 0)
    def _():
        buf_ref[FFN_ROWS:, :] = jnp.zeros((CONV_HALO, CONV_WIDTH), F32)

    keep = (i % per_seq != 0).astype(F32)
    buf_ref[0:CONV_HALO, :] = buf_ref[FFN_ROWS:, :] * keep
    first = CONV_HALO - (CONV_K - 1)
    part = FFN_PART
    for r in range(FFN_ROWS // part):
        rows = slice(r * part, (r + 1) * part)
        y = _ffn_part(rows, x_ref, wg_ref, wu_ref, wo_ref, g_ref, b_ref, a_ref)
        o_ref[rows, :] = y
        cu = _dot(y.astype(BF16), wc_ref[...])
        buf_ref[CONV_HALO + r * part:CONV_HALO + (r + 1) * part, :] = (
            cu[:, :CONV_WIDTH] * jax.nn.sigmoid(cu[:, CONV_WIDTH:]))
        win = buf_ref[r * part:r * part + part + CONV_HALO, :]
        shifted = [win] + [pltpu.roll(win, part + CONV_HALO - s, axis=0) for s in range(1, SUBLANES)]
        acc = jnp.zeros((part, CONV_WIDTH), F32)
        for k in range(CONV_K):
            a, s = divmod(first + k, SUBLANES)
            lo = a * SUBLANES
            acc = acc + cw_ref[k:k + 1, :] * shifted[s][lo:lo + part, :]
        z = _layer_norm(acc + cb_ref[...], cg_ref[...], cbeta_ref[...])
        oc_ref[rows, :] = (z * jax.nn.sigmoid(z)).astype(BF16)


def _ffn_ln_conv(x, w_in, w_out, g, b, w_cu, conv_w, conv_b, conv_g, conv_beta, seq):
    n = x.shape[0]
    row = lambda w: pl.BlockSpec((FFN_ROWS, w), lambda i: (i, 0))
    return pl.pallas_call(
        functools.partial(_ffn_conv_kernel, per_seq=seq // FFN_ROWS),
        grid=(n // FFN_ROWS,),
        in_specs=[
            row(D_MODEL),
            pl.BlockSpec((D_MODEL, FFN_DIM), lambda i: (0, 0), pipeline_mode=pl.Buffered(1)),
            pl.BlockSpec((D_MODEL, FFN_DIM), lambda i: (0, 1), pipeline_mode=pl.Buffered(1)),
            _resident((FFN_DIM, D_MODEL)),
            _resident((1, D_MODEL)), _resident((1, D_MODEL)),
            _resident((D_MODEL, CONV_IN)),
            _resident((CONV_K, CONV_WIDTH)),
            _resident((1, CONV_WIDTH)), _resident((1, CONV_WIDTH)), _resident((1, CONV_WIDTH)),
        ],
        out_specs=[row(D_MODEL), row(CONV_WIDTH)],
        out_shape=[jax.ShapeDtypeStruct((n, D_MODEL), F32), jax.ShapeDtypeStruct((n, CONV_WIDTH), BF16)],
        scratch_shapes=[pltpu.VMEM((FFN_ROWS, FFN_DIM), BF16),
                        pltpu.VMEM((FFN_ROWS + CONV_HALO, CONV_WIDTH), F32)],
        compiler_params=_params("arbitrary"),
        name="ffn_ln_conv",
    )(x, w_in, w_in, w_out, g, b, w_cu, conv_w, conv_b, conv_g, conv_beta)


def _mix_proj_kernel(x_ref, wa_ref, wq_ref, ws_ref, wvt_ref, qk_ref, ak_ref, q_ref, su_ref, vt_ref):
    xb = x_ref[...].astype(BF16)
    ha = _dot(xb, wa_ref[...])
    for j in range(2 * DIFF_HEADS):
        qk_ref[j] = ha[:, j * LANES:(j + 1) * LANES].astype(BF16)
    ak_ref[...] = ha[:, 2 * D_QK:].astype(BF16)
    q = _dot(xb, wq_ref[...])
    for h in range(SWA_HEADS):
        q_ref[h] = q[:, h * LANES:(h + 1) * LANES].astype(BF16)
    su_ref[...] = _dot(xb, ws_ref[...])
    vt_ref[...] = _dot_nt(wvt_ref[...], xb).astype(BF16)


def _mix_proj(x, w_att, w_q, w_su, w_v_t, batch, seq):
    n = x.shape[0]
    per_seq = seq // ROW_TILE
    row = lambda w: pl.BlockSpec((ROW_TILE, w), lambda i: (i, 0))
    return pl.pallas_call(
        _mix_proj_kernel,
        grid=(n // ROW_TILE,),
        in_specs=[
            row(D_MODEL),
            _resident((D_MODEL, ATT_COLS)),
            _resident((D_MODEL, SWA_HEADS * LANES)),
            _resident((D_MODEL, S5_WIDTH)),
            _resident((VT_ROWS, D_MODEL)),
        ],
        out_specs=[
            pl.BlockSpec((2 * DIFF_HEADS, ROW_TILE, LANES), lambda i: (0, i, 0)),
            row(A_KV),
            pl.BlockSpec((SWA_HEADS, ROW_TILE, LANES), lambda i: (0, i, 0)),
            row(S5_WIDTH),
            pl.BlockSpec((None, VT_ROWS, ROW_TILE), lambda i: (i // per_seq, 0, i % per_seq)),
        ],
        out_shape=[
            jax.ShapeDtypeStruct((2 * DIFF_HEADS, n, LANES), BF16),
            jax.ShapeDtypeStruct((n, A_KV), BF16),
            jax.ShapeDtypeStruct((SWA_HEADS, n, LANES), BF16),
            jax.ShapeDtypeStruct((n, S5_WIDTH), F32),
            jax.ShapeDtypeStruct((batch, VT_ROWS, seq), BF16),
        ],
        compiler_params=_params("parallel"),
        name="mix_proj",
    )(x, w_att, w_q, w_su, w_v_t)


def _swa_q_weight(w_aq):
    d = w_aq.shape[0]
    w = (w_aq * (LOG2E / math.sqrt(HEAD_DIM))).reshape(d, SWA_KV_HEADS, SWA_REP, 1, HEAD_DIM)
    sel = jnp.eye(SWA_KV_HEADS, dtype=w.dtype).reshape(1, SWA_KV_HEADS, 1, SWA_KV_HEADS, 1)
    return (w * sel).reshape(d, SWA_HEADS * LANES)


def _swa_kernel(sink_ref, bias_ref, q_ref, kc_ref, kp_ref, vc_ref, vp_ref, o_ref):
    i = pl.program_id(1)
    gw = SWA_REP * BLOCK
    k_all = jnp.concatenate([kp_ref[...], kc_ref[...]], axis=0)
    v_all = jnp.concatenate([vp_ref[...], vc_ref[...]], axis=1)
    vx_all = jnp.concatenate([v_all, jnp.ones((ONES_ROWS, v_all.shape[1]), BF16)], axis=0)
    no_prev = jnp.where(i == 0, NEG_INF, 0.0)

    def scores(blk, g):
        lo = blk * BLOCK
        qg = q_ref[g * SWA_REP:(g + 1) * SWA_REP, lo:lo + BLOCK, :].reshape(gw, LANES)
        s = _dot_nt(k_all[lo:lo + 2 * BLOCK], qg) + bias_ref[g]
        if blk == 0:
            s = jnp.concatenate([s[:BLOCK] + no_prev, s[BLOCK:]], axis=0)
        return s

    units = [(blk, g) for blk in range(SWA_TILE // BLOCK) for g in range(SWA_KV_HEADS)]
    s_next = scores(*units[0])
    outs = []
    for u, (blk, g) in enumerate(units):
        s = s_next
        if u + 1 < len(units):
            s_next = scores(*units[u + 1])
        lo = blk * BLOCK
        sink = sink_ref[g]
        m = jnp.maximum(jnp.max(s, axis=0, keepdims=True), sink)
        e = jnp.exp2(s - m).astype(BF16)
        pv = _dot(vx_all[:, lo:lo + 2 * BLOCK], e)
        den = pv[LANES:LANES + 1] + jnp.exp2(sink - m)
        og = pv[g * HEAD_DIM:(g + 1) * HEAD_DIM] * (1.0 / den)
        outs += [og[:, r * BLOCK:(r + 1) * BLOCK] for r in range(SWA_REP)]
        if g == SWA_KV_HEADS - 1:
            o_ref[lo:lo + BLOCK, :] = jnp.concatenate(outs, axis=0).T.astype(BF16)
            outs = []


def _swa_bias():
    kj = jnp.arange(2 * BLOCK)[:, None]
    qi = jnp.arange(BLOCK)[None, :]
    dist = BLOCK + qi - kj
    valid = (dist >= 0) & (dist < BLOCK)
    slopes = jnp.asarray([_alibi_slope(h, SWA_HEADS) * LOG2E for h in range(SWA_HEADS)], F32)
    b = jnp.where(valid[None], -slopes[:, None, None] * dist[None].astype(F32), NEG_INF)
    b = b.reshape(SWA_KV_HEADS, SWA_REP, 2 * BLOCK, BLOCK)
    return jnp.transpose(b, (0, 2, 1, 3)).reshape(SWA_KV_HEADS, 2 * BLOCK, SWA_REP * BLOCK)


def _swa(ak, q_swa, v_t, sinks, batch, seq):
    h3 = ak.reshape(batch, seq, A_KV)
    rv = D_V // A_KV
    per_seq = seq // SWA_TILE
    r = SWA_TILE // BLOCK
    gw = SWA_REP * BLOCK
    sink_rows = jnp.repeat(sinks.astype(F32).reshape(SWA_KV_HEADS, SWA_REP) * LOG2E, BLOCK, axis=1)
    prev = lambda i: jnp.maximum(i * r - 1, 0)
    return pl.pallas_call(
        _swa_kernel,
        grid=(batch, per_seq),
        in_specs=[
            _resident((SWA_KV_HEADS, 1, gw)),
            _resident((SWA_KV_HEADS, 2 * BLOCK, gw)),
            pl.BlockSpec((SWA_HEADS, SWA_TILE, LANES), lambda b, i: (0, b * per_seq + i, 0)),
            pl.BlockSpec((None, SWA_TILE, A_KV), lambda b, i: (b, i, 0)),
            pl.BlockSpec((None, BLOCK, A_KV), lambda b, i: (b, prev(i), 0)),
            pl.BlockSpec((None, A_KV, SWA_TILE), lambda b, i: (b, rv, i)),
            pl.BlockSpec((None, A_KV, BLOCK), lambda b, i: (b, rv, prev(i))),
        ],
        out_specs=pl.BlockSpec((None, SWA_TILE, A_Q), lambda b, i: (b, i, 0)),
        out_shape=jax.ShapeDtypeStruct((batch, seq, A_Q), BF16),
        compiler_params=_params("parallel", "parallel"),
        name="swa",
    )(sink_rows.reshape(SWA_KV_HEADS, 1, gw), _swa_bias(), q_swa, h3, h3, v_t, v_t).reshape(batch * seq, A_Q)


def _diff_kernel(sc_ref, q_ref, k_ref, vt_ref, g_ref, o_ref,
                 qs_ref, vx_ref, b_ref, s_ref, mx_ref, m_ref, acc_ref, *, n_tiles):
    h = pl.program_id(1)
    t = DIFF_TILE
    hf = t // 2
    v_dim = DIFF_V_DIM
    slope = jnp.float32(_alibi_slope(DIFF_HEADS - 1, DIFF_HEADS) * LOG2E)
    for hh in range(DIFF_HEADS - 1):
        slope = jnp.where(h == hh, jnp.float32(_alibi_slope(hh, DIFF_HEADS) * LOG2E), slope)

    vx_ref[:v_dim, :] = vt_ref[...]
    vx_ref[v_dim:, :] = jnp.ones((DIFF_VROWS - v_dim, vx_ref.shape[1]), BF16)
    krow = lax.broadcasted_iota(jnp.int32, (t, t), 0)
    qcol = lax.broadcasted_iota(jnp.int32, (t, t), 1)
    b_ref[0] = slope * krow.astype(F32)
    b_ref[1] = jnp.where(qcol >= krow, slope * krow.astype(F32), NEG_INF)
    lane = lax.broadcasted_iota(jnp.int32, (t, 2 * HEAD_DIM), 1)
    for qt in range(n_tiles):
        q = q_ref[qt * t:(qt + 1) * t, :].astype(F32) * (LOG2E / math.sqrt(HEAD_DIM))
        qs_ref[qt, 0] = jnp.where(lane < HEAD_DIM, q, 0.0).astype(BF16)
        qs_ref[qt, 1] = jnp.where(lane >= HEAD_DIM, q, 0.0).astype(BF16)

    def items_of(qt):
        out = []
        for j in range(qt):
            out += [(qt, j * t, t, 0, t, None, c) for c in range(2)]
        out += [(qt, qt * t, hf, 0, t, (0, 0), c) for c in range(2)]
        out += [(qt, qt * t + hf, hf, hf, hf, (hf, hf), c) for c in range(2)]
        return out

    def scores(item):
        qt, r0, nr, c0, nc, tri, c = item
        bias = b_ref[0, 0:nr, 0:nc] if tri is None else b_ref[1, tri[0]:tri[0] + nr, tri[1]:tri[1] + nc]
        s = _dot_nt(k_ref[r0:r0 + nr, :], qs_ref[qt, c, c0:c0 + nc, :]) + bias
        s_ref[c, 0:nr, 0:nc] = s
        mx_ref[c, :, 0:nc] = jnp.max(s, axis=0, keepdims=True)

    def accumulate(item, first):
        qt, r0, nr, c0, nc, tri, c = item
        mx = mx_ref[c, :, 0:nc]
        e = jnp.exp2(s_ref[c, 0:nr, 0:nc] - mx).astype(BF16)
        pv = _dot(vx_ref[:, r0:r0 + nr], e)
        mxo = mx + slope * float(r0 - (0 if tri is None else tri[0]))
        if first:
            m_ref[c, :, c0:c0 + nc] = mxo
            acc_ref[c, :, c0:c0 + nc] = pv
        else:
            m_old = m_ref[c, :, c0:c0 + nc]
            m_new = jnp.maximum(m_old, mxo)
            m_ref[c, :, c0:c0 + nc] = m_new
            acc_ref[c, :, c0:c0 + nc] = (jnp.exp2(m_old - m_new) * acc_ref[c, :, c0:c0 + nc]
                                         + jnp.exp2(mxo - m_new) * pv)

    def finalize(qt):
        def normalised(c):
            a = acc_ref[c]
            return a[:v_dim] * (1.0 / a[v_dim:v_dim + 1])

        o = (normalised(0) - sc_ref[0] * normalised(1)).T
        o = o * lax.rsqrt(jnp.mean(o * o, axis=-1, keepdims=True) + LN_EPS) * g_ref[...] * sc_ref[1]
        o_ref[qt * t:(qt + 1) * t, :] = o.astype(BF16)

    items = [it for qt in range(n_tiles) for it in items_of(qt)]
    scores(items[0])
    for n, item in enumerate(items):
        if n + 1 < len(items):
            scores(items[n + 1])
        qt = item[0]
        accumulate(item, first=(n < 2 or items[n - 2][0] != qt))
        if n + 1 == len(items) or items[n + 1][0] != qt:
            finalize(qt)


def _diff_attn(qk, v_t, scalars, norm_g, batch, seq):
    h3 = qk.reshape(2 * DIFF_HEADS, batch, seq, LANES)
    w = 2 * HEAD_DIM
    t = DIFF_TILE
    n_tiles = seq // t
    return pl.pallas_call(
        functools.partial(_diff_kernel, n_tiles=n_tiles),
        grid=(batch, DIFF_HEADS),
        in_specs=[
            pl.BlockSpec(memory_space=pltpu.SMEM),
            pl.BlockSpec((None, None, seq, w), lambda b, h: (h, b, 0, 0)),
            pl.BlockSpec((None, None, seq, w), lambda b, h: (DIFF_HEADS + h, b, 0, 0)),
            pl.BlockSpec((None, DIFF_V_DIM, seq), lambda b, h: (b, h, 0)),
            _resident((1, DIFF_V_DIM)),
        ],
        out_specs=pl.BlockSpec((None, None, seq, w), lambda b, h: (h, b, 0, 0)),
        out_shape=jax.ShapeDtypeStruct((DIFF_HEADS, batch, seq, w), BF16),
        scratch_shapes=[
            pltpu.VMEM((n_tiles, 2, t, w), BF16),
            pltpu.VMEM((DIFF_VROWS, seq), BF16),
            pltpu.VMEM((2, t, t), F32),
            pltpu.VMEM((2, t, t), F32),
            pltpu.VMEM((2, 1, t), F32),
            pltpu.VMEM((2, 1, t), F32),
            pltpu.VMEM((2, DIFF_VROWS, t), F32),
        ],
        compiler_params=_params("parallel", "parallel"),
        name="diff_attn",
    )(scalars, h3, h3, v_t, norm_g).reshape(DIFF_HEADS, batch * seq, w)


def _s5_kernel(u_ref, bw_ref, cw_ref, air_ref, aii_ref, apr_ref, api_ref, ah_ref, tri_ref,
               d_ref, gw_ref, gb_ref, o_ref, hr_ref, hi_ref):
    i = pl.program_id(1)

    @pl.when(i == 0)
    def _():
        hr_ref[...] = jnp.zeros_like(hr_ref)
        hi_ref[...] = jnp.zeros_like(hi_ref)

    tri = tri_ref[...]
    nq = S5_WIDTH // LANES
    sw = S5_NSTATE // nq
    tt = S5_TILE

    def in_proj(k, j):
        ub = u_ref[k * tt:(k + 1) * tt, j * LANES:(j + 1) * LANES].astype(BF16)
        return _dot(ub, bw_ref[j])

    def finish(k, ys):
        u = u_ref[k * tt:(k + 1) * tt, :]
        y = jnp.concatenate(ys, axis=1) + d_ref[...] * u
        y = jax.nn.gelu(y, approximate=True)
        gate = _dot(y.astype(BF16), gw_ref[...]) + gb_ref[...]
        o_ref[k * tt:(k + 1) * tt, :] = (y * jax.nn.sigmoid(gate)).astype(BF16)

    units = [(k, j) for k in range(S5_STEP // tt) for j in range(nq)]
    bu_next = in_proj(*units[0])
    prev = None
    ys = []
    for n, (k, j) in enumerate(units):
        sl = slice(j * sw, (j + 1) * sw)
        bu = bu_next
        if n + 1 < len(units):
            bu_next = in_proj(*units[n + 1])
        bur, bui = bu[:, :sw], bu[:, sw:]
        air, aii = air_ref[:, sl], aii_ref[:, sl]
        zr = (air * bur - aii * bui).astype(BF16)
        zi = (air * bui + aii * bur).astype(BF16)
        cr = _dot(tri, zr) + hr_ref[:, sl]
        ci = _dot(tri, zi) + hi_ref[:, sl]
        if prev is not None:
            ys.append(_dot(prev[2], cw_ref[prev[1]]))
            if prev[1] == nq - 1:
                finish(prev[0], ys)
                ys = []
        apr, api = apr_ref[:, sl], api_ref[:, sl]
        xr = apr * cr - api * ci
        xi = apr * ci + api * cr
        lr, li = xr[tt - 1:, :], xi[tt - 1:, :]
        ahr, ahi = ah_ref[0:1, sl], ah_ref[1:2, sl]
        hr_ref[:, sl] = ahr * lr - ahi * li
        hi_ref[:, sl] = ahr * li + ahi * lr
        prev = (k, j, jnp.concatenate([xr, xi], axis=1).astype(BF16))
    ys.append(_dot(prev[2], cw_ref[prev[1]]))
    finish(prev[0], ys)


def _s5_tables(a_re, a_im, log_step, b_re, b_im, c_re, c_im):
    g, p, c = S5_GROUPS, S5_STATE, S5_GROUP
    step = jnp.exp(log_step)[:, None]
    mag = jnp.exp(a_re * step)
    abar_r, abar_i = mag * jnp.cos(a_im * step), mag * jnp.sin(a_im * step)
    den = a_re * a_re + a_im * a_im
    nr, ni = abar_r - 1.0, abar_i
    coef_r = (nr * a_re + ni * a_im) / den
    coef_i = (ni * a_re - nr * a_im) / den
    bbar_r = coef_r[..., None] * b_re - coef_i[..., None] * b_im
    bbar_i = coef_r[..., None] * b_im + coef_i[..., None] * b_re
    nq = S5_WIDTH // LANES
    gq = g // nq
    eye = jnp.eye(gq, dtype=F32)

    def in_slab(bb):
        bb = bb.reshape(nq, gq, p, c)
        return jnp.einsum('qgpc,gh->qgchp', bb, eye).reshape(nq, gq * c, gq * p)

    bw = jnp.concatenate([in_slab(bbar_r), in_slab(bbar_i)], axis=2).astype(BF16)

    def out_slab(cc):
        cc = cc.reshape(nq, gq, c, p)
        return jnp.einsum('qgcp,gh->qgphc', cc, eye).reshape(nq, gq * p, gq * c)

    cw = jnp.concatenate([out_slab(c_re), -out_slab(c_im)], axis=1).astype(BF16)
    half = S5_TILE // 2
    tt = jnp.arange(1 - half, S5_TILE + 1 - half, dtype=F32)[:, None]
    la = (a_re * step).reshape(1, g * p)
    th = (a_im * step).reshape(1, g * p)
    pm, ang = jnp.exp(tt * la), tt * th
    im_ = jnp.exp(-tt * la)
    apr, api = pm * jnp.cos(ang), pm * jnp.sin(ang)
    air, aii = im_ * jnp.cos(ang), -im_ * jnp.sin(ang)
    hm = jnp.exp(half * la)
    ah = jnp.concatenate([hm * jnp.cos(half * th), hm * jnp.sin(half * th)], axis=0)
    return bw, cw, air, aii, apr, api, ah


def _s5(su, tables, d_skip, glu_w, glu_b, batch, seq):
    bw, cw, air, aii, apr, api, ah = tables
    h3 = su.reshape(batch, seq, S5_WIDTH)
    tri = jnp.tril(jnp.ones((S5_TILE, S5_TILE), F32)).astype(BF16)
    tab = (S5_TILE, S5_NSTATE)
    return pl.pallas_call(
        _s5_kernel,
        grid=(batch, seq // S5_STEP),
        in_specs=[
            pl.BlockSpec((None, S5_STEP, S5_WIDTH), lambda b, i: (b, i, 0)),
            _resident(bw.shape), _resident(cw.shape),
            _resident(tab), _resident(tab), _resident(tab), _resident(tab),
            _resident((2, S5_NSTATE)),
            _resident((S5_TILE, S5_TILE)),
            _resident((1, S5_WIDTH)),
            _resident((S5_WIDTH, S5_WIDTH)),
            _resident((1, S5_WIDTH)),
        ],
        out_specs=pl.BlockSpec((None, S5_STEP, S5_WIDTH), lambda b, i: (b, i, 0)),
        out_shape=jax.ShapeDtypeStruct((batch, seq, S5_WIDTH), BF16),
        scratch_shapes=[pltpu.VMEM((1, S5_NSTATE), F32), pltpu.VMEM((1, S5_NSTATE), F32)],
        compiler_params=_params("parallel", "arbitrary"),
        name="s5_ssm",
    )(h3, bw, cw, air, aii, apr, api, ah, tri, d_skip, glu_w, glu_b).reshape(batch * seq, S5_WIDTH)


def _merge_kernel(x_ref, oa_ref, ob_ref, oc_ref, od_ref, wg_ref, pa_ref, pb_ref, pc_ref, pd_ref,
                  wo_ref, g_ref, b_ref, o_ref):
    half = MERGE_PART
    for r in range(MERGE_ROWS // MERGE_PART):
        rows = slice(r * half, (r + 1) * half)
        x = x_ref[rows, :]
        xb = x.astype(BF16)
        merged = jnp.zeros((half, D_MODEL), F32)
        od = jnp.concatenate([od_ref[hd, rows, :] for hd in range(DIFF_HEADS)], axis=1)
        branches = ((oa_ref[rows, :], pa_ref), (ob_ref[rows, :], pb_ref), (oc_ref[rows, :], pc_ref), (od, pd_ref))
        for i, (br, pr_ref) in enumerate(branches):
            gl = _dot(xb, wg_ref[:, i * D_MODEL:(i + 1) * D_MODEL])
            merged = merged + jax.nn.sigmoid(gl) * _dot(br, pr_ref[...])
        m = _dot(merged.astype(BF16), wo_ref[...])
        o_ref[rows, :] = _layer_norm(DEEPNORM_ALPHA * x + m, g_ref[...], b_ref[...])


def _merge(x, oa, ob, oc, od, w_gate, pa, pb, pc, pd, w_out, g, b):
    n = x.shape[0]
    row = lambda w: pl.BlockSpec((MERGE_ROWS, w), lambda i: (i, 0))
    return pl.pallas_call(
        _merge_kernel,
        grid=(n // MERGE_ROWS,),
        in_specs=[
            row(D_MODEL), row(A_Q), row(S5_WIDTH), row(CONV_WIDTH),
            pl.BlockSpec((DIFF_HEADS, MERGE_ROWS, DIFF_V_DIM), lambda i: (0, i, 0)),
            _resident((D_MODEL, GATE_COLS)),
            _resident((A_Q, D_MODEL)), _resident((S5_WIDTH, D_MODEL)),
            _resident((CONV_WIDTH, D_MODEL)), _resident((D_V, D_MODEL)),
            _resident((D_MODEL, D_MODEL)),
            _resident((1, D_MODEL)), _resident((1, D_MODEL)),
        ],
        out_specs=row(D_MODEL),
        out_shape=jax.ShapeDtypeStruct((n, D_MODEL), F32),
        compiler_params=_params("parallel"),
        name="gated_merge",
    )(x, oa, ob, oc, od, w_gate, pa, pb, pc, pd, w_out, g, b)


def _mem_kv_kernel(m_ref, g_ref, b_ref, w_ref, o_ref):
    mn = _layer_norm(m_ref[...], g_ref[...], b_ref[...])
    o_ref[...] = _dot(mn.astype(BF16), w_ref[...]).astype(BF16)


def _mem_kv(mem, g, b, wkv):
    n = mem.shape[0]
    return pl.pallas_call(
        _mem_kv_kernel,
        grid=(DEPTH, n // ROW_TILE),
        in_specs=[
            pl.BlockSpec((ROW_TILE, D_MODEL), lambda l, i: (i, 0)),
            _resident((1, D_MODEL)), _resident((1, D_MODEL)),
            pl.BlockSpec((None, D_MODEL, 2 * CROSS_WIDTH), lambda l, i: (l, 0, 0)),
        ],
        out_specs=pl.BlockSpec((None, ROW_TILE, 2 * CROSS_WIDTH), lambda l, i: (l, i, 0)),
        out_shape=jax.ShapeDtypeStruct((DEPTH, n, 2 * CROSS_WIDTH), BF16),
        compiler_params=_params("parallel", "parallel"),
        name="mem_kv",
    )(mem, g, b, wkv)


def _cross_kernel(x_ref, kv_ref, wq_ref, wo_ref, g_ref, b_ref, o_ref):
    x = x_ref[...]
    q = _dot(x.astype(BF16), wq_ref[...]).astype(BF16)
    kv = kv_ref[...]
    scale = 1.0 / math.sqrt(CROSS_HEAD_DIM)
    outs = []

    def scores(h):
        sl = slice(h * CROSS_HEAD_DIM, (h + 1) * CROSS_HEAD_DIM)
        return _dot_nt(q[:, sl], kv[:, sl]) * scale

    s_next = scores(0)
    for h in range(CROSS_HEADS):
        s = s_next
        if h + 1 < CROSS_HEADS:
            s_next = scores(h + 1)
        p = jnp.exp(s - jnp.max(s, axis=-1, keepdims=True))
        p = p / jnp.sum(p, axis=-1, keepdims=True)
        outs.append(_dot(p.astype(BF16), kv[:, CROSS_WIDTH + h * CROSS_HEAD_DIM:CROSS_WIDTH + (h + 1) * CROSS_HEAD_DIM]))
    o = jnp.concatenate(outs, axis=1).astype(BF16)
    c = _dot(o, wo_ref[...])
    o_ref[...] = _layer_norm(DEEPNORM_ALPHA * x + c, g_ref[...], b_ref[...])


def _cross_attn(x, kv, wq, wo, g, b, batch, seq):
    x3 = x.reshape(batch, seq, D_MODEL)
    kv3 = kv.reshape(batch, MEM_LEN, 2 * CROSS_WIDTH)
    return pl.pallas_call(
        _cross_kernel,
        grid=(batch, seq // CROSS_TILE),
        in_specs=[
            pl.BlockSpec((None, CROSS_TILE, D_MODEL), lambda b, i: (b, i, 0)),
            pl.BlockSpec((None, MEM_LEN, 2 * CROSS_WIDTH), lambda b, i: (b, 0, 0)),
            _resident((D_MODEL, CROSS_WIDTH)),
            _resident((CROSS_WIDTH, D_MODEL)),
            _resident((1, D_MODEL)), _resident((1, D_MODEL)),
        ],
        out_specs=pl.BlockSpec((None, CROSS_TILE, D_MODEL), lambda b, i: (b, i, 0)),
        out_shape=jax.ShapeDtypeStruct((batch, seq, D_MODEL), F32),
        compiler_params=_params("parallel", "parallel"),
        name="cross_attn",
    )(x3, kv3, wq, wo, g, b).reshape(batch * seq, D_MODEL)


def _row(v):
    return v.reshape(1, -1).astype(F32)


def kernel(x, mem, ffn1_w_in, ffn1_w_out, ffn1_ln_g, ffn1_ln_b, mix_w_in, swa_sinks, swa_proj, s5_a_re, s5_a_im, s5_log_step, s5_b_re, s5_b_im, s5_c_re, s5_c_im, s5_d, s5_glu_w, s5_glu_b, s5_proj, conv_w, conv_b, conv_ln_g, conv_ln_b, conv_proj, diff_lq1, diff_lk1, diff_lq2, diff_lk2, diff_norm_g, diff_proj, mix_w_out, mix_ln_g, mix_ln_b, mem_ln_g, mem_ln_b, cross_wq, cross_wkv, cross_wo, cross_ln_g, cross_ln_b, ffn2_w_in, ffn2_w_out, ffn2_ln_g, ffn2_ln_b):
    batch, seq, _ = x.shape
    n = batch * seq
    assert seq % DIFF_TILE == 0 and seq % ROW_TILE == 0 and seq % CROSS_TILE == 0
    assert seq % S5_STEP == 0 and seq % SWA_TILE == 0 and (batch * MEM_LEN) % ROW_TILE == 0
    h = x.reshape(n, D_MODEL)
    kv_all = _mem_kv(mem.reshape(batch * MEM_LEN, D_MODEL), _row(mem_ln_g), _row(mem_ln_b),
                     cross_wkv.astype(BF16))
    for l in range(DEPTH):
        h, o_c = _ffn_ln_conv(h, ffn1_w_in[l].astype(BF16), ffn1_w_out[l].astype(BF16),
                               _row(ffn1_ln_g[l]), _row(ffn1_ln_b[l]),
                               mix_w_in[l][:, OFF_CU:OFF_CU + CONV_IN].astype(BF16), conv_w[l], _row(conv_b[l]),
                               _row(conv_ln_g[l]), _row(conv_ln_b[l]), seq)
        lambda_init = 0.8 - 0.6 * math.exp(-0.3 * l)
        w = mix_w_in[l]
        cols = lambda off, width: w[:, off:off + width]
        w_att = jnp.concatenate([cols(OFF_DQ, D_QK), cols(OFF_DK, D_QK), cols(OFF_AK, A_KV)], axis=1).astype(BF16)
        w_q = _swa_q_weight(cols(OFF_AQ, A_Q)).astype(BF16)
        w_v_t = jnp.concatenate([cols(OFF_DV, D_V), cols(OFF_AV, A_KV)], axis=1).T.astype(BF16)
        w_gate = cols(OFF_GL, GATE_COLS).astype(BF16)
        qk, ak, q_swa, su, v_t = _mix_proj(h, w_att, w_q, cols(OFF_SU, S5_WIDTH).astype(BF16), w_v_t, batch, seq)
        o_a = _swa(ak, q_swa, v_t, swa_sinks[l], batch, seq)
        tables = _s5_tables(s5_a_re[l], s5_a_im[l], s5_log_step[l], s5_b_re[l], s5_b_im[l], s5_c_re[l], s5_c_im[l])
        o_b = _s5(su, tables, _row(s5_d[l]), s5_glu_w[l].astype(BF16), _row(s5_glu_b[l]), batch, seq)
        lam = (jnp.exp(jnp.sum(diff_lq1[l] * diff_lk1[l])) - jnp.exp(jnp.sum(diff_lq2[l] * diff_lk2[l]))
               + lambda_init).astype(F32)
        o_d = _diff_attn(qk, v_t, jnp.stack([lam, jnp.float32(1.0 - lambda_init)]), _row(diff_norm_g[l]), batch, seq)
        h = _merge(h, o_a, o_b, o_c, o_d, w_gate, swa_proj[l].astype(BF16), s5_proj[l].astype(BF16),
                   conv_proj[l].astype(BF16), diff_proj[l].astype(BF16), mix_w_out[l].astype(BF16),
                   _row(mix_ln_g[l]), _row(mix_ln_b[l]))
        h = _cross_attn(h, kv_all[l], cross_wq[l].astype(BF16), cross_wo[l].astype(BF16),
                        _row(cross_ln_g[l]), _row(cross_ln_b[l]), batch, seq)
        h = _ffn_ln(h, ffn2_w_in[l].astype(BF16), ffn2_w_out[l].astype(BF16),
                    _row(ffn2_ln_g[l]), _row(ffn2_ln_b[l]))
    return h.reshape(batch, seq, D_MODEL)
```

```python
import functools
import math

import jax
import jax.numpy as jnp
from jax import lax
from jax.experimental import pallas as pl
from jax.experimental.pallas import tpu as pltpu

F32 = jnp.float32
BF16 = jnp.bfloat16

D_MODEL = 1024
DEPTH = 4
MEM_LEN = 256
HEAD_DIM = 64
BLOCK = 128
SWA_HEADS = 8
SWA_KV_HEADS = 2
SWA_REP = SWA_HEADS // SWA_KV_HEADS
S5_WIDTH = 512
S5_GROUP = 16
S5_GROUPS = S5_WIDTH // S5_GROUP
S5_STATE = 64
S5_NSTATE = S5_GROUPS * S5_STATE
CONV_WIDTH = 512
CONV_K = 31
DIFF_HEADS = 4
DIFF_V_DIM = 2 * HEAD_DIM
CROSS_HEADS = 4
CROSS_HEAD_DIM = 128
CROSS_WIDTH = CROSS_HEADS * CROSS_HEAD_DIM
FFN_DIM = 2816
N_BRANCHES = 4

A_Q = SWA_HEADS * HEAD_DIM
A_KV = SWA_KV_HEADS * HEAD_DIM
D_QK = DIFF_HEADS * 2 * HEAD_DIM
D_V = DIFF_HEADS * DIFF_V_DIM
CONV_IN = 2 * CONV_WIDTH
GATE_COLS = N_BRANCHES * D_MODEL
OFF_AQ = 0
OFF_AK = OFF_AQ + A_Q
OFF_AV = OFF_AK + A_KV
OFF_DQ = OFF_AV + A_KV
OFF_DK = OFF_DQ + D_QK
OFF_DV = OFF_DK + D_QK
OFF_SU = OFF_DV + D_V
OFF_CU = OFF_SU + S5_WIDTH
OFF_GL = OFF_CU + CONV_IN
ATT_COLS = 2 * D_QK + A_KV
VT_ROWS = D_V + A_KV

DEEPNORM_ALPHA = (2.0 * DEPTH) ** 0.25
LN_EPS = 1e-5
NEG_INF = -1e30

LANES = 128
SUBLANES = 8
MXU_WIDTH = 256
VMEM_LIMIT = 56 * 1024 * 1024

ROW_TILE = 1024
FFN_CHUNK = MXU_WIDTH
MERGE_ROWS = 1024
MERGE_PART = 256
FFN_ROWS = 1024
FFN_PART = 256
DIFF_TILE = 512
ONES_ROWS = 16
DIFF_VROWS = DIFF_V_DIM + ONES_ROWS
SWA_TILE = 1024
LOG2E = math.log2(math.e)
CONV_HALO = 32
S5_TILE = 256
S5_STEP = 2048
CROSS_TILE = 1024


def _alibi_slope(h, n):
    return 2.0 ** (-8.0 * (h + 1) / n)


def _layer_norm(z, g, b):
    mu = jnp.mean(z, axis=-1, keepdims=True)
    zc = z - mu
    var = jnp.mean(zc * zc, axis=-1, keepdims=True)
    return zc * lax.rsqrt(var + LN_EPS) * g + b


def _dot(a, b):
    return jnp.dot(a, b, preferred_element_type=F32)


def _dot_nt(a, b):
    return lax.dot_general(a, b, (((1,), (1,)), ((), ())), preferred_element_type=F32)


def _params(*sem):
    return pltpu.CompilerParams(dimension_semantics=sem, vmem_limit_bytes=VMEM_LIMIT)


def _resident(shape):
    nd = len(shape)
    return pl.BlockSpec(shape, lambda *_: (0,) * nd, pipeline_mode=pl.Buffered(1))


def _ffn_part(rows, x_ref, wg_ref, wu_ref, wo_ref, g_ref, b_ref, a_ref):
    x = x_ref[rows, :]
    xb = x.astype(BF16)
    for c in range(FFN_DIM // FFN_CHUNK):
        sl = slice(c * FFN_CHUNK, (c + 1) * FFN_CHUNK)
        gate = _dot(xb, wg_ref[:, sl])
        up = _dot(xb, wu_ref[:, sl])
        a_ref[rows, sl] = (gate * jax.nn.sigmoid(gate) * up).astype(BF16)
    f = _dot(a_ref[rows, :], wo_ref[...])
    return _layer_norm(DEEPNORM_ALPHA * x + 0.5 * f, g_ref[...], b_ref[...])


def _ffn_ln_kernel(x_ref, wg_ref, wu_ref, wo_ref, g_ref, b_ref, o_ref, a_ref):
    for r in range(FFN_ROWS // FFN_PART):
        rows = slice(r * FFN_PART, (r + 1) * FFN_PART)
        o_ref[rows, :] = _ffn_part(rows, x_ref, wg_ref, wu_ref, wo_ref, g_ref, b_ref, a_ref)


def _ffn_ln(x, w_in, w_out, g, b):
    n = x.shape[0]
    return pl.pallas_call(
        _ffn_ln_kernel,
        grid=(n // FFN_ROWS,),
        in_specs=[
            pl.BlockSpec((FFN_ROWS, D_MODEL), lambda i: (i, 0)),
            pl.BlockSpec((D_MODEL, FFN_DIM), lambda i: (0, 0), pipeline_mode=pl.Buffered(1)),
            pl.BlockSpec((D_MODEL, FFN_DIM), lambda i: (0, 1), pipeline_mode=pl.Buffered(1)),
            _resident((FFN_DIM, D_MODEL)),
            _resident((1, D_MODEL)),
            _resident((1, D_MODEL)),
        ],
        out_specs=pl.BlockSpec((FFN_ROWS, D_MODEL), lambda i: (i, 0)),
        out_shape=jax.ShapeDtypeStruct((n, D_MODEL), F32),
        scratch_shapes=[pltpu.VMEM((FFN_ROWS, FFN_DIM), BF16)],
        compiler_params=_params("parallel"),
        name="ffn_ln",
    )(x, w_in, w_in, w_out, g, b)


def _ffn_conv_kernel(x_ref, wg_ref, wu_ref, wo_ref, g_ref, b_ref, wc_ref, cw_ref, cb_ref, cg_ref, cbeta_ref,
                     o_ref, oc_ref, a_ref, buf_ref, *, per_seq):
    i = pl.program_id(0)

    @pl.when(i == 0)
    def _():
        buf_ref[FFN_ROWS:, :] = jnp.zeros((CONV_HALO, CONV_WIDTH), F32)

    keep = (i % per_seq != 0).astype(F32)
    buf_ref[0:CONV_HALO, :] = buf_ref[FFN_ROWS:, :] * keep
    first = CONV_HALO - (CONV_K - 1)
    part = FFN_PART
    for r in range(FFN_ROWS // part):
        rows = slice(r * part, (r + 1) * part)
        y = _ffn_part(rows, x_ref, wg_ref, wu_ref, wo_ref, g_ref, b_ref, a_ref)
        o_ref[rows, :] = y
        cu = _dot(y.astype(BF16), wc_ref[...])
        buf_ref[CONV_HALO + r * part:CONV_HALO + (r + 1) * part, :] = (
            cu[:, :CONV_WIDTH] * jax.nn.sigmoid(cu[:, CONV_WIDTH:]))
        win = buf_ref[r * part:r * part + part + CONV_HALO, :]
        shifted = [win] + [pltpu.roll(win, part + CONV_HALO - s, axis=0) for s in range(1, SUBLANES)]
        acc = jnp.zeros((part, CONV_WIDTH), F32)
        for k in range(CONV_K):
            a, s = divmod(first + k, SUBLANES)
            lo = a * SUBLANES
            acc = acc + cw_ref[k:k + 1, :] * shifted[s][lo:lo + part, :]
        z = _layer_norm(acc + cb_ref[...], cg_ref[...], cbeta_ref[...])
        oc_ref[rows, :] = (z * jax.nn.sigmoid(z)).astype(BF16)


def _ffn_ln_conv(x, w_in, w_out, g, b, w_cu, conv_w, conv_b, conv_g, conv_beta, seq):
    n = x.shape[0]
    row = lambda w: pl.BlockSpec((FFN_ROWS, w), lambda i: (i, 0))
    return pl.pallas_call(
        functools.partial(_ffn_conv_kernel, per_seq=seq // FFN_ROWS),
        grid=(n // FFN_ROWS,),
        in_specs=[
            row(D_MODEL),
            pl.BlockSpec((D_MODEL, FFN_DIM), lambda i: (0, 0), pipeline_mode=pl.Buffered(1)),
            pl.BlockSpec((D_MODEL, FFN_DIM), lambda i: (0, 1), pipeline_mode=pl.Buffered(1)),
            _resident((FFN_DIM, D_MODEL)),
            _resident((1, D_MODEL)), _resident((1, D_MODEL)),
            _resident((D_MODEL, CONV_IN)),
            _resident((CONV_K, CONV_WIDTH)),
            _resident((1, CONV_WIDTH)), _resident((1, CONV_WIDTH)), _resident((1, CONV_WIDTH)),
        ],
        out_specs=[row(D_MODEL), row(CONV_WIDTH)],
        out_shape=[jax.ShapeDtypeStruct((n, D_MODEL), F32), jax.ShapeDtypeStruct((n, CONV_WIDTH), BF16)],
        scratch_shapes=[pltpu.VMEM((FFN_ROWS, FFN_DIM), BF16),
                        pltpu.VMEM((FFN_ROWS + CONV_HALO, CONV_WIDTH), F32)],
        compiler_params=_params("arbitrary"),
        name="ffn_ln_conv",
    )(x, w_in, w_in, w_out, g, b, w_cu, conv_w, conv_b, conv_g, conv_beta)


def _mix_proj_kernel(x_ref, wa_ref, wq_ref, ws_ref, wvt_ref, qk_ref, ak_ref, q_ref, su_ref, vt_ref):
    xb = x_ref[...].astype(BF16)
    ha = _dot(xb, wa_ref[...])
    for j in range(2 * DIFF_HEADS):
        qk_ref[j] = ha[:, j * LANES:(j + 1) * LANES].astype(BF16)
    ak_ref[...] = ha[:, 2 * D_QK:].astype(BF16)
    q = _dot(xb, wq_ref[...])
    for h in range(SWA_HEADS):
        q_ref[h] = q[:, h * LANES:(h + 1) * LANES].astype(BF16)
    su_ref[...] = _dot(xb, ws_ref[...])
    vt_ref[...] = _dot_nt(wvt_ref[...], xb).astype(BF16)


def _mix_proj(x, w_att, w_q, w_su, w_v_t, batch, seq):
    n = x.shape[0]
    per_seq = seq // ROW_TILE
    row = lambda w: pl.BlockSpec((ROW_TILE, w), lambda i: (i, 0))
    return pl.pallas_call(
        _mix_proj_kernel,
        grid=(n // ROW_TILE,),
        in_specs=[
            row(D_MODEL),
            _resident((D_MODEL, ATT_COLS)),
            _resident((D_MODEL, SWA_HEADS * LANES)),
            _resident((D_MODEL, S5_WIDTH)),
            _resident((VT_ROWS, D_MODEL)),
        ],
        out_specs=[
            pl.BlockSpec((2 * DIFF_HEADS, ROW_TILE, LANES), lambda i: (0, i, 0)),
            row(A_KV),
            pl.BlockSpec((SWA_HEADS, ROW_TILE, LANES), lambda i: (0, i, 0)),
            row(S5_WIDTH),
            pl.BlockSpec((None, VT_ROWS, ROW_TILE), lambda i: (i // per_seq, 0, i % per_seq)),
        ],
        out_shape=[
            jax.ShapeDtypeStruct((2 * DIFF_HEADS, n, LANES), BF16),
            jax.ShapeDtypeStruct((n, A_KV), BF16),
            jax.ShapeDtypeStruct((SWA_HEADS, n, LANES), BF16),
            jax.ShapeDtypeStruct((n, S5_WIDTH), F32),
            jax.ShapeDtypeStruct((batch, VT_ROWS, seq), BF16),
        ],
        compiler_params=_params("parallel"),
        name="mix_proj",
    )(x, w_att, w_q, w_su, w_v_t)


def _swa_q_weight(w_aq):
    d = w_aq.shape[0]
    w = (w_aq * (LOG2E / math.sqrt(HEAD_DIM))).reshape(d, SWA_KV_HEADS, SWA_REP, 1, HEAD_DIM)
    sel = jnp.eye(SWA_KV_HEADS, dtype=w.dtype).reshape(1, SWA_KV_HEADS, 1, SWA_KV_HEADS, 1)
    return (w * sel).reshape(d, SWA_HEADS * LANES)


def _swa_kernel(sink_ref, bias_ref, q_ref, kc_ref, kp_ref, vc_ref, vp_ref, o_ref):
    i = pl.program_id(1)
    gw = SWA_REP * BLOCK
    k_all = jnp.concatenate([kp_ref[...], kc_ref[...]], axis=0)
    v_all = jnp.concatenate([vp_ref[...], vc_ref[...]], axis=1)
    vx_all = jnp.concatenate([v_all, jnp.ones((ONES_ROWS, v_all.shape[1]), BF16)], axis=0)
    no_prev = jnp.where(i == 0, NEG_INF, 0.0)

    def scores(blk, g):
        lo = blk * BLOCK
        qg = q_ref[g * SWA_REP:(g + 1) * SWA_REP, lo:lo + BLOCK, :].reshape(gw, LANES)
        s = _dot_nt(k_all[lo:lo + 2 * BLOCK], qg) + bias_ref[g]
        if blk == 0:
            s = jnp.concatenate([s[:BLOCK] + no_prev, s[BLOCK:]], axis=0)
        return s

    units = [(blk, g) for blk in range(SWA_TILE // BLOCK) for g in range(SWA_KV_HEADS)]
    s_next = scores(*units[0])
    outs = []
    for u, (blk, g) in enumerate(units):
        s = s_next
        if u + 1 < len(units):
            s_next = scores(*units[u + 1])
        lo = blk * BLOCK
        sink = sink_ref[g]
        m = jnp.maximum(jnp.max(s, axis=0, keepdims=True), sink)
        e = jnp.exp2(s - m).astype(BF16)
        pv = _dot(vx_all[:, lo:lo + 2 * BLOCK], e)
        den = pv[LANES:LANES + 1] + jnp.exp2(sink - m)
        og = pv[g * HEAD_DIM:(g + 1) * HEAD_DIM] * (1.0 / den)
        outs += [og[:, r * BLOCK:(r + 1) * BLOCK] for r in range(SWA_REP)]
        if g == SWA_KV_HEADS - 1:
            o_ref[lo:lo + BLOCK, :] = jnp.concatenate(outs, axis=0).T.astype(BF16)
            outs = []


def _swa_bias():
    kj = jnp.arange(2 * BLOCK)[:, None]
    qi = jnp.arange(BLOCK)[None, :]
    dist = BLOCK + qi - kj
    valid = (dist >= 0) & (dist < BLOCK)
    slopes = jnp.asarray([_alibi_slope(h, SWA_HEADS) * LOG2E for h in range(SWA_HEADS)], F32)
    b = jnp.where(valid[None], -slopes[:, None, None] * dist[None].astype(F32), NEG_INF)
    b = b.reshape(SWA_KV_HEADS, SWA_REP, 2 * BLOCK, BLOCK)
    return jnp.transpose(b, (0, 2, 1, 3)).reshape(SWA_KV_HEADS, 2 * BLOCK, SWA_REP * BLOCK)


def _swa(ak, q_swa, v_t, sinks, batch, seq):
    h3 = ak.reshape(batch, seq, A_KV)
    rv = D_V // A_KV
    per_seq = seq // SWA_TILE
    r = SWA_TILE // BLOCK
    gw = SWA_REP * BLOCK
    sink_rows = jnp.repeat(sinks.astype(F32).reshape(SWA_KV_HEADS, SWA_REP) * LOG2E, BLOCK, axis=1)
    prev = lambda i: jnp.maximum(i * r - 1, 0)
    return pl.pallas_call(
        _swa_kernel,
        grid=(batch, per_seq),
        in_specs=[
            _resident((SWA_KV_HEADS, 1, gw)),
            _resident((SWA_KV_HEADS, 2 * BLOCK, gw)),
            pl.BlockSpec((SWA_HEADS, SWA_TILE, LANES), lambda b, i: (0, b * per_seq + i, 0)),
            pl.BlockSpec((None, SWA_TILE, A_KV), lambda b, i: (b, i, 0)),
            pl.BlockSpec((None, BLOCK, A_KV), lambda b, i: (b, prev(i), 0)),
            pl.BlockSpec((None, A_KV, SWA_TILE), lambda b, i: (b, rv, i)),
            pl.BlockSpec((None, A_KV, BLOCK), lambda b, i: (b, rv, prev(i))),
        ],
        out_specs=pl.BlockSpec((None, SWA_TILE, A_Q), lambda b, i: (b, i, 0)),
        out_shape=jax.ShapeDtypeStruct((batch, seq, A_Q), BF16),
        compiler_params=_params("parallel", "parallel"),
        name="swa",
    )(sink_rows.reshape(SWA_KV_HEADS, 1, gw), _swa_bias(), q_swa, h3, h3, v_t, v_t).reshape(batch * seq, A_Q)


def _diff_kernel(sc_ref, q_ref, k_ref, vt_ref, g_ref, o_ref,
                 qs_ref, vx_ref, b_ref, s_ref, mx_ref, m_ref, acc_ref, *, n_tiles):
    h = pl.program_id(1)
    t = DIFF_TILE
    hf = t // 2
    v_dim = DIFF_V_DIM
    slope = jnp.float32(_alibi_slope(DIFF_HEADS - 1, DIFF_HEADS) * LOG2E)
    for hh in range(DIFF_HEADS - 1):
        slope = jnp.where(h == hh, jnp.float32(_alibi_slope(hh, DIFF_HEADS) * LOG2E), slope)

    vx_ref[:v_dim, :] = vt_ref[...]
    vx_ref[v_dim:, :] = jnp.ones((DIFF_VROWS - v_dim, vx_ref.shape[1]), BF16)
    krow = lax.broadcasted_iota(jnp.int32, (t, t), 0)
    qcol = lax.broadcasted_iota(jnp.int32, (t, t), 1)
    b_ref[0] = slope * krow.astype(F32)
    b_ref[1] = jnp.where(qcol >= krow, slope * krow.astype(F32), NEG_INF)
    lane = lax.broadcasted_iota(jnp.int32, (t, 2 * HEAD_DIM), 1)
    for qt in range(n_tiles):
        q = q_ref[qt * t:(qt + 1) * t, :].astype(F32) * (LOG2E / math.sqrt(HEAD_DIM))
        qs_ref[qt, 0] = jnp.where(lane < HEAD_DIM, q, 0.0).astype(BF16)
        qs_ref[qt, 1] = jnp.where(lane >= HEAD_DIM, q, 0.0).astype(BF16)

    def items_of(qt):
        out = []
        for j in range(qt):
            out += [(qt, j * t, t, 0, t, None, c) for c in range(2)]
        out += [(qt, qt * t, hf, 0, t, (0, 0), c) for c in range(2)]
        out += [(qt, qt * t + hf, hf, hf, hf, (hf, hf), c) for c in range(2)]
        return out

    def scores(item):
        qt, r0, nr, c0, nc, tri, c = item
        bias = b_ref[0, 0:nr, 0:nc] if tri is None else b_ref[1, tri[0]:tri[0] + nr, tri[1]:tri[1] + nc]
        s = _dot_nt(k_ref[r0:r0 + nr, :], qs_ref[qt, c, c0:c0 + nc, :]) + bias
        s_ref[c, 0:nr, 0:nc] = s
        mx_ref[c, :, 0:nc] = jnp.max(s, axis=0, keepdims=True)

    def accumulate(item, first):
        qt, r0, nr, c0, nc, tri, c = item
        mx = mx_ref[c, :, 0:nc]
        e = jnp.exp2(s_ref[c, 0:nr, 0:nc] - mx).astype(BF16)
        pv = _dot(vx_ref[:, r0:r0 + nr], e)
        mxo = mx + slope * float(r0 - (0 if tri is None else tri[0]))
        if first:
            m_ref[c, :, c0:c0 + nc] = mxo
            acc_ref[c, :, c0:c0 + nc] = pv
        else:
            m_old = m_ref[c, :, c0:c0 + nc]
            m_new = jnp.maximum(m_old, mxo)
            m_ref[c, :, c0:c0 + nc] = m_new
            acc_ref[c, :, c0:c0 + nc] = (jnp.exp2(m_old - m_new) * acc_ref[c, :, c0:c0 + nc]
                                         + jnp.exp2(mxo - m_new) * pv)

    def finalize(qt):
        def normalised(c):
            a = acc_ref[c]
            return a[:v_dim] * (1.0 / a[v_dim:v_dim + 1])

        o = (normalised(0) - sc_ref[0] * normalised(1)).T
        o = o * lax.rsqrt(jnp.mean(o * o, axis=-1, keepdims=True) + LN_EPS) * g_ref[...] * sc_ref[1]
        o_ref[qt * t:(qt + 1) * t, :] = o.astype(BF16)

    items = [it for qt in range(n_tiles) for it in items_of(qt)]
    scores(items[0])
    for n, item in enumerate(items):
        if n + 1 < len(items):
            scores(items[n + 1])
        qt = item[0]
        accumulate(item, first=(n < 2 or items[n - 2][0] != qt))
        if n + 1 == len(items) or items[n + 1][0] != qt:
            finalize(qt)


def _diff_attn(qk, v_t, scalars, norm_g, batch, seq):
    h3 = qk.reshape(2 * DIFF_HEADS, batch, seq, LANES)
    w = 2 * HEAD_DIM
    t = DIFF_TILE
    n_tiles = seq // t
    return pl.pallas_call(
        functools.partial(_diff_kernel, n_tiles=n_tiles),
        grid=(batch, DIFF_HEADS),
        in_specs=[
            pl.BlockSpec(memory_space=pltpu.SMEM),
            pl.BlockSpec((None, None, seq, w), lambda b, h: (h, b, 0, 0)),
            pl.BlockSpec((None, None, seq, w), lambda b, h: (DIFF_HEADS + h, b, 0, 0)),
            pl.BlockSpec((None, DIFF_V_DIM, seq), lambda b, h: (b, h, 0)),
            _resident((1, DIFF_V_DIM)),
        ],
        out_specs=pl.BlockSpec((None, None, seq, w), lambda b, h: (h, b, 0, 0)),
        out_shape=jax.ShapeDtypeStruct((DIFF_HEADS, batch, seq, w), BF16),
        scratch_shapes=[
            pltpu.VMEM((n_tiles, 2, t, w), BF16),
            pltpu.VMEM((DIFF_VROWS, seq), BF16),
            pltpu.VMEM((2, t, t), F32),
            pltpu.VMEM((2, t, t), F32),
            pltpu.VMEM((2, 1, t), F32),
            pltpu.VMEM((2, 1, t), F32),
            pltpu.VMEM((2, DIFF_VROWS, t), F32),
        ],
        compiler_params=_params("parallel", "parallel"),
        name="diff_attn",
    )(scalars, h3, h3, v_t, norm_g).reshape(DIFF_HEADS, batch * seq, w)


def _s5_kernel(u_ref, bw_ref, cw_ref, air_ref, aii_ref, apr_ref, api_ref, ah_ref, tri_ref,
               d_ref, gw_ref, gb_ref, o_ref, hr_ref, hi_ref):
    i = pl.program_id(1)

    @pl.when(i == 0)
    def _():
        hr_ref[...] = jnp.zeros_like(hr_ref)
        hi_ref[...] = jnp.zeros_like(hi_ref)

    tri = tri_ref[...]
    nq = S5_WIDTH // LANES
    sw = S5_NSTATE // nq
    tt = S5_TILE

    def in_proj(k, j):
        ub = u_ref[k * tt:(k + 1) * tt, j * LANES:(j + 1) * LANES].astype(BF16)
        return _dot(ub, bw_ref[j])

    def finish(k, ys):
        u = u_ref[k * tt:(k + 1) * tt, :]
        y = jnp.concatenate(ys, axis=1) + d_ref[...] * u
        y = jax.nn.gelu(y, approximate=True)
        gate = _dot(y.astype(BF16), gw_ref[...]) + gb_ref[...]
        o_ref[k * tt:(k + 1) * tt, :] = (y * jax.nn.sigmoid(gate)).astype(BF16)

    units = [(k, j) for k in range(S5_STEP // tt) for j in range(nq)]
    bu_next = in_proj(*units[0])
    prev = None
    ys = []
    for n, (k, j) in enumerate(units):
        sl = slice(j * sw, (j + 1) * sw)
        bu = bu_next
        if n + 1 < len(units):
            bu_next = in_proj(*units[n + 1])
        bur, bui = bu[:, :sw], bu[:, sw:]
        air, aii = air_ref[:, sl], aii_ref[:, sl]
        zr = (air * bur - aii * bui).astype(BF16)
        zi = (air * bui + aii * bur).astype(BF16)
        cr = _dot(tri, zr) + hr_ref[:, sl]
        ci = _dot(tri, zi) + hi_ref[:, sl]
        if prev is not None:
            ys.append(_dot(prev[2], cw_ref[prev[1]]))
            if prev[1] == nq - 1:
                finish(prev[0], ys)
                ys = []
        apr, api = apr_ref[:, sl], api_ref[:, sl]
        xr = apr * cr - api * ci
        xi = apr * ci + api * cr
        lr, li = xr[tt - 1:, :], xi[tt - 1:, :]
        ahr, ahi = ah_ref[0:1, sl], ah_ref[1:2, sl]
        hr_ref[:, sl] = ahr * lr - ahi * li
        hi_ref[:, sl] = ahr * li + ahi * lr
        prev = (k, j, jnp.concatenate([xr, xi], axis=1).astype(BF16))
    ys.append(_dot(prev[2], cw_ref[prev[1]]))
    finish(prev[0], ys)


def _s5_tables(a_re, a_im, log_step, b_re, b_im, c_re, c_im):
    g, p, c = S5_GROUPS, S5_STATE, S5_GROUP
    step = jnp.exp(log_step)[:, None]
    mag = jnp.exp(a_re * step)
    abar_r, abar_i = mag * jnp.cos(a_im * step), mag * jnp.sin(a_im * step)
    den = a_re * a_re + a_im * a_im
    nr, ni = abar_r - 1.0, abar_i
    coef_r = (nr * a_re + ni * a_im) / den
    coef_i = (ni * a_re - nr * a_im) / den
    bbar_r = coef_r[..., None] * b_re - coef_i[..., None] * b_im
    bbar_i = coef_r[..., None] * b_im + coef_i[..., None] * b_re
    nq = S5_WIDTH // LANES
    gq = g // nq
    eye = jnp.eye(gq, dtype=F32)

    def in_slab(bb):
        bb = bb.reshape(nq, gq, p, c)
        return jnp.einsum('qgpc,gh->qgchp', bb, eye).reshape(nq, gq * c, gq * p)

    bw = jnp.concatenate([in_slab(bbar_r), in_slab(bbar_i)], axis=2).astype(BF16)

    def out_slab(cc):
        cc = cc.reshape(nq, gq, c, p)
        return jnp.einsum('qgcp,gh->qgphc', cc, eye).reshape(nq, gq * p, gq * c)

    cw = jnp.concatenate([out_slab(c_re), -out_slab(c_im)], axis=1).astype(BF16)
    half = S5_TILE // 2
    tt = jnp.arange(1 - half, S5_TILE + 1 - half, dtype=F32)[:, None]
    la = (a_re * step).reshape(1, g * p)
    th = (a_im * step).reshape(1, g * p)
    pm, ang = jnp.exp(tt * la), tt * th
    im_ = jnp.exp(-tt * la)
    apr, api = pm * jnp.cos(ang), pm * jnp.sin(ang)
    air, aii = im_ * jnp.cos(ang), -im_ * jnp.sin(ang)
    hm = jnp.exp(half * la)
    ah = jnp.concatenate([hm * jnp.cos(half * th), hm * jnp.sin(half * th)], axis=0)
    return bw, cw, air, aii, apr, api, ah


def _s5(su, tables, d_skip, glu_w, glu_b, batch, seq):
    bw, cw, air, aii, apr, api, ah = tables
    h3 = su.reshape(batch, seq, S5_WIDTH)
    tri = jnp.tril(jnp.ones((S5_TILE, S5_TILE), F32)).astype(BF16)
    tab = (S5_TILE, S5_NSTATE)
    return pl.pallas_call(
        _s5_kernel,
        grid=(batch, seq // S5_STEP),
        in_specs=[
            pl.BlockSpec((None, S5_STEP, S5_WIDTH), lambda b, i: (b, i, 0)),
            _resident(bw.shape), _resident(cw.shape),
            _resident(tab), _resident(tab), _resident(tab), _resident(tab),
            _resident((2, S5_NSTATE)),
            _resident((S5_TILE, S5_TILE)),
            _resident((1, S5_WIDTH)),
            _resident((S5_WIDTH, S5_WIDTH)),
            _resident((1, S5_WIDTH)),
        ],
        out_specs=pl.BlockSpec((None, S5_STEP, S5_WIDTH), lambda b, i: (b, i, 0)),
        out_shape=jax.ShapeDtypeStruct((batch, seq, S5_WIDTH), BF16),
        scratch_shapes=[pltpu.VMEM((1, S5_NSTATE), F32), pltpu.VMEM((1, S5_NSTATE), F32)],
        compiler_params=_params("parallel", "arbitrary"),
        name="s5_ssm",
    )(h3, bw, cw, air, aii, apr, api, ah, tri, d_skip, glu_w, glu_b).reshape(batch * seq, S5_WIDTH)


def _merge_kernel(x_ref, oa_ref, ob_ref, oc_ref, od_ref, wg_ref, pa_ref, pb_ref, pc_ref, pd_ref,
                  wo_ref, g_ref, b_ref, o_ref):
    half = MERGE_PART
    for r in range(MERGE_ROWS // MERGE_PART):
        rows = slice(r * half, (r + 1) * half)
        x = x_ref[rows, :]
        xb = x.astype(BF16)
        merged = jnp.zeros((half, D_MODEL), F32)
        od = jnp.concatenate([od_ref[hd, rows, :] for hd in range(DIFF_HEADS)], axis=1)
        branches = ((oa_ref[rows, :], pa_ref), (ob_ref[rows, :], pb_ref), (oc_ref[rows, :], pc_ref), (od, pd_ref))
        for i, (br, pr_ref) in enumerate(branches):
            gl = _dot(xb, wg_ref[:, i * D_MODEL:(i + 1) * D_MODEL])
            merged = merged + jax.nn.sigmoid(gl) * _dot(br, pr_ref[...])
        m = _dot(merged.astype(BF16), wo_ref[...])
        o_ref[rows, :] = _layer_norm(DEEPNORM_ALPHA * x + m, g_ref[...], b_ref[...])


def _merge(x, oa, ob, oc, od, w_gate, pa, pb, pc, pd, w_out, g, b):
    n = x.shape[0]
    row = lambda w: pl.BlockSpec((MERGE_ROWS, w), lambda i: (i, 0))
    return pl.pallas_call(
        _merge_kernel,
        grid=(n // MERGE_ROWS,),
        in_specs=[
            row(D_MODEL), row(A_Q), row(S5_WIDTH), row(CONV_WIDTH),
            pl.BlockSpec((DIFF_HEADS, MERGE_ROWS, DIFF_V_DIM), lambda i: (0, i, 0)),
            _resident((D_MODEL, GATE_COLS)),
            _resident((A_Q, D_MODEL)), _resident((S5_WIDTH, D_MODEL)),
            _resident((CONV_WIDTH, D_MODEL)), _resident((D_V, D_MODEL)),
            _resident((D_MODEL, D_MODEL)),
            _resident((1, D_MODEL)), _resident((1, D_MODEL)),
        ],
        out_specs=row(D_MODEL),
        out_shape=jax.ShapeDtypeStruct((n, D_MODEL), F32),
        compiler_params=_params("parallel"),
        name="gated_merge",
    )(x, oa, ob, oc, od, w_gate, pa, pb, pc, pd, w_out, g, b)


def _mem_kv_kernel(m_ref, g_ref, b_ref, w_ref, o_ref):
    mn = _layer_norm(m_ref[...], g_ref[...], b_ref[...])
    o_ref[...] = _dot(mn.astype(BF16), w_ref[...]).astype(BF16)


def _mem_kv(mem, g, b, wkv):
    n = mem.shape[0]
    return pl.pallas_call(
        _mem_kv_kernel,
        grid=(DEPTH, n // ROW_TILE),
        in_specs=[
            pl.BlockSpec((ROW_TILE, D_MODEL), lambda l, i: (i, 0)),
            _resident((1, D_MODEL)), _resident((1, D_MODEL)),
            pl.BlockSpec((None, D_MODEL, 2 * CROSS_WIDTH), lambda l, i: (l, 0, 0)),
        ],
        out_specs=pl.BlockSpec((None, ROW_TILE, 2 * CROSS_WIDTH), lambda l, i: (l, i, 0)),
        out_shape=jax.ShapeDtypeStruct((DEPTH, n, 2 * CROSS_WIDTH), BF16),
        compiler_params=_params("parallel", "parallel"),
        name="mem_kv",
    )(mem, g, b, wkv)


def _cross_kernel(x_ref, kv_ref, wq_ref, wo_ref, g_ref, b_ref, o_ref):
    x = x_ref[...]
    q = _dot(x.astype(BF16), wq_ref[...]).astype(BF16)
    kv = kv_ref[...]
    scale = 1.0 / math.sqrt(CROSS_HEAD_DIM)
    outs = []

    def scores(h):
        sl = slice(h * CROSS_HEAD_DIM, (h + 1) * CROSS_HEAD_DIM)
        return _dot_nt(q[:, sl], kv[:, sl]) * scale

    s_next = scores(0)
    for h in range(CROSS_HEADS):
        s = s_next
        if h + 1 < CROSS_HEADS:
            s_next = scores(h + 1)
        p = jnp.exp(s - jnp.max(s, axis=-1, keepdims=True))
        p = p / jnp.sum(p, axis=-1, keepdims=True)
        outs.append(_dot(p.astype(BF16), kv[:, CROSS_WIDTH + h * CROSS_HEAD_DIM:CROSS_WIDTH + (h + 1) * CROSS_HEAD_DIM]))
    o = jnp.concatenate(outs, axis=1).astype(BF16)
    c = _dot(o, wo_ref[...])
    o_ref[...] = _layer_norm(DEEPNORM_ALPHA * x + c, g_ref[...], b_ref[...])


def _cross_attn(x, kv, wq, wo, g, b, batch, seq):
    x3 = x.reshape(batch, seq, D_MODEL)
    kv3 = kv.reshape(batch, MEM_LEN, 2 * CROSS_WIDTH)
    return pl.pallas_call(
        _cross_kernel,
        grid=(batch, seq // CROSS_TILE),
        in_specs=[
            pl.BlockSpec((None, CROSS_TILE, D_MODEL), lambda b, i: (b, i, 0)),
            pl.BlockSpec((None, MEM_LEN, 2 * CROSS_WIDTH), lambda b, i: (b, 0, 0)),
            _resident((D_MODEL, CROSS_WIDTH)),
            _resident((CROSS_WIDTH, D_MODEL)),
            _resident((1, D_MODEL)), _resident((1, D_MODEL)),
        ],
        out_specs=pl.BlockSpec((None, CROSS_TILE, D_MODEL), lambda b, i: (b, i, 0)),
        out_shape=jax.ShapeDtypeStruct((batch, seq, D_MODEL), F32),
        compiler_params=_params("parallel", "parallel"),
        name="cross_attn",
    )(x3, kv3, wq, wo, g, b).reshape(batch * seq, D_MODEL)


def _row(v):
    return v.reshape(1, -1).astype(F32)


def kernel(x, mem, ffn1_w_in, ffn1_w_out, ffn1_ln_g, ffn1_ln_b, mix_w_in, swa_sinks, swa_proj, s5_a_re, s5_a_im, s5_log_step, s5_b_re, s5_b_im, s5_c_re, s5_c_im, s5_d, s5_glu_w, s5_glu_b, s5_proj, conv_w, conv_b, conv_ln_g, conv_ln_b, conv_proj, diff_lq1, diff_lk1, diff_lq2, diff_lk2, diff_norm_g, diff_proj, mix_w_out, mix_ln_g, mix_ln_b, mem_ln_g, mem_ln_b, cross_wq, cross_wkv, cross_wo, cross_ln_g, cross_ln_b, ffn2_w_in, ffn2_w_out, ffn2_ln_g, ffn2_ln_b):
    batch, seq, _ = x.shape
    n = batch * seq
    assert seq % DIFF_TILE == 0 and seq % ROW_TILE == 0 and seq % CROSS_TILE == 0
    assert seq % S5_STEP == 0 and seq % SWA_TILE == 0 and (batch * MEM_LEN) % ROW_TILE == 0
    h = x.reshape(n, D_MODEL)
    kv_all = _mem_kv(mem.reshape(batch * MEM_LEN, D_MODEL), _row(mem_ln_g), _row(mem_ln_b),
                     cross_wkv.astype(BF16))
    for l in range(DEPTH):
        h, o_c = _ffn_ln_conv(h, ffn1_w_in[l].astype(BF16), ffn1_w_out[l].astype(BF16),
                               _row(ffn1_ln_g[l]), _row(ffn1_ln_b[l]),
                               mix_w_in[l][:, OFF_CU:OFF_CU + CONV_IN].astype(BF16), conv_w[l], _row(conv_b[l]),
                               _row(conv_ln_g[l]), _row(conv_ln_b[l]), seq)
        lambda_init = 0.8 - 0.6 * math.exp(-0.3 * l)
        w = mix_w_in[l]
        cols = lambda off, width: w[:, off:off + width]
        w_att = jnp.concatenate([cols(OFF_DQ, D_QK), cols(OFF_DK, D_QK), cols(OFF_AK, A_KV)], axis=1).astype(BF16)
        w_q = _swa_q_weight(cols(OFF_AQ, A_Q)).astype(BF16)
        w_v_t = jnp.concatenate([cols(OFF_DV, D_V), cols(OFF_AV, A_KV)], axis=1).T.astype(BF16)
        w_gate = cols(OFF_GL, GATE_COLS).astype(BF16)
        qk, ak, q_swa, su, v_t = _mix_proj(h, w_att, w_q, cols(OFF_SU, S5_WIDTH).astype(BF16), w_v_t, batch, seq)
        o_a = _swa(ak, q_swa, v_t, swa_sinks[l], batch, seq)
        tables = _s5_tables(s5_a_re[l], s5_a_im[l], s5_log_step[l], s5_b_re[l], s5_b_im[l], s5_c_re[l], s5_c_im[l])
        o_b = _s5(su, tables, _row(s5_d[l]), s5_glu_w[l].astype(BF16), _row(s5_glu_b[l]), batch, seq)
        lam = (jnp.exp(jnp.sum(diff_lq1[l] * diff_lk1[l])) - jnp.exp(jnp.sum(diff_lq2[l] * diff_lk2[l]))
               + lambda_init).astype(F32)
        o_d = _diff_attn(qk, v_t, jnp.stack([lam, jnp.float32(1.0 - lambda_init)]), _row(diff_norm_g[l]), batch, seq)
        h = _merge(h, o_a, o_b, o_c, o_d, w_gate, swa_proj[l].astype(BF16), s5_proj[l].astype(BF16),
                   conv_proj[l].astype(BF16), diff_proj[l].astype(BF16), mix_w_out[l].astype(BF16),
                   _row(mix_ln_g[l]), _row(mix_ln_b[l]))
        h = _cross_attn(h, kv_all[l], cross_wq[l].astype(BF16), cross_wo[l].astype(BF16),
                        _row(cross_ln_g[l]), _row(cross_ln_b[l]), batch, seq)
        h = _ffn_ln(h, ffn2_w_in[l].astype(BF16), ffn2_w_out[l].astype(BF16),
                    _row(ffn2_ln_g[l]), _row(ffn2_ln_b[l]))
    return h.reshape(batch, seq, D_MODEL)
```
